```python
import jax, jax.numpy as jnp
from jax import lax
import numpy as np

D_MODEL = 2048
BATCH = 2
SEQ = 4096
DEPTH = 1

SGU_WIDTH = D_MODEL // 2
SGU_GROUP_DIM = 128
SGU_GROUPS = SGU_WIDTH // SGU_GROUP_DIM
SGU_CHUNK = 128
RWKV_WIDTH = D_MODEL // 2
RWKV_HEAD_DIM = 64
RWKV_HEADS = RWKV_WIDTH // RWKV_HEAD_DIM
DECAY_LORA = 64
AAA_LORA = 64
GATE_LORA = 160
RWKV_FEAT = 3 * RWKV_WIDTH + 2 * DECAY_LORA + 2 * AAA_LORA + GATE_LORA
IN_WIDTH = 2 * SGU_WIDTH + RWKV_FEAT + 2 * D_MODEL
D_FF = ((8 * D_MODEL // 3 + 255) // 256) * 256
CONV_WIDTH = 3
NORM_EPS = 1e-6
LN_EPS = 1e-5
GN_EPS = 64e-5

kernel_name = 'hybrid_sgu_rwkv7_convglu_encoder'

F32 = jnp.float32


def rms_norm(x, g):
    xf = x.astype(F32)
    y = xf * lax.rsqrt(jnp.mean(xf * xf, axis=-1, keepdims=True) + NORM_EPS)
    return (y * g.astype(F32)).astype(x.dtype)


def layer_norm(x, g, b):
    xf = x.astype(F32)
    mu = jnp.mean(xf, axis=-1, keepdims=True)
    var = jnp.mean(jnp.square(xf - mu), axis=-1, keepdims=True)
    y = (xf - mu) * lax.rsqrt(var + LN_EPS)
    return (y * g.astype(F32) + b.astype(F32)).astype(x.dtype)


def split_cols(t, sizes):
    out, o = [], 0
    for s in sizes:
        out.append(t[..., o:o + s])
        o += s
    return out


def shift_prev(x):
    return jnp.pad(x[:, :-1], ((0, 0), (1, 0), (0, 0)))


def shift_next(x):
    return jnp.pad(x[:, 1:], ((0, 0), (0, 1), (0, 0)))


def sgu_mixer(u, v, ln_g, ln_b, w_s, b_s):
    u = jax.nn.gelu(u, approximate=False)
    v = layer_norm(jax.nn.gelu(v, approximate=False), ln_g, ln_b)
    bsz, seq, _ = v.shape
    v = v.reshape(bsz, seq // SGU_CHUNK, SGU_CHUNK, SGU_GROUPS, SGU_GROUP_DIM)
    mixed = jnp.einsum('gij,bcjgd->bcigd', w_s, v) + b_s.T[None, None, :, :, None]
    return u * mixed.reshape(bsz, seq, SGU_WIDTH)


def wkv7_scan(r, w, k, v, kk, a, reverse):
    bsz, _, h, n = r.shape
    xs = tuple(jnp.moveaxis(t.astype(F32), 1, 0) for t in (r, w, k, v, kk, a))

    def step(state, inp):
        r_t, w_t, k_t, v_t, kk_t, a_t = inp
        sa = jnp.einsum('bhvk,bhk->bhv', state, kk_t)
        state = (state * w_t[:, :, None, :]
                 - sa[..., None] * (kk_t * a_t)[:, :, None, :]
                 + v_t[..., None] * k_t[:, :, None, :])
        return state, jnp.einsum('bhvk,bhk->bhv', state, r_t)

    _, out = lax.scan(step, jnp.zeros((bsz, h, n, n), F32), xs, reverse=reverse)
    return jnp.moveaxis(out, 0, 1)


def rwkv_direction(r, k, v, kk, w_lo, a_lo, w0, w2, a0, a2, k_a, reverse):
    bsz, seq, _ = r.shape
    heads = lambda t: t.reshape(bsz, seq, RWKV_HEADS, RWKV_HEAD_DIM)
    w_raw = -jax.nn.softplus(-(w0 + jnp.tanh(w_lo) @ w2)) - 0.5
    decay = jnp.exp(-jnp.exp(w_raw.astype(F32)))
    a = jax.nn.sigmoid(a0 + a_lo @ a2)
    k_dir = k * (1 + (a - 1) * k_a)
    o = wkv7_scan(heads(r), heads(decay), heads(k_dir), heads(v), kk, heads(a), reverse)
    return o, k_dir


def head_group_norm(o, g, b):
    bsz, seq, _, _ = o.shape
    of = o.astype(F32)
    mu = jnp.mean(of, axis=-1, keepdims=True)
    var = jnp.mean(jnp.square(of - mu), axis=-1, keepdims=True)
    y = ((of - mu) * lax.rsqrt(var + GN_EPS)).reshape(bsz, seq, RWKV_WIDTH)
    return (y * g.astype(F32) + b.astype(F32)).astype(g.dtype)


def rwkv_mixer(feat, mu_prev, mu_next, w0_f, w2_f, a0_f, a2_f, w0_b, w2_b, a0_b, a2_b,
               k_k, k_a, r_k, g2, gn_g, gn_b):
    bsz, seq, _ = feat.shape
    xs = feat + mu_prev * (shift_prev(feat) - feat) + mu_next * (shift_next(feat) - feat)
    r, k, v, wlo_f, wlo_b, alo_f, alo_b, glo = split_cols(
        xs, (RWKV_WIDTH, RWKV_WIDTH, RWKV_WIDTH, DECAY_LORA, DECAY_LORA, AAA_LORA, AAA_LORA, GATE_LORA))
    heads = lambda t: t.reshape(bsz, seq, RWKV_HEADS, RWKV_HEAD_DIM)
    kkf = heads(k * k_k).astype(F32)
    kk = kkf * lax.rsqrt(jnp.maximum(jnp.sum(kkf * kkf, axis=-1, keepdims=True), 1e-24))
    o_f, k_f = rwkv_direction(r, k, v, kk, wlo_f, alo_f, w0_f, w2_f, a0_f, a2_f, k_a, False)
    o_b, k_b = rwkv_direction(r, k, v, kk, wlo_b, alo_b, w0_b, w2_b, a0_b, a2_b, k_a, True)
    o = head_group_norm(o_f + o_b, gn_g, gn_b)
    bonus = jnp.sum(heads(r * (k_f + k_b) * r_k), axis=-1, keepdims=True) * heads(v)
    g = jax.nn.sigmoid(glo) @ g2
    return (o + bonus.reshape(bsz, seq, RWKV_WIDTH)) * g


def dwconv3_centred(x, w, b):
    xp = jnp.pad(x, ((0, 0), (1, 1), (0, 0)))
    return xp[:, :-2] * w[0] + xp[:, 1:-1] * w[1] + xp[:, 2:] * w[2] + b


def hybrid_layer(x, norm1_g, w_in, sgu_ln_g, sgu_ln_b, sgu_w, sgu_b,
                 rwkv_mu_prev, rwkv_mu_next, rwkv_w0_f, rwkv_w2_f, rwkv_a0_f, rwkv_a2_f,
                 rwkv_w0_b, rwkv_w2_b, rwkv_a0_b, rwkv_a2_b, rwkv_k_k, rwkv_k_a, rwkv_r_k,
                 rwkv_g2, rwkv_gn_g, rwkv_gn_b, w_proj_a, w_proj_b, w_out,
                 norm2_g, ffn_w_gate, ffn_w_up, ffn_conv_w, ffn_conv_b, ffn_w_down):
    h = rms_norm(x, norm1_g)
    proj = h @ w_in
    u_a, v_a, feat_b, gate_a, gate_b = split_cols(
        proj, (SGU_WIDTH, SGU_WIDTH, RWKV_FEAT, D_MODEL, D_MODEL))
    y_a = sgu_mixer(u_a, v_a, sgu_ln_g, sgu_ln_b, sgu_w, sgu_b)
    y_b = rwkv_mixer(feat_b, rwkv_mu_prev, rwkv_mu_next, rwkv_w0_f, rwkv_w2_f, rwkv_a0_f, rwkv_a2_f,
                     rwkv_w0_b, rwkv_w2_b, rwkv_a0_b, rwkv_a2_b, rwkv_k_k, rwkv_k_a, rwkv_r_k,
                     rwkv_g2, rwkv_gn_g, rwkv_gn_b)
    merged = jax.nn.sigmoid(gate_a) * (y_a @ w_proj_a) + jax.nn.sigmoid(gate_b) * (y_b @ w_proj_b)
    x = x + merged @ w_out
    h2 = rms_norm(x, norm2_g)
    gt = dwconv3_centred(h2 @ ffn_w_gate, ffn_conv_w, ffn_conv_b)
    x = x + (jax.nn.silu(gt) * (h2 @ ffn_w_up)) @ ffn_w_down
    return x


def setup_inputs(seed: int = 0) -> dict:
    key = jax.random.key(seed)
    ks = iter(jax.random.split(key, 40))
    nrm = lambda shape, s: jax.random.normal(next(ks), shape, F32) * s
    uni = lambda shape, lo, hi: jax.random.uniform(next(ks), shape, F32, lo, hi)
    L = DEPTH
    return {
        'x': nrm((BATCH, SEQ, D_MODEL), 1.0),
        'norm1_g': 1.0 + nrm((L, D_MODEL), 0.02),
        'w_in': nrm((L, D_MODEL, IN_WIDTH), D_MODEL ** -0.5),
        'sgu_ln_g': 1.0 + nrm((L, SGU_WIDTH), 0.02),
        'sgu_ln_b': nrm((L, SGU_WIDTH), 0.02),
        'sgu_w': nrm((L, SGU_GROUPS, SGU_CHUNK, SGU_CHUNK), 0.5 * SGU_CHUNK ** -0.5),
        'sgu_b': 1.0 + nrm((L, SGU_GROUPS, SGU_CHUNK), 0.1),
        'rwkv_mu_prev': uni((L, RWKV_FEAT), 0.0, 0.5),
        'rwkv_mu_next': uni((L, RWKV_FEAT), 0.0, 0.5),
        'rwkv_w0_f': uni((L, RWKV_WIDTH), -5.0, 0.5),
        'rwkv_w2_f': nrm((L, DECAY_LORA, RWKV_WIDTH), 0.3 * DECAY_LORA ** -0.5),
        'rwkv_a0_f': nrm((L, RWKV_WIDTH), 0.1),
        'rwkv_a2_f': nrm((L, AAA_LORA, RWKV_WIDTH), 0.3 * AAA_LORA ** -0.5),
        'rwkv_w0_b': uni((L, RWKV_WIDTH), -5.0, 0.5),
        'rwkv_w2_b': nrm((L, DECAY_LORA, RWKV_WIDTH), 0.3 * DECAY_LORA ** -0.5),
        'rwkv_a0_b': nrm((L, RWKV_WIDTH), 0.1),
        'rwkv_a2_b': nrm((L, AAA_LORA, RWKV_WIDTH), 0.3 * AAA_LORA ** -0.5),
        'rwkv_k_k': 0.85 + nrm((L, RWKV_WIDTH), 0.05),
        'rwkv_k_a': 1.0 + nrm((L, RWKV_WIDTH), 0.05),
        'rwkv_r_k': nrm((L, RWKV_WIDTH), 0.1),
        'rwkv_g2': nrm((L, GATE_LORA, RWKV_WIDTH), GATE_LORA ** -0.5),
        'rwkv_gn_g': 1.0 + nrm((L, RWKV_WIDTH), 0.02),
        'rwkv_gn_b': nrm((L, RWKV_WIDTH), 0.02),
        'w_proj_a': nrm((L, SGU_WIDTH, D_MODEL), SGU_WIDTH ** -0.5),
        'w_proj_b': nrm((L, RWKV_WIDTH, D_MODEL), RWKV_WIDTH ** -0.5),
        'w_out': nrm((L, D_MODEL, D_MODEL), D_MODEL ** -0.5),
        'norm2_g': 1.0 + nrm((L, D_MODEL), 0.02),
        'ffn_w_gate': nrm((L, D_MODEL, D_FF), D_MODEL ** -0.5),
        'ffn_w_up': nrm((L, D_MODEL, D_FF), D_MODEL ** -0.5),
        'ffn_conv_w': nrm((L, CONV_WIDTH, D_FF), CONV_WIDTH ** -0.5),
        'ffn_conv_b': nrm((L, D_FF), 0.02),
        'ffn_w_down': nrm((L, D_FF, D_MODEL), D_FF ** -0.5),
        'norm_f_g': 1.0 + nrm((D_MODEL,), 0.02),
    }


def reference(x, norm1_g, w_in, sgu_ln_g, sgu_ln_b, sgu_w, sgu_b,
              rwkv_mu_prev, rwkv_mu_next, rwkv_w0_f, rwkv_w2_f, rwkv_a0_f, rwkv_a2_f,
              rwkv_w0_b, rwkv_w2_b, rwkv_a0_b, rwkv_a2_b, rwkv_k_k, rwkv_k_a, rwkv_r_k,
              rwkv_g2, rwkv_gn_g, rwkv_gn_b, w_proj_a, w_proj_b, w_out,
              norm2_g, ffn_w_gate, ffn_w_up, ffn_conv_w, ffn_conv_b, ffn_w_down, norm_f_g):
    for d in range(DEPTH):
        x = hybrid_layer(x, norm1_g[d], w_in[d], sgu_ln_g[d], sgu_ln_b[d], sgu_w[d], sgu_b[d],
                         rwkv_mu_prev[d], rwkv_mu_next[d], rwkv_w0_f[d], rwkv_w2_f[d],
                         rwkv_a0_f[d], rwkv_a2_f[d], rwkv_w0_b[d], rwkv_w2_b[d],
                         rwkv_a0_b[d], rwkv_a2_b[d], rwkv_k_k[d], rwkv_k_a[d], rwkv_r_k[d],
                         rwkv_g2[d], rwkv_gn_g[d], rwkv_gn_b[d], w_proj_a[d], w_proj_b[d], w_out[d],
                         norm2_g[d], ffn_w_gate[d], ffn_w_up[d], ffn_conv_w[d], ffn_conv_b[d],
                         ffn_w_down[d])
    return rms_norm(x, norm_f_g)
```

```python
import functools

import jax
import jax.numpy as jnp
from jax import lax
from jax.experimental import pallas as pl
from jax.experimental.pallas import tpu as pltpu

F32 = jnp.float32
BF16 = jnp.bfloat16

NORM_EPS = 1e-6
LN_EPS = 1e-5
GN_EPS = 64e-5

HEAD_DIM = 64
LANES = 128
SGU_CHUNK = 128
SCAN_CHUNK = 64
VMEM_LIMIT = 48 * 1024 * 1024

HI = lax.Precision.HIGHEST


def _cparams(*sem):
    return pltpu.CompilerParams(dimension_semantics=sem, vmem_limit_bytes=VMEM_LIMIT)


def _proj_kernel(x_ref, g_ref, w_ref, o_ref, h_ref, *, act):
    @pl.when(pl.program_id(1) == 0)
    def _():
        x = x_ref[...]
        ms = jnp.mean(x * x, axis=-1, keepdims=True)
        h_ref[...] = (x * lax.rsqrt(ms + NORM_EPS) * g_ref[...]).astype(BF16)

    acc = jnp.dot(h_ref[...], w_ref[...], preferred_element_type=F32)
    if act == "sigmoid":
        acc = jax.nn.sigmoid(acc)
    o_ref[...] = acc.astype(o_ref.dtype)


def _norm_proj(x, g, w, out_dtype, act=None, tm=1024, tn=512):
    t, d = x.shape
    n = w.shape[1]
    return pl.pallas_call(
        functools.partial(_proj_kernel, act=act),
        grid=(t // tm, n // tn),
        in_specs=[
            pl.BlockSpec((tm, d), lambda i, j: (i, 0)),
            pl.BlockSpec((1, d), lambda i, j: (0, 0)),
            pl.BlockSpec((d, tn), lambda i, j: (0, j)),
        ],
        out_specs=pl.BlockSpec((tm, tn), lambda i, j: (i, j)),
        out_shape=jax.ShapeDtypeStruct((t, n), out_dtype),
        scratch_shapes=[pltpu.VMEM((tm, d), BF16)],
        compiler_params=_cparams("parallel", "arbitrary"),
        name="norm_proj",
    )(x, g, w)


def _gelu(x):
    return 0.5 * x * (1.0 + lax.erf(x * (2.0 ** -0.5)))


def _sgu_kernel(uv_ref, lng_ref, lnb_ref, ws_ref, bs_ref, o_ref, *, width, groups):
    tm = uv_ref.shape[0]
    gv = _gelu(uv_ref[:, width:])
    mu = jnp.mean(gv, axis=-1, keepdims=True)
    vc = gv - mu
    var = jnp.mean(vc * vc, axis=-1, keepdims=True)
    vn = (vc * lax.rsqrt(var + LN_EPS) * lng_ref[...] + lnb_ref[...]).astype(BF16)
    gd = width // groups
    for c in range(tm // SGU_CHUNK):
        rows = slice(c * SGU_CHUNK, (c + 1) * SGU_CHUNK)
        for g in range(groups):
            cols = slice(g * gd, (g + 1) * gd)
            mixed = jnp.dot(ws_ref[g], vn[rows, cols], preferred_element_type=F32)
            gu = _gelu(uv_ref[rows, cols])
            o_ref[rows, cols] = (gu * (mixed + bs_ref[:, cols])).astype(o_ref.dtype)


def _sgu(uv, ln_g, ln_b, ws, bs_full, tm=256):
    t, w2 = uv.shape
    width = w2 // 2
    groups = ws.shape[0]
    return pl.pallas_call(
        functools.partial(_sgu_kernel, width=width, groups=groups),
        grid=(t // tm,),
        in_specs=[
            pl.BlockSpec((tm, w2), lambda i: (i, 0)),
            pl.BlockSpec((1, width), lambda i: (0, 0)),
            pl.BlockSpec((1, width), lambda i: (0, 0)),
            pl.BlockSpec(ws.shape, lambda i: (0, 0, 0)),
            pl.BlockSpec(bs_full.shape, lambda i: (0, 0)),
        ],
        out_specs=pl.BlockSpec((tm, width), lambda i: (i, 0)),
        out_shape=jax.ShapeDtypeStruct((t, width), BF16),
        compiler_params=_cparams("parallel"),
        name="sgu_mixer",
    )(uv, ln_g, ln_b, ws, bs_full)


def _head_block_ones(scale):
    r = lax.broadcasted_iota(jnp.int32, (LANES, LANES), 0) // HEAD_DIM
    c = lax.broadcasted_iota(jnp.int32, (LANES, LANES), 1) // HEAD_DIM
    return jnp.where(r == c, scale, 0.0).astype(F32)


def _head_sum(x, bd):
    return jnp.dot(x, bd, precision=HI, preferred_element_type=F32)


def _shifted_rows(x, before_ref, after_ref, first_row_of_seq, last_row_of_seq):
    tm = x.shape[0]
    row = lax.broadcasted_iota(jnp.int32, (tm, 1), 0)
    prev = pltpu.roll(x, 1, 0)
    prev = jnp.where(row == 0, before_ref[7:8, :], prev)
    prev = jnp.where(jnp.logical_and(row == 0, first_row_of_seq), 0.0, prev)
    nxt = pltpu.roll(x, tm - 1, 0)
    nxt = jnp.where(row == tm - 1, after_ref[0:1, :], nxt)
    nxt = jnp.where(jnp.logical_and(row == tm - 1, last_row_of_seq), 0.0, nxt)
    return prev, nxt


def _prep_kernel(f_ref, fb_ref, fa_ref, mup_ref, mun_ref, w0_ref, w2_ref, a0_ref, a2_ref,
                 g2_ref, kk_ref, ka_ref, rk_ref,
                 sh_ref, dr_ref, bonus_ref, gg_ref, *, seq, width):
    tm = f_ref.shape[0]
    i = pl.program_id(0)
    tiles_per_seq = seq // tm
    first = (i % tiles_per_seq) == 0
    last = (i % tiles_per_seq) == tiles_per_seq - 1
    x = f_ref[...]
    prev, nxt = _shifted_rows(x, fb_ref, fa_ref, first, last)
    xs = x + mup_ref[...] * (prev - x) + mun_ref[...] * (nxt - x)

    r = xs[:, 0:width]
    k = xs[:, width:2 * width]
    v = xs[:, 2 * width:3 * width]
    o = 3 * width
    lo_w = xs[:, o:o + LANES]
    lo_a = xs[:, o + LANES:o + 2 * LANES]
    lo_g = xs[:, o + 2 * LANES:]

    wpre = w0_ref[...] + jnp.dot(jnp.tanh(lo_w), w2_ref[...], precision=HI,
                                 preferred_element_type=F32)
    z = -wpre
    softplus = jnp.maximum(z, 0.0) + jnp.log(1.0 + jnp.exp(-jnp.abs(z)))
    lw = -jnp.exp(-softplus - 0.5)
    a = jax.nn.sigmoid(a0_ref[...] + jnp.dot(lo_a, a2_ref[...], precision=HI,
                                             preferred_element_type=F32))
    gg_ref[...] = jnp.dot(jax.nn.sigmoid(lo_g), g2_ref[...], precision=HI,
                          preferred_element_type=F32)

    bd = _head_block_ones(1.0)
    kk = k * kk_ref[...]
    sh_ref[0] = r
    sh_ref[1] = v
    for j in range(width // LANES):
        cols = slice(j * LANES, (j + 1) * LANES)
        kkj = kk[:, cols]
        ss = _head_sum(kkj * kkj, bd)
        kkn = kkj * lax.rsqrt(jnp.maximum(ss, 1e-24))
        sh_ref[2, :, cols] = kkn
        ksum = jnp.zeros_like(kkj)
        for d in range(2):
            dcols = slice(d * width + j * LANES, d * width + (j + 1) * LANES)
            a_d = a[:, dcols]
            k_d = k[:, cols] * (1.0 + (a_d - 1.0) * ka_ref[:, cols])
            dr_ref[d, 0, :, cols] = lw[:, dcols]
            dr_ref[d, 1, :, cols] = kkn * a_d
            dr_ref[d, 2, :, cols] = k_d
            ksum = ksum + k_d
        bsum = _head_sum(r[:, cols] * ksum * rk_ref[:, cols], bd)
        bonus_ref[:, cols] = bsum * v[:, cols]


def _rwkv_prep(feat, seq, width, mup, mun, w0c, w2c, a0c, a2c, g2p, k_k, k_a, r_k, tm=128):
    t, fw = feat.shape
    nb8 = t // 8
    row = lambda i: (0, 0)
    return pl.pallas_call(
        functools.partial(_prep_kernel, seq=seq, width=width),
        grid=(t // tm,),
        in_specs=[
            pl.BlockSpec((tm, fw), lambda i: (i, 0)),
            pl.BlockSpec((8, fw), lambda i: (jnp.maximum(i * (tm // 8) - 1, 0), 0)),
            pl.BlockSpec((8, fw), lambda i: (jnp.minimum((i + 1) * (tm // 8), nb8 - 1), 0)),
            pl.BlockSpec((1, fw), row),
            pl.BlockSpec((1, fw), row),
            pl.BlockSpec(w0c.shape, row),
            pl.BlockSpec(w2c.shape, row),
            pl.BlockSpec(a0c.shape, row),
            pl.BlockSpec(a2c.shape, row),
            pl.BlockSpec(g2p.shape, row),
            pl.BlockSpec((1, width), row),
            pl.BlockSpec((1, width), row),
            pl.BlockSpec((1, width), row),
        ],
        out_specs=[
            pl.BlockSpec((3, tm, width), lambda i: (0, i, 0)),
            pl.BlockSpec((2, 3, tm, width), lambda i: (0, 0, i, 0)),
            pl.BlockSpec((tm, width), lambda i: (i, 0)),
            pl.BlockSpec((tm, width), lambda i: (i, 0)),
        ],
        out_shape=[
            jax.ShapeDtypeStruct((3, t, width), F32),
            jax.ShapeDtypeStruct((2, 3, t, width), F32),
            jax.ShapeDtypeStruct((t, width), F32),
            jax.ShapeDtypeStruct((t, width), F32),
        ],
        compiler_params=_cparams("parallel"),
        name="rwkv_prep",
    )(feat, feat, feat, mup, mun, w0c, w2c, a0c, a2c, g2p, k_k, k_a, r_k)


def _hat(x, lane_is_head0):
    return jnp.concatenate([jnp.where(lane_is_head0, x, 0.0), jnp.where(lane_is_head0, 0.0, x)],
                           axis=0)


def _scan_kernel(sh_ref, dr_ref, o_ref, s_ref, *, prec):
    c = SCAN_CHUNK
    width = sh_ref.shape[2]
    sign = 1 - 2 * pl.program_id(0)

    @pl.when(pl.program_id(2) == 0)
    def _():
        s_ref[...] = jnp.zeros_like(s_ref)

    dot = functools.partial(jnp.dot, precision=prec, preferred_element_type=F32)
    dot_nt = lambda p, q: lax.dot_general(p, q, (((1,), (1,)), ((), ())), precision=prec,
                                          preferred_element_type=F32)
    dot_tn = lambda p, q: lax.dot_general(p, q, (((0,), (0,)), ((), ())), precision=prec,
                                          preferred_element_type=F32)

    ti = lax.broadcasted_iota(jnp.int32, (c, c), 0)
    tj = lax.broadcasted_iota(jnp.int32, (c, c), 1)
    cum = jnp.where(sign * (ti - tj) >= 0, 1.0, 0.0).astype(F32)
    hi_ = lax.broadcasted_iota(jnp.int32, (2 * c, 2 * c), 0) % c
    hj_ = lax.broadcasted_iota(jnp.int32, (2 * c, 2 * c), 1) % c
    order = sign * (hi_ - hj_)
    strict = order > 0
    incl = order >= 0
    eye = (lax.broadcasted_iota(jnp.int32, (2 * c, 2 * c), 0)
           == lax.broadcasted_iota(jnp.int32, (2 * c, 2 * c), 1))
    head0 = lax.broadcasted_iota(jnp.int32, (c, LANES), 1) < HEAD_DIM

    lw_all = dr_ref[0, 0]
    gi_all = jnp.dot(cum, lw_all, precision=HI, preferred_element_type=F32)

    for p in range(width // LANES):
        cols = slice(p * LANES, (p + 1) * LANES)
        r = sh_ref[0, :, cols]
        v = sh_ref[1, :, cols]
        kk = sh_ref[2, :, cols]
        lw = lw_all[:, cols]
        b = dr_ref[0, 1, :, cols]
        kd = dr_ref[0, 2, :, cols]
        gi = gi_all[:, cols]
        gtot = jnp.sum(lw, axis=0, keepdims=True)
        en = jnp.exp(-gi)
        ec = jnp.exp(gtot - gi)
        at = -kk * jnp.exp(gi - lw)
        rt = r * jnp.exp(gi)

        lhs = jnp.concatenate([_hat(at, head0), _hat(rt, head0)], axis=0)
        rhs = jnp.concatenate([_hat(b * en, head0), _hat(kd * en, head0)], axis=0)
        aa = dot_nt(lhs, rhs)
        a_ab = jnp.where(strict, aa[:2 * c, :2 * c], 0.0)
        a_ak = jnp.where(strict, aa[:2 * c, 2 * c:], 0.0)
        a_rb = jnp.where(incl, aa[2 * c:, :2 * c], 0.0)
        a_rk = jnp.where(incl, aa[2 * c:, 2 * c:], 0.0)

        inv = jnp.where(eye, 1.0, 0.0) + a_ab
        pw = a_ab
        n = 2
        while n < c * 2:
            pw = dot(pw, pw)
            inv = inv + dot(inv, pw)
            n *= 2

        s0 = s_ref[p]
        ls = dot_nt(lhs, s0)
        vh = _hat(v, head0)
        u = dot(inv, ls[:2 * c] + dot(a_ak, vh))
        uv = jnp.concatenate([u, vh], axis=0)
        oh = ls[2 * c:] + dot(jnp.concatenate([a_rb, a_rk], axis=1), uv)
        o_ref[0, :, cols] = oh[:c] + oh[c:]
        bk = jnp.concatenate([_hat(b * ec, head0), _hat(kd * ec, head0)], axis=0)
        s_ref[p] = s0 * jnp.exp(gtot) + dot_tn(uv, bk)


def _rwkv_scan(sh, dr, batch, seq, prec=HI):
    _, t, width = sh.shape
    c = SCAN_CHUNK
    nc = seq // c

    def blk(d, b, i):
        return b * nc + i + d * (nc - 1 - 2 * i)

    return pl.pallas_call(
        functools.partial(_scan_kernel, prec=prec),
        grid=(2, batch, nc),
        in_specs=[
            pl.BlockSpec((3, c, width), lambda d, b, i: (0, blk(d, b, i), 0)),
            pl.BlockSpec((1, 3, c, width), lambda d, b, i: (d, 0, blk(d, b, i), 0)),
        ],
        out_specs=pl.BlockSpec((1, c, width), lambda d, b, i: (d, blk(d, b, i), 0)),
        out_shape=jax.ShapeDtypeStruct((2, t, width), F32),
        scratch_shapes=[pltpu.VMEM((width // LANES, LANES, LANES), F32)],
        compiler_params=_cparams("parallel", "parallel", "arbitrary"),
        name="rwkv_scan",
    )(sh, dr)


def _post_kernel(o_ref, bonus_ref, gg_ref, gng_ref, gnb_ref, y_ref):
    width = y_ref.shape[1]
    avg = _head_block_ones(1.0 / HEAD_DIM)
    for j in range(width // LANES):
        cols = slice(j * LANES, (j + 1) * LANES)
        o = o_ref[0, :, cols] + o_ref[1, :, cols]
        oc = o - _head_sum(o, avg)
        var = _head_sum(oc * oc, avg)
        y = oc * lax.rsqrt(var + GN_EPS) * gng_ref[:, cols] + gnb_ref[:, cols]
        y_ref[:, cols] = ((y + bonus_ref[:, cols]) * gg_ref[:, cols]).astype(y_ref.dtype)


def _rwkv_post(o, bonus, gg, gn_g, gn_b, tm=512):
    _, t, width = o.shape
    return pl.pallas_call(
        _post_kernel,
        grid=(t // tm,),
        in_specs=[
            pl.BlockSpec((2, tm, width), lambda i: (0, i, 0)),
            pl.BlockSpec((tm, width), lambda i: (i, 0)),
            pl.BlockSpec((tm, width), lambda i: (i, 0)),
            pl.BlockSpec((1, width), lambda i: (0, 0)),
            pl.BlockSpec((1, width), lambda i: (0, 0)),
        ],
        out_specs=pl.BlockSpec((tm, width), lambda i: (i, 0)),
        out_shape=jax.ShapeDtypeStruct((t, width), BF16),
        compiler_params=_cparams("parallel"),
        name="rwkv_post",
    )(o, bonus, gg, gn_g, gn_b)


def _merge_kernel(ya_ref, yb_ref, wa_ref, wb_ref, ga_ref, gb_ref, o_ref):
    pa = jnp.dot(ya_ref[...], wa_ref[...], preferred_element_type=F32)
    pb = jnp.dot(yb_ref[...], wb_ref[...], preferred_element_type=F32)
    o_ref[...] = (ga_ref[...].astype(F32) * pa + gb_ref[...].astype(F32) * pb).astype(o_ref.dtype)


def _merge(ya, yb, wa, wb, gates, tm=1024, tn=512):
    t, k = ya.shape
    n = wa.shape[1]
    nb = n // tn
    return pl.pallas_call(
        _merge_kernel,
        grid=(t // tm, nb),
        in_specs=[
            pl.BlockSpec((tm, k), lambda i, j: (i, 0)),
            pl.BlockSpec((tm, k), lambda i, j: (i, 0)),
            pl.BlockSpec((k, tn), lambda i, j: (0, j)),
            pl.BlockSpec((k, tn), lambda i, j: (0, j)),
            pl.BlockSpec((tm, tn), lambda i, j: (i, j)),
            pl.BlockSpec((tm, tn), lambda i, j: (i, j + nb)),
        ],
        out_specs=pl.BlockSpec((tm, tn), lambda i, j: (i, j)),
        out_shape=jax.ShapeDtypeStruct((t, n), BF16),
        compiler_params=_cparams("parallel", "arbitrary"),
        name="merge_proj",
    )(ya, yb, wa, wb, gates, gates)


def _outproj_kernel(m_ref, w_ref, x_ref, g_ref, x1_ref, h2_ref):
    x1 = x_ref[...] + jnp.dot(m_ref[...], w_ref[...], preferred_element_type=F32)
    x1_ref[...] = x1
    ms = jnp.mean(x1 * x1, axis=-1, keepdims=True)
    h2_ref[...] = (x1 * lax.rsqrt(ms + NORM_EPS) * g_ref[...]).astype(h2_ref.dtype)


def _outproj(m, w, x, g, tm=256):
    t, d = x.shape
    return pl.pallas_call(
        _outproj_kernel,
        grid=(t // tm,),
        in_specs=[
            pl.BlockSpec((tm, d), lambda i: (i, 0)),
            pl.BlockSpec((d, d), lambda i: (0, 0)),
            pl.BlockSpec((tm, d), lambda i: (i, 0)),
            pl.BlockSpec((1, d), lambda i: (0, 0)),
        ],
        out_specs=[
            pl.BlockSpec((tm, d), lambda i: (i, 0)),
            pl.BlockSpec((tm, d), lambda i: (i, 0)),
        ],
        out_shape=[jax.ShapeDtypeStruct((t, d), F32), jax.ShapeDtypeStruct((t, d), BF16)],
        compiler_params=_cparams("parallel"),
        name="out_proj",
    )(m, w, x, g)


def _ffn1_kernel(h_ref, wg_ref, wu_ref, g_ref, u_ref):
    h = h_ref[...]
    g_ref[...] = jnp.dot(h, wg_ref[...], preferred_element_type=F32).astype(g_ref.dtype)
    u_ref[...] = jnp.dot(h, wu_ref[...], preferred_element_type=F32).astype(u_ref.dtype)


def _ffn1(h, wg, wu, tm=1024, tn=512):
    t, d = h.shape
    f = wg.shape[1]
    return pl.pallas_call(
        _ffn1_kernel,
        grid=(t // tm, f // tn),
        in_specs=[
            pl.BlockSpec((tm, d), lambda i, j: (i, 0)),
            pl.BlockSpec((d, tn), lambda i, j: (0, j)),
            pl.BlockSpec((d, tn), lambda i, j: (0, j)),
        ],
        out_specs=[
            pl.BlockSpec((tm, tn), lambda i, j: (i, j)),
            pl.BlockSpec((tm, tn), lambda i, j: (i, j)),
        ],
        out_shape=[jax.ShapeDtypeStruct((t, f), BF16), jax.ShapeDtypeStruct((t, f), BF16)],
        compiler_params=_cparams("parallel", "arbitrary"),
        name="ffn_gate_up",
    )(h, wg, wu)


def _ffn2_kernel(g_ref, gb_ref, ga_ref, u_ref, cw_ref, cb_ref, wd_ref, x1_ref, nf_ref,
                 o_ref, acc_ref, *, seq):
    tm = g_ref.shape[0]
    i = pl.program_id(0)
    kidx = pl.program_id(1)
    tiles_per_seq = seq // tm
    first = (i % tiles_per_seq) == 0
    last = (i % tiles_per_seq) == tiles_per_seq - 1

    @pl.when(kidx == 0)
    def _():
        acc_ref[...] = jnp.zeros_like(acc_ref)

    g = g_ref[...].astype(F32)
    row = lax.broadcasted_iota(jnp.int32, (tm, 1), 0)
    prev = pltpu.roll(g, 1, 0)
    prev = jnp.where(row == 0, gb_ref[7:8, :].astype(F32), prev)
    prev = jnp.where(jnp.logical_and(row == 0, first), 0.0, prev)
    nxt = pltpu.roll(g, tm - 1, 0)
    nxt = jnp.where(row == tm - 1, ga_ref[0:1, :].astype(F32), nxt)
    nxt = jnp.where(jnp.logical_and(row == tm - 1, last), 0.0, nxt)
    gt = prev * cw_ref[0:1, :] + g * cw_ref[1:2, :] + nxt * cw_ref[2:3, :] + cb_ref[...]
    act = (jax.nn.silu(gt) * u_ref[...].astype(F32)).astype(BF16)
    acc_ref[...] += jnp.dot(act, wd_ref[...], preferred_element_type=F32)

    @pl.when(kidx == pl.num_programs(1) - 1)
    def _():
        xf = x1_ref[...] + acc_ref[...]
        ms = jnp.mean(xf * xf, axis=-1, keepdims=True)
        o_ref[...] = xf * lax.rsqrt(ms + NORM_EPS) * nf_ref[...]


def _ffn2(g, u, cw, cb, wd, x1, nf, seq, tm=512, tk=512):
    t, f = g.shape
    d = x1.shape[1]
    nb8 = t // 8
    return pl.pallas_call(
        functools.partial(_ffn2_kernel, seq=seq),
        grid=(t // tm, f // tk),
        in_specs=[
            pl.BlockSpec((tm, tk), lambda i, k: (i, k)),
            pl.BlockSpec((8, tk), lambda i, k: (jnp.maximum(i * (tm // 8) - 1, 0), k)),
            pl.BlockSpec((8, tk), lambda i, k: (jnp.minimum((i + 1) * (tm // 8), nb8 - 1), k)),
            pl.BlockSpec((tm, tk), lambda i, k: (i, k)),
            pl.BlockSpec((cw.shape[0], tk), lambda i, k: (0, k)),
            pl.BlockSpec((1, tk), lambda i, k: (0, k)),
            pl.BlockSpec((tk, d), lambda i, k: (k, 0)),
            pl.BlockSpec((tm, d), lambda i, k: (i, 0)),
            pl.BlockSpec((1, d), lambda i, k: (0, 0)),
        ],
        out_specs=pl.BlockSpec((tm, d), lambda i, k: (i, 0)),
        out_shape=jax.ShapeDtypeStruct((t, d), F32),
        scratch_shapes=[pltpu.VMEM((tm, d), F32)],
        compiler_params=_cparams("parallel", "arbitrary"),
        name="ffn_down",
    )(g, g, g, u, cw, cb, wd, x1, nf)


def _pad_cols(a, n):
    return jnp.pad(a, ((0, 0), (0, n - a.shape[1])))


def _pad_rows(a, n):
    return jnp.pad(a, ((0, n - a.shape[0]), (0, 0)))


def _layer(x, batch, seq, norm1_g, w_in, sgu_ln_g, sgu_ln_b, sgu_w, sgu_b,
           mu_prev, mu_next, w0_f, w2_f, a0_f, a2_f, w0_b, w2_b, a0_b, a2_b, k_k, k_a, r_k,
           g2, gn_g, gn_b, w_proj_a, w_proj_b, w_out, norm2_g, ffn_w_gate, ffn_w_up,
           ffn_conv_w, ffn_conv_b, ffn_w_down, norm_out_g):
    d = x.shape[1]
    sgu_width = sgu_ln_g.shape[0]
    width = k_k.shape[0]
    dl = w2_f.shape[0]
    al = a2_f.shape[0]
    feat_w = mu_prev.shape[0]
    feat_pad = -(-feat_w // (4 * LANES)) * (4 * LANES)
    assert dl == HEAD_DIM and al == HEAD_DIM and 2 * dl == LANES
    row = lambda a: a.reshape(1, -1)

    o1 = 2 * sgu_width
    o2 = o1 + feat_w
    w_uv = w_in[:, :o1].astype(BF16)
    w_feat = _pad_cols(w_in[:, o1:o2], feat_pad).astype(BF16)
    w_gates = w_in[:, o2:].astype(BF16)
    zw = jnp.zeros_like(w2_f)
    w2c = jnp.concatenate([jnp.concatenate([w2_f, zw], 1), jnp.concatenate([zw, w2_b], 1)], 0)
    za = jnp.zeros_like(a2_f)
    a2c = jnp.concatenate([jnp.concatenate([a2_f, za], 1), jnp.concatenate([za, a2_b], 1)], 0)
    g2p = _pad_rows(g2, feat_pad - 3 * width - 2 * LANES)
    w0c = row(jnp.concatenate([w0_f, w0_b]))
    a0c = row(jnp.concatenate([a0_f, a0_b]))
    mup = _pad_cols(row(mu_prev), feat_pad)
    mun = _pad_cols(row(mu_next), feat_pad)
    bs_full = jnp.repeat(sgu_b.T, sgu_width // sgu_b.shape[0], axis=1)

    g1 = row(norm1_g)
    uv = _norm_proj(x, g1, w_uv, F32)
    feat = _norm_proj(x, g1, w_feat, F32)
    gates = _norm_proj(x, g1, w_gates, BF16, act="sigmoid")

    y_a = _sgu(uv, row(sgu_ln_g), row(sgu_ln_b), sgu_w.astype(BF16), bs_full)

    sh, dr, bonus, gg = _rwkv_prep(feat, seq, width, mup, mun, w0c, w2c, a0c, a2c, g2p,
                                   row(k_k), row(k_a), row(r_k))
    o = _rwkv_scan(sh, dr, batch, seq)
    y_b = _rwkv_post(o, bonus, gg, row(gn_g), row(gn_b))

    merged = _merge(y_a, y_b, w_proj_a.astype(BF16), w_proj_b.astype(BF16), gates)
    x1, h2 = _outproj(merged, w_out.astype(BF16), x, row(norm2_g))

    gm, um = _ffn1(h2, ffn_w_gate.astype(BF16), ffn_w_up.astype(BF16))
    return _ffn2(gm, um, ffn_conv_w, row(ffn_conv_b), ffn_w_down.astype(BF16), x1,
                 norm_out_g, seq)


def kernel(x, norm1_g, w_in, sgu_ln_g, sgu_ln_b, sgu_w, sgu_b, rwkv_mu_prev, rwkv_mu_next, rwkv_w0_f, rwkv_w2_f, rwkv_a0_f, rwkv_a2_f, rwkv_w0_b, rwkv_w2_b, rwkv_a0_b, rwkv_a2_b, rwkv_k_k, rwkv_k_a, rwkv_r_k, rwkv_g2, rwkv_gn_g, rwkv_gn_b, w_proj_a, w_proj_b, w_out, norm2_g, ffn_w_gate, ffn_w_up, ffn_conv_w, ffn_conv_b, ffn_w_down, norm_f_g):
    batch, seq, d = x.shape
    depth = norm1_g.shape[0]
    assert depth == 1, "the fused final RMSNorm assumes a single layer"
    xf = x.reshape(batch * seq, d)
    per_layer = (norm1_g, w_in, sgu_ln_g, sgu_ln_b, sgu_w, sgu_b, rwkv_mu_prev, rwkv_mu_next,
                 rwkv_w0_f, rwkv_w2_f, rwkv_a0_f, rwkv_a2_f, rwkv_w0_b, rwkv_w2_b, rwkv_a0_b,
                 rwkv_a2_b, rwkv_k_k, rwkv_k_a, rwkv_r_k, rwkv_g2, rwkv_gn_g, rwkv_gn_b,
                 w_proj_a, w_proj_b, w_out, norm2_g, ffn_w_gate, ffn_w_up, ffn_conv_w,
                 ffn_conv_b, ffn_w_down)
    out = _layer(xf, batch, seq, *(p[0] for p in per_layer), norm_f_g.reshape(1, d))
    return out.reshape(batch, seq, d)
```

```python
import functools

import jax
import jax.numpy as jnp
from jax import lax
from jax.experimental import pallas as pl
from jax.experimental.pallas import tpu as pltpu

F32 = jnp.float32
BF16 = jnp.bfloat16

NORM_EPS = 1e-6
LN_EPS = 1e-5
GN_EPS = 64e-5

HEAD_DIM = 64
LANES = 128
SGU_CHUNK = 128
SCAN_CHUNK = 64
VMEM_LIMIT = 48 * 1024 * 1024

HI = lax.Precision.HIGHEST


def _cparams(*sem):
    return pltpu.CompilerParams(dimension_semantics=sem, vmem_limit_bytes=VMEM_LIMIT)


def _proj_kernel(x_ref, g_ref, w_ref, o_ref, h_ref, *, act):
    @pl.when(pl.program_id(1) == 0)
    def _():
        x = x_ref[...]
        ms = jnp.mean(x * x, axis=-1, keepdims=True)
        h_ref[...] = (x * lax.rsqrt(ms + NORM_EPS) * g_ref[...]).astype(BF16)

    acc = jnp.dot(h_ref[...], w_ref[...], preferred_element_type=F32)
    if act == "sigmoid":
        acc = jax.nn.sigmoid(acc)
    o_ref[...] = acc.astype(o_ref.dtype)


def _norm_proj(x, g, w, out_dtype, act=None, tm=1024, tn=512):
    t, d = x.shape
    n = w.shape[1]
    return pl.pallas_call(
        functools.partial(_proj_kernel, act=act),
        grid=(t // tm, n // tn),
        in_specs=[
            pl.BlockSpec((tm, d), lambda i, j: (i, 0)),
            pl.BlockSpec((1, d), lambda i, j: (0, 0)),
            pl.BlockSpec((d, tn), lambda i, j: (0, j)),
        ],
        out_specs=pl.BlockSpec((tm, tn), lambda i, j: (i, j)),
        out_shape=jax.ShapeDtypeStruct((t, n), out_dtype),
        scratch_shapes=[pltpu.VMEM((tm, d), BF16)],
        compiler_params=_cparams("parallel", "arbitrary"),
        name="norm_proj",
    )(x, g, w)


def _gelu(x):
    return 0.5 * x * (1.0 + lax.erf(x * (2.0 ** -0.5)))


def _sgu_kernel(uv_ref, lng_ref, lnb_ref, ws_ref, bs_ref, o_ref, *, width, groups):
    tm = uv_ref.shape[0]
    gv = _gelu(uv_ref[:, width:])
    mu = jnp.mean(gv, axis=-1, keepdims=True)
    vc = gv - mu
    var = jnp.mean(vc * vc, axis=-1, keepdims=True)
    vn = (vc * lax.rsqrt(var + LN_EPS) * lng_ref[...] + lnb_ref[...]).astype(BF16)
    gd = width // groups
    for c in range(tm // SGU_CHUNK):
        rows = slice(c * SGU_CHUNK, (c + 1) * SGU_CHUNK)
        for g in range(groups):
            cols = slice(g * gd, (g + 1) * gd)
            mixed = jnp.dot(ws_ref[g], vn[rows, cols], preferred_element_type=F32)
            gu = _gelu(uv_ref[rows, cols])
            o_ref[rows, cols] = (gu * (mixed + bs_ref[:, cols])).astype(o_ref.dtype)


def _sgu(uv, ln_g, ln_b, ws, bs_full, tm=256):
    t, w2 = uv.shape
    width = w2 // 2
    groups = ws.shape[0]
    return pl.pallas_call(
        functools.partial(_sgu_kernel, width=width, groups=groups),
        grid=(t // tm,),
        in_specs=[
            pl.BlockSpec((tm, w2), lambda i: (i, 0)),
            pl.BlockSpec((1, width), lambda i: (0, 0)),
            pl.BlockSpec((1, width), lambda i: (0, 0)),
            pl.BlockSpec(ws.shape, lambda i: (0, 0, 0)),
            pl.BlockSpec(bs_full.shape, lambda i: (0, 0)),
        ],
        out_specs=pl.BlockSpec((tm, width), lambda i: (i, 0)),
        out_shape=jax.ShapeDtypeStruct((t, width), BF16),
        compiler_params=_cparams("parallel"),
        name="sgu_mixer",
    )(uv, ln_g, ln_b, ws, bs_full)


def _head_block_ones(scale):
    r = lax.broadcasted_iota(jnp.int32, (LANES, LANES), 0) // HEAD_DIM
    c = lax.broadcasted_iota(jnp.int32, (LANES, LANES), 1) // HEAD_DIM
    return jnp.where(r == c, scale, 0.0).astype(F32)


def _head_sum(x, bd):
    return jnp.dot(x, bd, precision=HI, preferred_element_type=F32)


def _shifted_rows(x, before_ref, after_ref, first_row_of_seq, last_row_of_seq):
    tm = x.shape[0]
    row = lax.broadcasted_iota(jnp.int32, (tm, 1), 0)
    prev = pltpu.roll(x, 1, 0)
    prev = jnp.where(row == 0, before_ref[7:8, :], prev)
    prev = jnp.where(jnp.logical_and(row == 0, first_row_of_seq), 0.0, prev)
    nxt = pltpu.roll(x, tm - 1, 0)
    nxt = jnp.where(row == tm - 1, after_ref[0:1, :], nxt)
    nxt = jnp.where(jnp.logical_and(row == tm - 1, last_row_of_seq), 0.0, nxt)
    return prev, nxt


def _prep_kernel(f_ref, fb_ref, fa_ref, mup_ref, mun_ref, w0_ref, w2_ref, a0_ref, a2_ref,
                 g2_ref, kk_ref, ka_ref, rk_ref,
                 sh_ref, dr_ref, bonus_ref, gg_ref, *, seq, width):
    tm = f_ref.shape[0]
    i = pl.program_id(0)
    tiles_per_seq = seq // tm
    first = (i % tiles_per_seq) == 0
    last = (i % tiles_per_seq) == tiles_per_seq - 1
    x = f_ref[...]
    prev, nxt = _shifted_rows(x, fb_ref, fa_ref, first, last)
    xs = x + mup_ref[...] * (prev - x) + mun_ref[...] * (nxt - x)

    r = xs[:, 0:width]
    k = xs[:, width:2 * width]
    v = xs[:, 2 * width:3 * width]
    o = 3 * width
    lo_w = xs[:, o:o + LANES]
    lo_a = xs[:, o + LANES:o + 2 * LANES]
    lo_g = xs[:, o + 2 * LANES:]

    wpre = w0_ref[...] + jnp.dot(jnp.tanh(lo_w), w2_ref[...], precision=HI,
                                 preferred_element_type=F32)
    z = -wpre
    softplus = jnp.maximum(z, 0.0) + jnp.log(1.0 + jnp.exp(-jnp.abs(z)))
    lw = -jnp.exp(-softplus - 0.5)
    a = jax.nn.sigmoid(a0_ref[...] + jnp.dot(lo_a, a2_ref[...], precision=HI,
                                             preferred_element_type=F32))
    gg_ref[...] = jnp.dot(jax.nn.sigmoid(lo_g), g2_ref[...], precision=HI,
                          preferred_element_type=F32)

    bd = _head_block_ones(1.0)
    kk = k * kk_ref[...]
    sh_ref[0] = r
    sh_ref[1] = v
    for j in range(width // LANES):
        cols = slice(j * LANES, (j + 1) * LANES)
        kkj = kk[:, cols]
        ss = _head_sum(kkj * kkj, bd)
        kkn = kkj * lax.rsqrt(jnp.maximum(ss, 1e-24))
        sh_ref[2, :, cols] = kkn
        ksum = jnp.zeros_like(kkj)
        for d in range(2):
            dcols = slice(d * width + j * LANES, d * width + (j + 1) * LANES)
            a_d = a[:, dcols]
            k_d = k[:, cols] * (1.0 + (a_d - 1.0) * ka_ref[:, cols])
            dr_ref[d, 0, :, cols] = lw[:, dcols]
            dr_ref[d, 1, :, cols] = kkn * a_d
            dr_ref[d, 2, :, cols] = k_d
            ksum = ksum + k_d
        bsum = _head_sum(r[:, cols] * ksum * rk_ref[:, cols], bd)
        bonus_ref[:, cols] = bsum * v[:, cols]


def _rwkv_prep(feat, seq, width, mup, mun, w0c, w2c, a0c, a2c, g2p, k_k, k_a, r_k, tm=128):
    t, fw = feat.shape
    nb8 = t // 8
    row = lambda i: (0, 0)
    return pl.pallas_call(
        functools.partial(_prep_kernel, seq=seq, width=width),
        grid=(t // tm,),
        in_specs=[
            pl.BlockSpec((tm, fw), lambda i: (i, 0)),
            pl.BlockSpec((8, fw), lambda i: (jnp.maximum(i * (tm // 8) - 1, 0), 0)),
            pl.BlockSpec((8, fw), lambda i: (jnp.minimum((i + 1) * (tm // 8), nb8 - 1), 0)),
            pl.BlockSpec((1, fw), row),
            pl.BlockSpec((1, fw), row),
            pl.BlockSpec(w0c.shape, row),
            pl.BlockSpec(w2c.shape, row),
            pl.BlockSpec(a0c.shape, row),
            pl.BlockSpec(a2c.shape, row),
            pl.BlockSpec(g2p.shape, row),
            pl.BlockSpec((1, width), row),
            pl.BlockSpec((1, width), row),
            pl.BlockSpec((1, width), row),
        ],
        out_specs=[
            pl.BlockSpec((3, tm, width), lambda i: (0, i, 0)),
            pl.BlockSpec((2, 3, tm, width), lambda i: (0, 0, i, 0)),
            pl.BlockSpec((tm, width), lambda i: (i, 0)),
            pl.BlockSpec((tm, width), lambda i: (i, 0)),
        ],
        out_shape=[
            jax.ShapeDtypeStruct((3, t, width), F32),
            jax.ShapeDtypeStruct((2, 3, t, width), F32),
            jax.ShapeDtypeStruct((t, width), F32),
            jax.ShapeDtypeStruct((t, width), F32),
        ],
        compiler_params=_cparams("parallel"),
        name="rwkv_prep",
    )(feat, feat, feat, mup, mun, w0c, w2c, a0c, a2c, g2p, k_k, k_a, r_k)


def _hat(x, first_half):
    return jnp.concatenate([jnp.where(first_half, x, 0.0), jnp.where(first_half, 0.0, x)],
                           axis=0)


def _scan_kernel(sh_ref, dr_ref, o_ref, s_ref, *, group):
    c = SCAN_CHUNK
    batch, width = sh_ref.shape[1], sh_ref.shape[3]
    assert 2 * c == LANES and c == HEAD_DIM
    sign = 1 - 2 * pl.program_id(0)

    @pl.when(pl.program_id(1) == 0)
    def _():
        s_ref[...] = jnp.zeros_like(s_ref)

    bf = lambda t: t.astype(BF16)
    dg = lambda p, q, dims: lax.dot_general(bf(p), bf(q), (dims, ((), ())),
                                            preferred_element_type=F32)
    dot = lambda p, q: dg(p, q, ((1,), (0,)))
    dot_nt = lambda p, q: dg(p, q, ((1,), (1,)))
    dot_tn = lambda p, q: dg(p, q, ((0,), (0,)))

    ti = lax.broadcasted_iota(jnp.int32, (c, c), 0)
    tj = lax.broadcasted_iota(jnp.int32, (c, c), 1)
    cum = jnp.where(sign * (ti - tj) >= 0, 1.0, 0.0).astype(F32)
    wt = lax.broadcasted_iota(jnp.int32, (c, LANES), 0)
    wl = lax.broadcasted_iota(jnp.int32, (c, LANES), 1)
    order = sign * (wt - wl % c)
    strict = order > 0
    incl = order >= 0
    eye = jnp.where(wt == wl % c, 1.0, 0.0).astype(F32)
    half = wl < c
    same_head = (lax.broadcasted_iota(jnp.int32, (LANES, LANES), 0) // HEAD_DIM
                 == lax.broadcasted_iota(jnp.int32, (LANES, LANES), 1) // HEAD_DIM)
    hat = lambda t: _hat(t, half)

    units = [(b, p) for b in range(batch) for p in range(width // LANES)]
    for g0 in range(0, len(units), group):
        grp = units[g0:g0 + group]
        st = {}
        for (b, p) in grp:
            cols = slice(p * LANES, (p + 1) * LANES)
            r = sh_ref[0, b, :, cols]
            v = sh_ref[1, b, :, cols]
            kk = sh_ref[2, b, :, cols]
            lw = dr_ref[0, 0, b, :, cols]
            bb = dr_ref[0, 1, b, :, cols]
            kd = dr_ref[0, 2, b, :, cols]
            gi = jnp.dot(cum, lw, precision=HI, preferred_element_type=F32)
            gtot = jnp.sum(lw, axis=0, keepdims=True)
            en = jnp.exp(-gi)
            ec = jnp.exp(gtot - gi)
            lhs = bf(jnp.concatenate([-kk * jnp.exp(gi - lw), r * jnp.exp(gi)], axis=0))
            rhs = jnp.concatenate([hat(bb * en), hat(kd * en)], axis=0)
            aa = dot_nt(lhs, rhs)
            a_ab = jnp.where(strict, aa[:c, :LANES], 0.0)
            a_ak = jnp.where(strict, aa[:c, LANES:], 0.0)
            ark = bf(jnp.concatenate([jnp.where(incl, aa[c:, :LANES], 0.0),
                                      jnp.where(incl, aa[c:, LANES:], 0.0)], axis=1))
            vh = bf(hat(v))
            s0 = s_ref[b * (width // LANES) + p]
            ls = dot_nt(lhs, s0)
            st[(b, p)] = dict(
                cols=cols, v=v, s0=s0, ark=ark, vh=vh,
                bk=bf(jnp.concatenate([bb * ec, kd * ec], axis=0)),
                eg=jnp.exp(gtot), pw=a_ab, inv=eye + a_ab,
                rhs_u=ls[:c] + dot(a_ak, vh), rs=ls[c:])

        n = 2
        while n < 2 * c:
            for u in grp:
                st[u]["pw"] = dot(st[u]["pw"], hat(st[u]["pw"]))
            for u in grp:
                st[u]["inv"] = st[u]["inv"] + dot(st[u]["inv"], hat(st[u]["pw"]))
            n *= 2

        for u in grp:
            st[u]["u"] = dot(st[u]["inv"], hat(st[u]["rhs_u"]))
        for (b, p) in grp:
            d = st[(b, p)]
            uvh = jnp.concatenate([bf(hat(d["u"])), d["vh"]], axis=0)
            o_ref[0, b, :, d["cols"]] = d["rs"] + dot(d["ark"], uvh)
        for (b, p) in grp:
            d = st[(b, p)]
            uv = jnp.concatenate([d["u"], d["v"]], axis=0)
            s_ref[b * (width // LANES) + p] = (d["s0"] * d["eg"]
                                               + jnp.where(same_head, dot_tn(uv, d["bk"]), 0.0))


def _rwkv_scan(sh, dr, batch, seq, group=8):
    _, t, width = sh.shape
    c = SCAN_CHUNK
    nc = seq // c
    sh = sh.reshape(3, batch, seq, width)
    dr = dr.reshape(2, 3, batch, seq, width)
    blk = lambda d, i: i + d * (nc - 1 - 2 * i)
    o = pl.pallas_call(
        functools.partial(_scan_kernel, group=group),
        grid=(2, nc),
        in_specs=[
            pl.BlockSpec((3, batch, c, width), lambda d, i: (0, 0, blk(d, i), 0)),
            pl.BlockSpec((1, 3, batch, c, width), lambda d, i: (d, 0, 0, blk(d, i), 0)),
        ],
        out_specs=pl.BlockSpec((1, batch, c, width), lambda d, i: (d, 0, blk(d, i), 0)),
        out_shape=jax.ShapeDtypeStruct((2, batch, seq, width), F32),
        scratch_shapes=[pltpu.VMEM((batch * (width // LANES), LANES, LANES), F32)],
        compiler_params=_cparams("parallel", "arbitrary"),
        name="rwkv_scan",
    )(sh, dr)
    return o.reshape(2, t, width)


def _post_kernel(o_ref, bonus_ref, gg_ref, gng_ref, gnb_ref, y_ref):
    width = y_ref.shape[1]
    avg = _head_block_ones(1.0 / HEAD_DIM)
    for j in range(width // LANES):
        cols = slice(j * LANES, (j + 1) * LANES)
        o = o_ref[0, :, cols] + o_ref[1, :, cols]
        oc = o - _head_sum(o, avg)
        var = _head_sum(oc * oc, avg)
        y = oc * lax.rsqrt(var + GN_EPS) * gng_ref[:, cols] + gnb_ref[:, cols]
        y_ref[:, cols] = ((y + bonus_ref[:, cols]) * gg_ref[:, cols]).astype(y_ref.dtype)


def _rwkv_post(o, bonus, gg, gn_g, gn_b, tm=512, out_dtype=BF16):
    _, t, width = o.shape
    return pl.pallas_call(
        _post_kernel,
        grid=(t // tm,),
        in_specs=[
            pl.BlockSpec((2, tm, width), lambda i: (0, i, 0)),
            pl.BlockSpec((tm, width), lambda i: (i, 0)),
            pl.BlockSpec((tm, width), lambda i: (i, 0)),
            pl.BlockSpec((1, width), lambda i: (0, 0)),
            pl.BlockSpec((1, width), lambda i: (0, 0)),
        ],
        out_specs=pl.BlockSpec((tm, width), lambda i: (i, 0)),
        out_shape=jax.ShapeDtypeStruct((t, width), out_dtype),
        compiler_params=_cparams("parallel"),
        name="rwkv_post",
    )(o, bonus, gg, gn_g, gn_b)


def _merge_kernel(ya_ref, yb_ref, wa_ref, wb_ref, ga_ref, gb_ref, o_ref):
    pa = jnp.dot(ya_ref[...], wa_ref[...], preferred_element_type=F32)
    pb = jnp.dot(yb_ref[...], wb_ref[...], preferred_element_type=F32)
    o_ref[...] = (ga_ref[...].astype(F32) * pa + gb_ref[...].astype(F32) * pb).astype(o_ref.dtype)


def _merge(ya, yb, wa, wb, gates, tm=1024, tn=512):
    t, k = ya.shape
    n = wa.shape[1]
    nb = n // tn
    return pl.pallas_call(
        _merge_kernel,
        grid=(t // tm, nb),
        in_specs=[
            pl.BlockSpec((tm, k), lambda i, j: (i, 0)),
            pl.BlockSpec((tm, k), lambda i, j: (i, 0)),
            pl.BlockSpec((k, tn), lambda i, j: (0, j)),
            pl.BlockSpec((k, tn), lambda i, j: (0, j)),
            pl.BlockSpec((tm, tn), lambda i, j: (i, j)),
            pl.BlockSpec((tm, tn), lambda i, j: (i, j + nb)),
        ],
        out_specs=pl.BlockSpec((tm, tn), lambda i, j: (i, j)),
        out_shape=jax.ShapeDtypeStruct((t, n), BF16),
        compiler_params=_cparams("parallel", "arbitrary"),
        name="merge_proj",
    )(ya, yb, wa, wb, gates, gates)


def _outproj_kernel(m_ref, w_ref, x_ref, g_ref, x1_ref, h2_ref):
    x1 = x_ref[...] + jnp.dot(m_ref[...], w_ref[...], preferred_element_type=F32)
    x1_ref[...] = x1
    ms = jnp.mean(x1 * x1, axis=-1, keepdims=True)
    h2_ref[...] = (x1 * lax.rsqrt(ms + NORM_EPS) * g_ref[...]).astype(h2_ref.dtype)


def _outproj(m, w, x, g, tm=256):
    t, d = x.shape
    return pl.pallas_call(
        _outproj_kernel,
        grid=(t // tm,),
        in_specs=[
            pl.BlockSpec((tm, d), lambda i: (i, 0)),
            pl.BlockSpec((d, d), lambda i: (0, 0)),
            pl.BlockSpec((tm, d), lambda i: (i, 0)),
            pl.BlockSpec((1, d), lambda i: (0, 0)),
        ],
        out_specs=[
            pl.BlockSpec((tm, d), lambda i: (i, 0)),
            pl.BlockSpec((tm, d), lambda i: (i, 0)),
        ],
        out_shape=[jax.ShapeDtypeStruct((t, d), F32), jax.ShapeDtypeStruct((t, d), BF16)],
        compiler_params=_cparams("parallel"),
        name="out_proj",
    )(m, w, x, g)


def _ffn1_kernel(h_ref, wg_ref, wu_ref, g_ref, u_ref):
    h = h_ref[...]
    g_ref[...] = jnp.dot(h, wg_ref[...], preferred_element_type=F32).astype(g_ref.dtype)
    u_ref[...] = jnp.dot(h, wu_ref[...], preferred_element_type=F32).astype(u_ref.dtype)


def _ffn1(h, wg, wu, tm=1024, tn=512):
    t, d = h.shape
    f = wg.shape[1]
    return pl.pallas_call(
        _ffn1_kernel,
        grid=(t // tm, f // tn),
        in_specs=[
            pl.BlockSpec((tm, d), lambda i, j: (i, 0)),
            pl.BlockSpec((d, tn), lambda i, j: (0, j)),
            pl.BlockSpec((d, tn), lambda i, j: (0, j)),
        ],
        out_specs=[
            pl.BlockSpec((tm, tn), lambda i, j: (i, j)),
            pl.BlockSpec((tm, tn), lambda i, j: (i, j)),
        ],
        out_shape=[jax.ShapeDtypeStruct((t, f), BF16), jax.ShapeDtypeStruct((t, f), BF16)],
        compiler_params=_cparams("parallel", "arbitrary"),
        name="ffn_gate_up",
    )(h, wg, wu)


def _ffn2_kernel(g_ref, gb_ref, ga_ref, u_ref, cw_ref, cb_ref, wd_ref, x1_ref, nf_ref,
                 o_ref, acc_ref, *, seq):
    tm = g_ref.shape[0]
    i = pl.program_id(0)
    kidx = pl.program_id(1)
    tiles_per_seq = seq // tm
    first = (i % tiles_per_seq) == 0
    last = (i % tiles_per_seq) == tiles_per_seq - 1

    @pl.when(kidx == 0)
    def _():
        acc_ref[...] = jnp.zeros_like(acc_ref)

    g = g_ref[...].astype(F32)
    row = lax.broadcasted_iota(jnp.int32, (tm, 1), 0)
    prev = pltpu.roll(g, 1, 0)
    prev = jnp.where(row == 0, gb_ref[7:8, :].astype(F32), prev)
    prev = jnp.where(jnp.logical_and(row == 0, first), 0.0, prev)
    nxt = pltpu.roll(g, tm - 1, 0)
    nxt = jnp.where(row == tm - 1, ga_ref[0:1, :].astype(F32), nxt)
    nxt = jnp.where(jnp.logical_and(row == tm - 1, last), 0.0, nxt)
    gt = prev * cw_ref[0:1, :] + g * cw_ref[1:2, :] + nxt * cw_ref[2:3, :] + cb_ref[...]
    act = (jax.nn.silu(gt) * u_ref[...].astype(F32)).astype(BF16)
    acc_ref[...] += jnp.dot(act, wd_ref[...], preferred_element_type=F32)

    @pl.when(kidx == pl.num_programs(1) - 1)
    def _():
        xf = x1_ref[...] + acc_ref[...]
        ms = jnp.mean(xf * xf, axis=-1, keepdims=True)
        o_ref[...] = xf * lax.rsqrt(ms + NORM_EPS) * nf_ref[...]


def _ffn2(g, u, cw, cb, wd, x1, nf, seq, tm=512, tk=512):
    t, f = g.shape
    d = x1.shape[1]
    nb8 = t // 8
    return pl.pallas_call(
        functools.partial(_ffn2_kernel, seq=seq),
        grid=(t // tm, f // tk),
        in_specs=[
            pl.BlockSpec((tm, tk), lambda i, k: (i, k)),
            pl.BlockSpec((8, tk), lambda i, k: (jnp.maximum(i * (tm // 8) - 1, 0), k)),
            pl.BlockSpec((8, tk), lambda i, k: (jnp.minimum((i + 1) * (tm // 8), nb8 - 1), k)),
            pl.BlockSpec((tm, tk), lambda i, k: (i, k)),
            pl.BlockSpec((cw.shape[0], tk), lambda i, k: (0, k)),
            pl.BlockSpec((1, tk), lambda i, k: (0, k)),
            pl.BlockSpec((tk, d), lambda i, k: (k, 0)),
            pl.BlockSpec((tm, d), lambda i, k: (i, 0)),
            pl.BlockSpec((1, d), lambda i, k: (0, 0)),
        ],
        out_specs=pl.BlockSpec((tm, d), lambda i, k: (i, 0)),
        out_shape=jax.ShapeDtypeStruct((t, d), F32),
        scratch_shapes=[pltpu.VMEM((tm, d), F32)],
        compiler_params=_cparams("parallel", "arbitrary"),
        name="ffn_down",
    )(g, g, g, u, cw, cb, wd, x1, nf)


def _pad_cols(a, n):
    return jnp.pad(a, ((0, 0), (0, n - a.shape[1])))


def _pad_rows(a, n):
    return jnp.pad(a, ((0, n - a.shape[0]), (0, 0)))


def _layer(x, batch, seq, norm1_g, w_in, sgu_ln_g, sgu_ln_b, sgu_w, sgu_b,
           mu_prev, mu_next, w0_f, w2_f, a0_f, a2_f, w0_b, w2_b, a0_b, a2_b, k_k, k_a, r_k,
           g2, gn_g, gn_b, w_proj_a, w_proj_b, w_out, norm2_g, ffn_w_gate, ffn_w_up,
           ffn_conv_w, ffn_conv_b, ffn_w_down, norm_out_g):
    d = x.shape[1]
    sgu_width = sgu_ln_g.shape[0]
    width = k_k.shape[0]
    dl = w2_f.shape[0]
    al = a2_f.shape[0]
    feat_w = mu_prev.shape[0]
    feat_pad = -(-feat_w // (4 * LANES)) * (4 * LANES)
    assert dl == HEAD_DIM and al == HEAD_DIM and 2 * dl == LANES
    row = lambda a: a.reshape(1, -1)

    o1 = 2 * sgu_width
    o2 = o1 + feat_w
    w_uv = w_in[:, :o1].astype(BF16)
    w_feat = _pad_cols(w_in[:, o1:o2], feat_pad).astype(BF16)
    w_gates = w_in[:, o2:].astype(BF16)
    zw = jnp.zeros_like(w2_f)
    w2c = jnp.concatenate([jnp.concatenate([w2_f, zw], 1), jnp.concatenate([zw, w2_b], 1)], 0)
    za = jnp.zeros_like(a2_f)
    a2c = jnp.concatenate([jnp.concatenate([a2_f, za], 1), jnp.concatenate([za, a2_b], 1)], 0)
    g2p = _pad_rows(g2, feat_pad - 3 * width - 2 * LANES)
    w0c = row(jnp.concatenate([w0_f, w0_b]))
    a0c = row(jnp.concatenate([a0_f, a0_b]))
    mup = _pad_cols(row(mu_prev), feat_pad)
    mun = _pad_cols(row(mu_next), feat_pad)
    bs_full = jnp.repeat(sgu_b.T, sgu_width // sgu_b.shape[0], axis=1)

    g1 = row(norm1_g)
    uv = _norm_proj(x, g1, w_uv, F32)
    feat = _norm_proj(x, g1, w_feat, F32)
    gates = _norm_proj(x, g1, w_gates, BF16, act="sigmoid")

    y_a = _sgu(uv, row(sgu_ln_g), row(sgu_ln_b), sgu_w.astype(BF16), bs_full)

    sh, dr, bonus, gg = _rwkv_prep(feat, seq, width, mup, mun, w0c, w2c, a0c, a2c, g2p,
                                   row(k_k), row(k_a), row(r_k))
    o = _rwkv_scan(sh, dr, batch, seq)
    y_b = _rwkv_post(o, bonus, gg, row(gn_g), row(gn_b))

    merged = _merge(y_a, y_b, w_proj_a.astype(BF16), w_proj_b.astype(BF16), gates)
    x1, h2 = _outproj(merged, w_out.astype(BF16), x, row(norm2_g))

    gm, um = _ffn1(h2, ffn_w_gate.astype(BF16), ffn_w_up.astype(BF16))
    return _ffn2(gm, um, ffn_conv_w, row(ffn_conv_b), ffn_w_down.astype(BF16), x1,
                 norm_out_g, seq)


def kernel(x, norm1_g, w_in, sgu_ln_g, sgu_ln_b, sgu_w, sgu_b, rwkv_mu_prev, rwkv_mu_next, rwkv_w0_f, rwkv_w2_f, rwkv_a0_f, rwkv_a2_f, rwkv_w0_b, rwkv_w2_b, rwkv_a0_b, rwkv_a2_b, rwkv_k_k, rwkv_k_a, rwkv_r_k, rwkv_g2, rwkv_gn_g, rwkv_gn_b, w_proj_a, w_proj_b, w_out, norm2_g, ffn_w_gate, ffn_w_up, ffn_conv_w, ffn_conv_b, ffn_w_down, norm_f_g):
    batch, seq, d = x.shape
    depth = norm1_g.shape[0]
    assert depth == 1, "the fused final RMSNorm assumes a single layer"
    xf = x.reshape(batch * seq, d)
    per_layer = (norm1_g, w_in, sgu_ln_g, sgu_ln_b, sgu_w, sgu_b, rwkv_mu_prev, rwkv_mu_next,
                 rwkv_w0_f, rwkv_w2_f, rwkv_a0_f, rwkv_a2_f, rwkv_w0_b, rwkv_w2_b, rwkv_a0_b,
                 rwkv_a2_b, rwkv_k_k, rwkv_k_a, rwkv_r_k, rwkv_g2, rwkv_gn_g, rwkv_gn_b,
                 w_proj_a, w_proj_b, w_out, norm2_g, ffn_w_gate, ffn_w_up, ffn_conv_w,
                 ffn_conv_b, ffn_w_down)
    out = _layer(xf, batch, seq, *(p[0] for p in per_layer), norm_f_g.reshape(1, d))
    return out.reshape(batch, seq, d)
```

```python
import functools

import jax
import jax.numpy as jnp
from jax import lax
from jax.experimental import pallas as pl
from jax.experimental.pallas import tpu as pltpu

F32 = jnp.float32
BF16 = jnp.bfloat16

NORM_EPS = 1e-6
LN_EPS = 1e-5
GN_EPS = 64e-5

HEAD_DIM = 64
LANES = 128
SGU_CHUNK = 128
SCAN_CHUNK = 64
VMEM_LIMIT = 48 * 1024 * 1024


def _cparams(*sem):
    return pltpu.CompilerParams(dimension_semantics=sem, vmem_limit_bytes=VMEM_LIMIT)


def _in_proj_kernel(x_ref, g_ref, w_ref, uv_ref, ft_ref, gt_ref, h_ref, *, n_uv, n_ft):
    j = pl.program_id(1)

    @pl.when(j == 0)
    def _():
        x = x_ref[...]
        ms = jnp.mean(x * x, axis=-1, keepdims=True)
        h_ref[...] = (x * lax.rsqrt(ms + NORM_EPS) * g_ref[...]).astype(BF16)

    acc = jnp.dot(h_ref[...], w_ref[...], preferred_element_type=F32)

    @pl.when(j < n_uv)
    def _():
        uv_ref[...] = acc

    @pl.when(jnp.logical_and(j >= n_uv, j < n_uv + n_ft))
    def _():
        ft_ref[...] = acc

    @pl.when(j >= n_uv + n_ft)
    def _():
        gt_ref[...] = jax.nn.sigmoid(acc).astype(gt_ref.dtype)


def _in_proj(x, g, w, w_uv, w_ft, tm=1024, tn=512):
    t, d = x.shape
    n = w.shape[1]
    n_uv, n_ft, n_all = w_uv // tn, w_ft // tn, n // tn
    n_gt = n_all - n_uv - n_ft
    return pl.pallas_call(
        functools.partial(_in_proj_kernel, n_uv=n_uv, n_ft=n_ft),
        grid=(t // tm, n_all),
        in_specs=[
            pl.BlockSpec((tm, d), lambda i, j: (i, 0)),
            pl.BlockSpec((1, d), lambda i, j: (0, 0)),
            pl.BlockSpec((d, tn), lambda i, j: (0, j)),
        ],
        out_specs=[
            pl.BlockSpec((tm, tn), lambda i, j: (i, jnp.minimum(j, n_uv - 1))),
            pl.BlockSpec((tm, tn), lambda i, j: (i, jnp.clip(j - n_uv, 0, n_ft - 1))),
            pl.BlockSpec((tm, tn), lambda i, j: (i, jnp.maximum(j - n_uv - n_ft, 0))),
        ],
        out_shape=[
            jax.ShapeDtypeStruct((t, n_uv * tn), F32),
            jax.ShapeDtypeStruct((t, n_ft * tn), F32),
            jax.ShapeDtypeStruct((t, n_gt * tn), BF16),
        ],
        scratch_shapes=[pltpu.VMEM((tm, d), BF16)],
        compiler_params=_cparams("parallel", "arbitrary"),
        name="in_proj",
    )(x, g, w)


def _gelu(x):
    return 0.5 * x * (1.0 + lax.erf(x * (2.0 ** -0.5)))


def _sgu_kernel(uv_ref, lng_ref, lnb_ref, ws_ref, bs_ref, o_ref, *, width, groups):
    tm = uv_ref.shape[0]
    gv = _gelu(uv_ref[:, width:])
    mu = jnp.mean(gv, axis=-1, keepdims=True)
    vc = gv - mu
    var = jnp.mean(vc * vc, axis=-1, keepdims=True)
    vn = (vc * lax.rsqrt(var + LN_EPS) * lng_ref[...] + lnb_ref[...]).astype(BF16)
    gd = width // groups
    for c in range(tm // SGU_CHUNK):
        rows = slice(c * SGU_CHUNK, (c + 1) * SGU_CHUNK)
        for g in range(groups):
            cols = slice(g * gd, (g + 1) * gd)
            mixed = jnp.dot(ws_ref[g], vn[rows, cols], preferred_element_type=F32)
            gu = _gelu(uv_ref[rows, cols])
            o_ref[rows, cols] = (gu * (mixed + bs_ref[:, cols])).astype(o_ref.dtype)


def _sgu(uv, ln_g, ln_b, ws, bs_full, tm=256):
    t, w2 = uv.shape
    width = w2 // 2
    groups = ws.shape[0]
    return pl.pallas_call(
        functools.partial(_sgu_kernel, width=width, groups=groups),
        grid=(t // tm,),
        in_specs=[
            pl.BlockSpec((tm, w2), lambda i: (i, 0)),
            pl.BlockSpec((1, width), lambda i: (0, 0)),
            pl.BlockSpec((1, width), lambda i: (0, 0)),
            pl.BlockSpec(ws.shape, lambda i: (0, 0, 0)),
            pl.BlockSpec(bs_full.shape, lambda i: (0, 0)),
        ],
        out_specs=pl.BlockSpec((tm, width), lambda i: (i, 0)),
        out_shape=jax.ShapeDtypeStruct((t, width), BF16),
        compiler_params=_cparams("parallel"),
        name="sgu_mixer",
    )(uv, ln_g, ln_b, ws, bs_full)


def _head_block_ones(scale):
    r = lax.broadcasted_iota(jnp.int32, (LANES, LANES), 0) // HEAD_DIM
    c = lax.broadcasted_iota(jnp.int32, (LANES, LANES), 1) // HEAD_DIM
    return jnp.where(r == c, scale, 0.0).astype(BF16)


def _split_bf16(x):
    hi = x.astype(BF16)
    return hi, (x - hi.astype(F32)).astype(BF16)


def _head_sum(x, bd):
    hi, lo = _split_bf16(x)
    return (jnp.dot(hi, bd, preferred_element_type=F32)
            + jnp.dot(lo, bd, preferred_element_type=F32))


def _dot_split3(x, w_hi, w_lo):
    hi, lo = _split_bf16(x)
    return (jnp.dot(hi, w_hi, preferred_element_type=F32)
            + jnp.dot(hi, w_lo, preferred_element_type=F32)
            + jnp.dot(lo, w_hi, preferred_element_type=F32))


def _shifted_rows(x, before_ref, after_ref, first_row_of_seq, last_row_of_seq):
    tm = x.shape[0]
    row = lax.broadcasted_iota(jnp.int32, (tm, 1), 0)
    prev = pltpu.roll(x, 1, 0)
    prev = jnp.where(row == 0, before_ref[7:8, :], prev)
    prev = jnp.where(jnp.logical_and(row == 0, first_row_of_seq), 0.0, prev)
    nxt = pltpu.roll(x, tm - 1, 0)
    nxt = jnp.where(row == tm - 1, after_ref[0:1, :], nxt)
    nxt = jnp.where(jnp.logical_and(row == tm - 1, last_row_of_seq), 0.0, nxt)
    return prev, nxt


def _prep_kernel(f_ref, fb_ref, fa_ref, mup_ref, mun_ref, w0_ref, w2_ref, a0_ref, a2_ref,
                 g2_ref, kk_ref, ka_ref, rk_ref,
                 sh_ref, dr_ref, bonus_ref, gg_ref, *, seq, width):
    tm = f_ref.shape[0]
    i = pl.program_id(0)
    tiles_per_seq = seq // tm
    first = (i % tiles_per_seq) == 0
    last = (i % tiles_per_seq) == tiles_per_seq - 1
    x = f_ref[...]
    prev, nxt = _shifted_rows(x, fb_ref, fa_ref, first, last)
    xs = x + mup_ref[...] * (prev - x) + mun_ref[...] * (nxt - x)

    r = xs[:, 0:width]
    k = xs[:, width:2 * width]
    v = xs[:, 2 * width:3 * width]
    o = 3 * width
    lo_w = xs[:, o:o + LANES]
    lo_a = xs[:, o + LANES:o + 2 * LANES]
    lo_g = xs[:, o + 2 * LANES:]

    wpre = w0_ref[...] + _dot_split3(jnp.tanh(lo_w), w2_ref[0], w2_ref[1])
    z = -wpre
    softplus = jnp.maximum(z, 0.0) + jnp.log(1.0 + jnp.exp(-jnp.abs(z)))
    lw = -jnp.exp(-softplus - 0.5)
    a = jax.nn.sigmoid(a0_ref[...] + _dot_split3(lo_a, a2_ref[0], a2_ref[1]))
    gg_ref[...] = _dot_split3(jax.nn.sigmoid(lo_g), g2_ref[0], g2_ref[1])

    bd = _head_block_ones(1.0)
    kk = k * kk_ref[...]
    sh_ref[0] = r
    sh_ref[1] = v
    for j in range(width // LANES):
        cols = slice(j * LANES, (j + 1) * LANES)
        kkj = kk[:, cols]
        ss = _head_sum(kkj * kkj, bd)
        kkn = kkj * lax.rsqrt(jnp.maximum(ss, 1e-24))
        sh_ref[2, :, cols] = kkn
        ksum = jnp.zeros_like(kkj)
        for d in range(2):
            dcols = slice(d * width + j * LANES, d * width + (j + 1) * LANES)
            a_d = a[:, dcols]
            k_d = k[:, cols] * (1.0 + (a_d - 1.0) * ka_ref[:, cols])
            dr_ref[d, 0, :, cols] = lw[:, dcols]
            dr_ref[d, 1, :, cols] = kkn * a_d
            dr_ref[d, 2, :, cols] = k_d
            ksum = ksum + k_d
        bsum = _head_sum(r[:, cols] * ksum * rk_ref[:, cols], bd)
        bonus_ref[:, cols] = bsum * v[:, cols]


def _rwkv_prep(feat, seq, width, mup, mun, w0c, w2c, a0c, a2c, g2p, k_k, k_a, r_k, tm=128):
    t, fw = feat.shape
    nb8 = t // 8
    row = lambda i: (0, 0)
    return pl.pallas_call(
        functools.partial(_prep_kernel, seq=seq, width=width),
        grid=(t // tm,),
        in_specs=[
            pl.BlockSpec((tm, fw), lambda i: (i, 0)),
            pl.BlockSpec((8, fw), lambda i: (jnp.maximum(i * (tm // 8) - 1, 0), 0)),
            pl.BlockSpec((8, fw), lambda i: (jnp.minimum((i + 1) * (tm // 8), nb8 - 1), 0)),
            pl.BlockSpec((1, fw), row),
            pl.BlockSpec((1, fw), row),
            pl.BlockSpec(w0c.shape, row),
            pl.BlockSpec(w2c.shape, lambda i: (0, 0, 0)),
            pl.BlockSpec(a0c.shape, row),
            pl.BlockSpec(a2c.shape, lambda i: (0, 0, 0)),
            pl.BlockSpec(g2p.shape, lambda i: (0, 0, 0)),
            pl.BlockSpec((1, width), row),
            pl.BlockSpec((1, width), row),
            pl.BlockSpec((1, width), row),
        ],
        out_specs=[
            pl.BlockSpec((3, tm, width), lambda i: (0, i, 0)),
            pl.BlockSpec((2, 3, tm, width), lambda i: (0, 0, i, 0)),
            pl.BlockSpec((tm, width), lambda i: (i, 0)),
            pl.BlockSpec((tm, width), lambda i: (i, 0)),
        ],
        out_shape=[
            jax.ShapeDtypeStruct((3, t, width), F32),
            jax.ShapeDtypeStruct((2, 3, t, width), F32),
            jax.ShapeDtypeStruct((t, width), F32),
            jax.ShapeDtypeStruct((t, width), F32),
        ],
        compiler_params=_cparams("parallel"),
        name="rwkv_prep",
    )(feat, feat, feat, mup, mun, w0c, w2c, a0c, a2c, g2p, k_k, k_a, r_k)


def _hat(x, first_half):
    return jnp.concatenate([jnp.where(first_half, x, 0.0), jnp.where(first_half, 0.0, x)],
                           axis=0)


def _scan_kernel(sh_ref, dr_ref, o_ref, s_ref, *, group):
    c = SCAN_CHUNK
    batch, width = sh_ref.shape[1], sh_ref.shape[3]
    assert 2 * c == LANES and c == HEAD_DIM
    sign = 1 - 2 * pl.program_id(0)

    @pl.when(pl.program_id(1) == 0)
    def _():
        s_ref[...] = jnp.zeros_like(s_ref)

    bf = lambda t: t.astype(BF16)
    dg = lambda p, q, dims: lax.dot_general(bf(p), bf(q), (dims, ((), ())),
                                            preferred_element_type=F32)
    dot = lambda p, q: dg(p, q, ((1,), (0,)))
    dot_nt = lambda p, q: dg(p, q, ((1,), (1,)))
    dot_tn = lambda p, q: dg(p, q, ((0,), (0,)))

    forward = pl.program_id(0) == 0
    wt = lax.broadcasted_iota(jnp.int32, (c, LANES), 0)
    wl = lax.broadcasted_iota(jnp.int32, (c, LANES), 1)
    order = sign * (wt - wl % c)
    strict = order > 0
    incl = order >= 0
    eye = jnp.where(wt == wl % c, 1.0, 0.0).astype(F32)
    half = wl < c
    same_head = (lax.broadcasted_iota(jnp.int32, (LANES, LANES), 0) // HEAD_DIM
                 == lax.broadcasted_iota(jnp.int32, (LANES, LANES), 1) // HEAD_DIM)
    hat = lambda t: _hat(t, half)

    units = [(b, p) for b in range(batch) for p in range(width // LANES)]
    for g0 in range(0, len(units), group):
        grp = units[g0:g0 + group]
        st = {}
        for (b, p) in grp:
            cols = slice(p * LANES, (p + 1) * LANES)
            r = sh_ref[0, b, :, cols]
            v = sh_ref[1, b, :, cols]
            kk = sh_ref[2, b, :, cols]
            lw = dr_ref[0, 0, b, :, cols]
            bb = dr_ref[0, 1, b, :, cols]
            kd = dr_ref[0, 2, b, :, cols]
            pre = lw
            s = 1
            while s < c:
                pre = pre + jnp.where(wt >= s, pltpu.roll(pre, s, 0), 0.0)
                s *= 2
            gtot = pre[c - 1:c, :]
            gi = jnp.where(forward, pre, gtot - pre + lw)
            en = jnp.exp(-gi)
            ec = jnp.exp(gtot - gi)
            lhs = bf(jnp.concatenate([-kk * jnp.exp(gi - lw), r * jnp.exp(gi)], axis=0))
            rhs = jnp.concatenate([hat(bb * en), hat(kd * en)], axis=0)
            aa = dot_nt(lhs, rhs)
            a_ab = jnp.where(strict, aa[:c, :LANES], 0.0)
            a_ak = jnp.where(strict, aa[:c, LANES:], 0.0)
            ark = bf(jnp.concatenate([jnp.where(incl, aa[c:, :LANES], 0.0),
                                      jnp.where(incl, aa[c:, LANES:], 0.0)], axis=1))
            vh = bf(hat(v))
            s0 = s_ref[b * (width // LANES) + p]
            ls = dot_nt(lhs, s0)
            st[(b, p)] = dict(
                cols=cols, v=v, s0=s0, ark=ark, vh=vh,
                bk=bf(jnp.concatenate([bb * ec, kd * ec], axis=0)),
                eg=jnp.exp(gtot), pw=a_ab, inv=eye + a_ab,
                rhs_u=ls[:c] + dot(a_ak, vh), rs=ls[c:])

        for u in grp:
            st[u]["pw"] = dot(st[u]["pw"], hat(st[u]["pw"]))
        n = 4
        while n < c:
            for u in grp:
                d = st[u]
                both = dot(jnp.concatenate([d["pw"], d["inv"]], axis=0), hat(d["pw"]))
                d["pw"] = both[:c]
                d["inv"] = d["inv"] + both[c:]
            n *= 2
        for u in grp:
            st[u]["inv"] = st[u]["inv"] + dot(st[u]["inv"], hat(st[u]["pw"]))

        for u in grp:
            st[u]["u"] = dot(st[u]["inv"], hat(st[u]["rhs_u"]))
        for (b, p) in grp:
            d = st[(b, p)]
            uvh = jnp.concatenate([bf(hat(d["u"])), d["vh"]], axis=0)
            o_ref[0, b, :, d["cols"]] = d["rs"] + dot(d["ark"], uvh)
        for (b, p) in grp:
            d = st[(b, p)]
            uv = jnp.concatenate([d["u"], d["v"]], axis=0)
            s_ref[b * (width // LANES) + p] = (d["s0"] * d["eg"]
                                               + jnp.where(same_head, dot_tn(uv, d["bk"]), 0.0))


def _rwkv_scan(sh, dr, batch, seq, group=16):
    _, t, width = sh.shape
    c = SCAN_CHUNK
    nc = seq // c
    sh = sh.reshape(3, batch, seq, width)
    dr = dr.reshape(2, 3, batch, seq, width)
    blk = lambda d, i: i + d * (nc - 1 - 2 * i)
    o = pl.pallas_call(
        functools.partial(_scan_kernel, group=group),
        grid=(2, nc),
        in_specs=[
            pl.BlockSpec((3, batch, c, width), lambda d, i: (0, 0, blk(d, i), 0)),
            pl.BlockSpec((1, 3, batch, c, width), lambda d, i: (d, 0, 0, blk(d, i), 0)),
        ],
        out_specs=pl.BlockSpec((1, batch, c, width), lambda d, i: (d, 0, blk(d, i), 0)),
        out_shape=jax.ShapeDtypeStruct((2, batch, seq, width), F32),
        scratch_shapes=[pltpu.VMEM((batch * (width // LANES), LANES, LANES), F32)],
        compiler_params=_cparams("parallel", "arbitrary"),
        name="rwkv_scan",
    )(sh, dr)
    return o.reshape(2, t, width)


def _post_kernel(o_ref, bonus_ref, gg_ref, gng_ref, gnb_ref, y_ref):
    width = y_ref.shape[1]
    avg = _head_block_ones(1.0 / HEAD_DIM)
    for j in range(width // LANES):
        cols = slice(j * LANES, (j + 1) * LANES)
        o = o_ref[0, :, cols] + o_ref[1, :, cols]
        oc = o - _head_sum(o, avg)
        var = _head_sum(oc * oc, avg)
        y = oc * lax.rsqrt(var + GN_EPS) * gng_ref[:, cols] + gnb_ref[:, cols]
        y_ref[:, cols] = ((y + bonus_ref[:, cols]) * gg_ref[:, cols]).astype(y_ref.dtype)


def _rwkv_post(o, bonus, gg, gn_g, gn_b, tm=512, out_dtype=BF16):
    _, t, width = o.shape
    return pl.pallas_call(
        _post_kernel,
        grid=(t // tm,),
        in_specs=[
            pl.BlockSpec((2, tm, width), lambda i: (0, i, 0)),
            pl.BlockSpec((tm, width), lambda i: (i, 0)),
            pl.BlockSpec((tm, width), lambda i: (i, 0)),
            pl.BlockSpec((1, width), lambda i: (0, 0)),
            pl.BlockSpec((1, width), lambda i: (0, 0)),
        ],
        out_specs=pl.BlockSpec((tm, width), lambda i: (i, 0)),
        out_shape=jax.ShapeDtypeStruct((t, width), out_dtype),
        compiler_params=_cparams("parallel"),
        name="rwkv_post",
    )(o, bonus, gg, gn_g, gn_b)


def _merge_kernel(ya_ref, yb_ref, wa_ref, wb_ref, ga_ref, gb_ref, o_ref):
    pa = jnp.dot(ya_ref[...], wa_ref[...], preferred_element_type=F32)
    pb = jnp.dot(yb_ref[...], wb_ref[...], preferred_element_type=F32)
    o_ref[...] = (ga_ref[...].astype(F32) * pa + gb_ref[...].astype(F32) * pb).astype(o_ref.dtype)


def _merge(ya, yb, wa, wb, gates, tm=1024, tn=512):
    t, k = ya.shape
    n = wa.shape[1]
    nb = n // tn
    return pl.pallas_call(
        _merge_kernel,
        grid=(t // tm, nb),
        in_specs=[
            pl.BlockSpec((tm, k), lambda i, j: (i, 0)),
            pl.BlockSpec((tm, k), lambda i, j: (i, 0)),
            pl.BlockSpec((k, tn), lambda i, j: (0, j)),
            pl.BlockSpec((k, tn), lambda i, j: (0, j)),
            pl.BlockSpec((tm, tn), lambda i, j: (i, j)),
            pl.BlockSpec((tm, tn), lambda i, j: (i, j + nb)),
        ],
        out_specs=pl.BlockSpec((tm, tn), lambda i, j: (i, j)),
        out_shape=jax.ShapeDtypeStruct((t, n), BF16),
        compiler_params=_cparams("parallel", "arbitrary"),
        name="merge_proj",
    )(ya, yb, wa, wb, gates, gates)


def _outproj_kernel(m_ref, w_ref, x_ref, g_ref, x1_ref, h2_ref):
    x1 = x_ref[...] + jnp.dot(m_ref[...], w_ref[...], preferred_element_type=F32)
    x1_ref[...] = x1
    ms = jnp.mean(x1 * x1, axis=-1, keepdims=True)
    h2_ref[...] = (x1 * lax.rsqrt(ms + NORM_EPS) * g_ref[...]).astype(h2_ref.dtype)


def _outproj(m, w, x, g, tm=256):
    t, d = x.shape
    return pl.pallas_call(
        _outproj_kernel,
        grid=(t // tm,),
        in_specs=[
            pl.BlockSpec((tm, d), lambda i: (i, 0)),
            pl.BlockSpec((d, d), lambda i: (0, 0)),
            pl.BlockSpec((tm, d), lambda i: (i, 0)),
            pl.BlockSpec((1, d), lambda i: (0, 0)),
        ],
        out_specs=[
            pl.BlockSpec((tm, d), lambda i: (i, 0)),
            pl.BlockSpec((tm, d), lambda i: (i, 0)),
        ],
        out_shape=[jax.ShapeDtypeStruct((t, d), F32), jax.ShapeDtypeStruct((t, d), BF16)],
        compiler_params=_cparams("parallel"),
        name="out_proj",
    )(m, w, x, g)


def _ffn1_kernel(h_ref, hb_ref, ha_ref, wg_ref, wu_ref, cw_ref, cb_ref, a_ref, *, seq):
    tm = h_ref.shape[0]
    halo = hb_ref.shape[0]
    i = pl.program_id(0)
    tiles_per_seq = seq // tm
    first = (i % tiles_per_seq) == 0
    last = (i % tiles_per_seq) == tiles_per_seq - 1

    h = h_ref[...]
    h_ext = jnp.concatenate([hb_ref[...], h, ha_ref[...]], axis=0)
    g_ext = jnp.dot(h_ext, wg_ref[...], preferred_element_type=F32)
    up = jnp.dot(h, wu_ref[...], preferred_element_type=F32)
    rows = tm + 2 * halo
    row = lax.broadcasted_iota(jnp.int32, (tm, 1), 0)
    g = g_ext[halo:halo + tm]
    prev = pltpu.roll(g_ext, 1, 0)[halo:halo + tm]
    prev = jnp.where(jnp.logical_and(row == 0, first), 0.0, prev)
    nxt = pltpu.roll(g_ext, rows - 1, 0)[halo:halo + tm]
    nxt = jnp.where(jnp.logical_and(row == tm - 1, last), 0.0, nxt)
    gt = prev * cw_ref[0:1, :] + g * cw_ref[1:2, :] + nxt * cw_ref[2:3, :] + cb_ref[...]
    a_ref[...] = (jax.nn.silu(gt) * up).astype(a_ref.dtype)


def _ffn1(h, wg, wu, cw, cb, seq, tm=1024, tn=512, halo=16):
    t, d = h.shape
    f = wg.shape[1]
    nbh = t // halo
    return pl.pallas_call(
        functools.partial(_ffn1_kernel, seq=seq),
        grid=(t // tm, f // tn),
        in_specs=[
            pl.BlockSpec((tm, d), lambda i, j: (i, 0)),
            pl.BlockSpec((halo, d), lambda i, j: (jnp.maximum(i * (tm // halo) - 1, 0), 0)),
            pl.BlockSpec((halo, d), lambda i, j: (jnp.minimum((i + 1) * (tm // halo), nbh - 1), 0)),
            pl.BlockSpec((d, tn), lambda i, j: (0, j)),
            pl.BlockSpec((d, tn), lambda i, j: (0, j)),
            pl.BlockSpec((cw.shape[0], tn), lambda i, j: (0, j)),
            pl.BlockSpec((1, tn), lambda i, j: (0, j)),
        ],
        out_specs=pl.BlockSpec((tm, tn), lambda i, j: (i, j)),
        out_shape=jax.ShapeDtypeStruct((t, f), BF16),
        compiler_params=_cparams("parallel", "arbitrary"),
        name="ffn_gate_up",
    )(h, h, h, wg, wu, cw, cb)


def _ffn2_kernel(a_ref, wd_ref, x1_ref, nf_ref, o_ref):
    kidx = pl.program_id(1)
    part = jnp.dot(a_ref[...], wd_ref[...], preferred_element_type=F32)

    @pl.when(kidx == 0)
    def _():
        o_ref[...] = x1_ref[...] + part

    @pl.when(kidx > 0)
    def _():
        o_ref[...] += part

    @pl.when(kidx == pl.num_programs(1) - 1)
    def _():
        xf = o_ref[...]
        ms = jnp.mean(xf * xf, axis=-1, keepdims=True)
        o_ref[...] = xf * lax.rsqrt(ms + NORM_EPS) * nf_ref[...]


def _ffn2(act, wd, x1, nf, tm=512, tk=1408):
    t, f = act.shape
    d = x1.shape[1]
    return pl.pallas_call(
        _ffn2_kernel,
        grid=(t // tm, f // tk),
        in_specs=[
            pl.BlockSpec((tm, tk), lambda i, k: (i, k)),
            pl.BlockSpec((tk, d), lambda i, k: (k, 0)),
            pl.BlockSpec((tm, d), lambda i, k: (i, 0)),
            pl.BlockSpec((1, d), lambda i, k: (0, 0)),
        ],
        out_specs=pl.BlockSpec((tm, d), lambda i, k: (i, 0)),
        out_shape=jax.ShapeDtypeStruct((t, d), F32),
        compiler_params=_cparams("parallel", "arbitrary"),
        name="ffn_down",
    )(act, wd, x1, nf)


def _pad_cols(a, n):
    return jnp.pad(a, ((0, 0), (0, n - a.shape[1])))


def _pad_rows(a, n):
    return jnp.pad(a, ((0, n - a.shape[0]), (0, 0)))


def _layer(x, batch, seq, norm1_g, w_in, sgu_ln_g, sgu_ln_b, sgu_w, sgu_b,
           mu_prev, mu_next, w0_f, w2_f, a0_f, a2_f, w0_b, w2_b, a0_b, a2_b, k_k, k_a, r_k,
           g2, gn_g, gn_b, w_proj_a, w_proj_b, w_out, norm2_g, ffn_w_gate, ffn_w_up,
           ffn_conv_w, ffn_conv_b, ffn_w_down, norm_out_g):
    d = x.shape[1]
    sgu_width = sgu_ln_g.shape[0]
    width = k_k.shape[0]
    dl = w2_f.shape[0]
    al = a2_f.shape[0]
    feat_w = mu_prev.shape[0]
    feat_pad = -(-feat_w // (4 * LANES)) * (4 * LANES)
    assert dl == HEAD_DIM and al == HEAD_DIM and 2 * dl == LANES
    row = lambda a: a.reshape(1, -1)

    o1 = 2 * sgu_width
    o2 = o1 + feat_w
    w_all = jnp.concatenate(
        [w_in[:, :o2], jnp.zeros((d, feat_pad - feat_w), w_in.dtype), w_in[:, o2:]],
        axis=1).astype(BF16)
    hi_lo = lambda w: jnp.stack(_split_bf16(w))
    zw = jnp.zeros_like(w2_f)
    w2c = hi_lo(jnp.concatenate([jnp.concatenate([w2_f, zw], 1),
                                 jnp.concatenate([zw, w2_b], 1)], 0))
    za = jnp.zeros_like(a2_f)
    a2c = hi_lo(jnp.concatenate([jnp.concatenate([a2_f, za], 1),
                                 jnp.concatenate([za, a2_b], 1)], 0))
    g2p = hi_lo(_pad_rows(g2, feat_pad - 3 * width - 2 * LANES))
    w0c = row(jnp.concatenate([w0_f, w0_b]))
    a0c = row(jnp.concatenate([a0_f, a0_b]))
    mup = _pad_cols(row(mu_prev), feat_pad)
    mun = _pad_cols(row(mu_next), feat_pad)
    bs_full = jnp.repeat(sgu_b.T, sgu_width // sgu_b.shape[0], axis=1)

    g1 = row(norm1_g)
    uv, feat, gates = _in_proj(x, g1, w_all, o1, feat_pad)

    y_a = _sgu(uv, row(sgu_ln_g), row(sgu_ln_b), sgu_w.astype(BF16), bs_full)

    sh, dr, bonus, gg = _rwkv_prep(feat, seq, width, mup, mun, w0c, w2c, a0c, a2c, g2p,
                                   row(k_k), row(k_a), row(r_k))
    o = _rwkv_scan(sh, dr, batch, seq)
    y_b = _rwkv_post(o, bonus, gg, row(gn_g), row(gn_b))

    merged = _merge(y_a, y_b, w_proj_a.astype(BF16), w_proj_b.astype(BF16), gates)
    x1, h2 = _outproj(merged, w_out.astype(BF16), x, row(norm2_g))

    act = _ffn1(h2, ffn_w_gate.astype(BF16), ffn_w_up.astype(BF16), ffn_conv_w,
                row(ffn_conv_b), seq)
    return _ffn2(act, ffn_w_down.astype(BF16), x1, norm_out_g)


def kernel(x, norm1_g, w_in, sgu_ln_g, sgu_ln_b, sgu_w, sgu_b, rwkv_mu_prev, rwkv_mu_next, rwkv_w0_f, rwkv_w2_f, rwkv_a0_f, rwkv_a2_f, rwkv_w0_b, rwkv_w2_b, rwkv_a0_b, rwkv_a2_b, rwkv_k_k, rwkv_k_a, rwkv_r_k, rwkv_g2, rwkv_gn_g, rwkv_gn_b, w_proj_a, w_proj_b, w_out, norm2_g, ffn_w_gate, ffn_w_up, ffn_conv_w, ffn_conv_b, ffn_w_down, norm_f_g):
    batch, seq, d = x.shape
    depth = norm1_g.shape[0]
    assert depth == 1, "the fused final RMSNorm assumes a single layer"
    xf = x.reshape(batch * seq, d)
    per_layer = (norm1_g, w_in, sgu_ln_g, sgu_ln_b, sgu_w, sgu_b, rwkv_mu_prev, rwkv_mu_next,
                 rwkv_w0_f, rwkv_w2_f, rwkv_a0_f, rwkv_a2_f, rwkv_w0_b, rwkv_w2_b, rwkv_a0_b,
                 rwkv_a2_b, rwkv_k_k, rwkv_k_a, rwkv_r_k, rwkv_g2, rwkv_gn_g, rwkv_gn_b,
                 w_proj_a, w_proj_b, w_out, norm2_g, ffn_w_gate, ffn_w_up, ffn_conv_w,
                 ffn_conv_b, ffn_w_down)
    out = _layer(xf, batch, seq, *(p[0] for p in per_layer), norm_f_g.reshape(1, d))
    return out.reshape(batch, seq, d)
```

```python
import functools

import jax
import jax.numpy as jnp
from jax import lax
from jax.experimental import pallas as pl
from jax.experimental.pallas import tpu as pltpu

F32 = jnp.float32
BF16 = jnp.bfloat16

NORM_EPS = 1e-6
LN_EPS = 1e-5
GN_EPS = 64e-5

HEAD_DIM = 64
LANES = 128
SGU_CHUNK = 128
SCAN_CHUNK = 64
VMEM_LIMIT = 48 * 1024 * 1024


def _cparams(*sem):
    return pltpu.CompilerParams(dimension_semantics=sem, vmem_limit_bytes=VMEM_LIMIT)


def _in_proj_kernel(x_ref, g_ref, w_ref, wg_ref, uv_ref, ft_ref, gt_ref, h_ref, *, n_uv, n_ft):
    j = pl.program_id(1)

    @pl.when(j == 0)
    def _():
        x = x_ref[...]
        ms = jnp.mean(x * x, axis=-1, keepdims=True)
        h_ref[...] = (x * lax.rsqrt(ms + NORM_EPS) * g_ref[...]).astype(BF16)

    def head_cols():
        return jnp.dot(h_ref[...], w_ref[...].astype(BF16), preferred_element_type=F32)

    @pl.when(j < n_uv)
    def _():
        uv_ref[...] = head_cols().astype(uv_ref.dtype)

    @pl.when(jnp.logical_and(j >= n_uv, j < n_uv + n_ft))
    def _():
        ft_ref[...] = head_cols().astype(ft_ref.dtype)

    @pl.when(j >= n_uv + n_ft)
    def _():
        acc = jnp.dot(h_ref[...], wg_ref[...], preferred_element_type=F32)
        gt_ref[...] = jax.nn.sigmoid(acc).astype(gt_ref.dtype)


def _in_proj(x, g, w_in, w_gates, w_uv, w_ft, tm=1024, tn=512):
    t, d = x.shape
    n_uv, n_ft, n_gt = w_uv // tn, w_ft // tn, w_gates.shape[1] // tn
    n_head = n_uv + n_ft
    return pl.pallas_call(
        functools.partial(_in_proj_kernel, n_uv=n_uv, n_ft=n_ft),
        grid=(t // tm, n_head + n_gt),
        in_specs=[
            pl.BlockSpec((tm, d), lambda i, j: (i, 0)),
            pl.BlockSpec((1, d), lambda i, j: (0, 0)),
            pl.BlockSpec((d, tn), lambda i, j: (0, jnp.minimum(j, n_head - 1))),
            pl.BlockSpec((d, tn), lambda i, j: (0, jnp.maximum(j - n_head, 0))),
        ],
        out_specs=[
            pl.BlockSpec((tm, tn), lambda i, j: (i, jnp.minimum(j, n_uv - 1))),
            pl.BlockSpec((tm, tn), lambda i, j: (i, jnp.clip(j - n_uv, 0, n_ft - 1))),
            pl.BlockSpec((tm, tn), lambda i, j: (i, jnp.maximum(j - n_head, 0))),
        ],
        out_shape=[
            jax.ShapeDtypeStruct((t, n_uv * tn), BF16),
            jax.ShapeDtypeStruct((t, n_ft * tn), BF16),
            jax.ShapeDtypeStruct((t, n_gt * tn), BF16),
        ],
        scratch_shapes=[pltpu.VMEM((tm, d), BF16)],
        compiler_params=_cparams("parallel", "arbitrary"),
        name="in_proj",
    )(x, g, w_in, w_gates)


def _gelu(x):
    return 0.5 * x * (1.0 + lax.erf(x * (2.0 ** -0.5)))


def _sgu_kernel(uv_ref, lng_ref, lnb_ref, ws_ref, bs_ref, o_ref, *, width, groups):
    tm = uv_ref.shape[0]
    gv = _gelu(uv_ref[:, width:].astype(F32))
    mu = jnp.mean(gv, axis=-1, keepdims=True)
    vc = gv - mu
    var = jnp.mean(vc * vc, axis=-1, keepdims=True)
    vn = (vc * lax.rsqrt(var + LN_EPS) * lng_ref[...] + lnb_ref[...]).astype(BF16)
    gd = width // groups
    for c in range(tm // SGU_CHUNK):
        rows = slice(c * SGU_CHUNK, (c + 1) * SGU_CHUNK)
        for g in range(groups):
            cols = slice(g * gd, (g + 1) * gd)
            mixed = jnp.dot(ws_ref[g], vn[rows, cols], preferred_element_type=F32)
            gu = _gelu(uv_ref[rows, cols].astype(F32))
            o_ref[rows, cols] = (gu * (mixed + bs_ref[:, cols])).astype(o_ref.dtype)


def _sgu(uv, ln_g, ln_b, ws, bs_full, tm=256):
    t, w2 = uv.shape
    width = w2 // 2
    groups = ws.shape[0]
    return pl.pallas_call(
        functools.partial(_sgu_kernel, width=width, groups=groups),
        grid=(t // tm,),
        in_specs=[
            pl.BlockSpec((tm, w2), lambda i: (i, 0)),
            pl.BlockSpec((1, width), lambda i: (0, 0)),
            pl.BlockSpec((1, width), lambda i: (0, 0)),
            pl.BlockSpec(ws.shape, lambda i: (0, 0, 0)),
            pl.BlockSpec(bs_full.shape, lambda i: (0, 0)),
        ],
        out_specs=pl.BlockSpec((tm, width), lambda i: (i, 0)),
        out_shape=jax.ShapeDtypeStruct((t, width), BF16),
        compiler_params=_cparams("parallel"),
        name="sgu_mixer",
    )(uv, ln_g, ln_b, ws, bs_full)


def _head_block_ones(scale):
    r = lax.broadcasted_iota(jnp.int32, (LANES, LANES), 0) // HEAD_DIM
    c = lax.broadcasted_iota(jnp.int32, (LANES, LANES), 1) // HEAD_DIM
    return jnp.where(r == c, scale, 0.0).astype(BF16)


def _split_bf16(x):
    hi = x.astype(BF16)
    return hi, (x - hi.astype(F32)).astype(BF16)


def _head_sum(x, bd):
    hi, lo = _split_bf16(x)
    return (jnp.dot(hi, bd, preferred_element_type=F32)
            + jnp.dot(lo, bd, preferred_element_type=F32))


def _dot_split3(x, w_hi, w_lo):
    hi, lo = _split_bf16(x)
    return (jnp.dot(hi, w_hi, preferred_element_type=F32)
            + jnp.dot(hi, w_lo, preferred_element_type=F32)
            + jnp.dot(lo, w_hi, preferred_element_type=F32))


def _shifted_rows(x, before_ref, after_ref, first_row_of_seq, last_row_of_seq):
    tm = x.shape[0]
    row = lax.broadcasted_iota(jnp.int32, (tm, 1), 0)
    prev = pltpu.roll(x, 1, 0)
    halo = before_ref.shape[0]
    prev = jnp.where(row == 0, before_ref[...].astype(F32)[halo - 1:halo], prev)
    prev = jnp.where(jnp.logical_and(row == 0, first_row_of_seq), 0.0, prev)
    nxt = pltpu.roll(x, tm - 1, 0)
    nxt = jnp.where(row == tm - 1, after_ref[...].astype(F32)[0:1], nxt)
    nxt = jnp.where(jnp.logical_and(row == tm - 1, last_row_of_seq), 0.0, nxt)
    return prev, nxt


def _prep_kernel(f_ref, fb_ref, fa_ref, mup_ref, mun_ref, w0_ref, w2_ref, a0_ref, a2_ref,
                 g2_ref, kk_ref, ka_ref, rk_ref,
                 sh_ref, lw_ref, bk_ref, bonus_ref, gg_ref, *, seq, width):
    tm = f_ref.shape[0]
    i = pl.program_id(0)
    tiles_per_seq = seq // tm
    first = (i % tiles_per_seq) == 0
    last = (i % tiles_per_seq) == tiles_per_seq - 1
    x = f_ref[...].astype(F32)
    prev, nxt = _shifted_rows(x, fb_ref, fa_ref, first, last)
    xs = x + mup_ref[...] * (prev - x) + mun_ref[...] * (nxt - x)

    r = xs[:, 0:width]
    k = xs[:, width:2 * width]
    v = xs[:, 2 * width:3 * width]
    o = 3 * width
    lo_w = xs[:, o:o + LANES]
    lo_a = xs[:, o + LANES:o + 2 * LANES]
    lo_g = xs[:, o + 2 * LANES:]

    wpre = w0_ref[...] + _dot_split3(jnp.tanh(lo_w), w2_ref[0], w2_ref[1])
    z = -wpre
    softplus = jnp.maximum(z, 0.0) + jnp.log(1.0 + jnp.exp(-jnp.abs(z)))
    lw = -jnp.exp(-softplus - 0.5)
    a = jax.nn.sigmoid(a0_ref[...] + _dot_split3(lo_a, a2_ref[0], a2_ref[1]))
    gg_ref[...] = _dot_split3(jax.nn.sigmoid(lo_g), g2_ref[0], g2_ref[1]).astype(gg_ref.dtype)
    lw_ref[0] = lw[:, :width]
    lw_ref[1] = lw[:, width:]

    bd = _head_block_ones(1.0)
    kk = k * kk_ref[...]
    sh_ref[0] = r.astype(sh_ref.dtype)
    sh_ref[1] = v.astype(sh_ref.dtype)
    for j in range(width // LANES):
        cols = slice(j * LANES, (j + 1) * LANES)
        kkj = kk[:, cols]
        ss = _head_sum(kkj * kkj, bd)
        kkn = kkj * lax.rsqrt(jnp.maximum(ss, 1e-24))
        sh_ref[2, :, cols] = kkn.astype(sh_ref.dtype)
        ksum = jnp.zeros_like(kkj)
        for d in range(2):
            dcols = slice(d * width + j * LANES, d * width + (j + 1) * LANES)
            a_d = a[:, dcols]
            k_d = k[:, cols] * (1.0 + (a_d - 1.0) * ka_ref[:, cols])
            bk_ref[d, 0, :, cols] = (kkn * a_d).astype(bk_ref.dtype)
            bk_ref[d, 1, :, cols] = k_d.astype(bk_ref.dtype)
            ksum = ksum + k_d
        bsum = _head_sum(r[:, cols] * ksum * rk_ref[:, cols], bd)
        bonus_ref[:, cols] = (bsum * v[:, cols]).astype(bonus_ref.dtype)


def _rwkv_prep(feat, seq, width, mup, mun, w0c, w2c, a0c, a2c, g2p, k_k, k_a, r_k, tm=128,
               halo=16):
    t, fw = feat.shape
    nbh = t // halo
    row = lambda i: (0, 0)
    return pl.pallas_call(
        functools.partial(_prep_kernel, seq=seq, width=width),
        grid=(t // tm,),
        in_specs=[
            pl.BlockSpec((tm, fw), lambda i: (i, 0)),
            pl.BlockSpec((halo, fw), lambda i: (jnp.maximum(i * (tm // halo) - 1, 0), 0)),
            pl.BlockSpec((halo, fw), lambda i: (jnp.minimum((i + 1) * (tm // halo), nbh - 1), 0)),
            pl.BlockSpec((1, fw), row),
            pl.BlockSpec((1, fw), row),
            pl.BlockSpec(w0c.shape, row),
            pl.BlockSpec(w2c.shape, lambda i: (0, 0, 0)),
            pl.BlockSpec(a0c.shape, row),
            pl.BlockSpec(a2c.shape, lambda i: (0, 0, 0)),
            pl.BlockSpec(g2p.shape, lambda i: (0, 0, 0)),
            pl.BlockSpec((1, width), row),
            pl.BlockSpec((1, width), row),
            pl.BlockSpec((1, width), row),
        ],
        out_specs=[
            pl.BlockSpec((3, tm, width), lambda i: (0, i, 0)),
            pl.BlockSpec((2, tm, width), lambda i: (0, i, 0)),
            pl.BlockSpec((2, 2, tm, width), lambda i: (0, 0, i, 0)),
            pl.BlockSpec((tm, width), lambda i: (i, 0)),
            pl.BlockSpec((tm, width), lambda i: (i, 0)),
        ],
        out_shape=[
            jax.ShapeDtypeStruct((3, t, width), BF16),
            jax.ShapeDtypeStruct((2, t, width), F32),
            jax.ShapeDtypeStruct((2, 2, t, width), BF16),
            jax.ShapeDtypeStruct((t, width), BF16),
            jax.ShapeDtypeStruct((t, width), BF16),
        ],
        compiler_params=_cparams("parallel"),
        name="rwkv_prep",
    )(feat, feat, feat, mup, mun, w0c, w2c, a0c, a2c, g2p, k_k, k_a, r_k)


def _hat(x, first_half):
    return jnp.concatenate([jnp.where(first_half, x, 0.0), jnp.where(first_half, 0.0, x)],
                           axis=0)


def _scan_kernel(shf_ref, shb_ref, lwf_ref, lwb_ref, bkf_ref, bkb_ref, of_ref, ob_ref, s_ref,
                 *, group):
    c = SCAN_CHUNK
    batch, width = shf_ref.shape[1], shf_ref.shape[3]
    npair = width // LANES
    assert 2 * c == LANES and c == HEAD_DIM

    @pl.when(pl.program_id(0) == 0)
    def _():
        s_ref[...] = jnp.zeros_like(s_ref)

    bf = lambda t: t.astype(BF16)
    f32 = lambda t: t.astype(F32)
    dg = lambda p, q, dims: lax.dot_general(bf(p), bf(q), (dims, ((), ())),
                                            preferred_element_type=F32)
    dot = lambda p, q: dg(p, q, ((1,), (0,)))
    dot_nt = lambda p, q: dg(p, q, ((1,), (1,)))
    dot_tn = lambda p, q: dg(p, q, ((0,), (0,)))

    wt = lax.broadcasted_iota(jnp.int32, (c, LANES), 0)
    ws = lax.broadcasted_iota(jnp.int32, (c, LANES), 1) % c
    masks = ((wt > ws, wt >= ws), (wt < ws, wt <= ws))
    eye = jnp.where(wt == ws, 1.0, 0.0).astype(F32)
    half = lax.broadcasted_iota(jnp.int32, (c, LANES), 1) < c
    same_head = (lax.broadcasted_iota(jnp.int32, (LANES, LANES), 0) // HEAD_DIM
                 == lax.broadcasted_iota(jnp.int32, (LANES, LANES), 1) // HEAD_DIM)
    hat = lambda t: _hat(t, half)
    refs = ((shf_ref, lwf_ref, bkf_ref, of_ref), (shb_ref, lwb_ref, bkb_ref, ob_ref))

    units = [(d, b, p) for d in range(2) for b in range(batch) for p in range(npair)]
    for g0 in range(0, len(units), group):
        grp = units[g0:g0 + group]
        st = {}
        for (d, b, p) in grp:
            sh_ref, lw_ref, bk_ref, _ = refs[d]
            strict, incl = masks[d]
            cols = slice(p * LANES, (p + 1) * LANES)
            r = f32(sh_ref[0, b, :, cols])
            v = f32(sh_ref[1, b, :, cols])
            kk = f32(sh_ref[2, b, :, cols])
            lw = lw_ref[0, b, :, cols]
            bb = f32(bk_ref[0, 0, b, :, cols])
            kd = f32(bk_ref[0, 1, b, :, cols])
            pre = lw
            s = 1
            while s < c:
                pre = pre + jnp.where(wt >= s, pltpu.roll(pre, s, 0), 0.0)
                s *= 2
            gtot = pre[c - 1:c, :]
            gi = pre if d == 0 else gtot - pre + lw
            en = jnp.exp(-gi)
            ec = jnp.exp(gtot - gi)
            lhs = bf(jnp.concatenate([-kk * jnp.exp(gi - lw), r * jnp.exp(gi)], axis=0))
            rhs = jnp.concatenate([hat(bb * en), hat(kd * en)], axis=0)
            aa = dot_nt(lhs, rhs)
            a_ab = jnp.where(strict, aa[:c, :LANES], 0.0)
            a_ak = jnp.where(strict, aa[:c, LANES:], 0.0)
            ark = bf(jnp.concatenate([jnp.where(incl, aa[c:, :LANES], 0.0),
                                      jnp.where(incl, aa[c:, LANES:], 0.0)], axis=1))
            vh = bf(hat(v))
            sidx = (d * batch + b) * npair + p
            s0 = s_ref[sidx]
            ls = dot_nt(lhs, s0)
            st[(d, b, p)] = dict(
                sidx=sidx, cols=cols, v=v, s0=s0, ark=ark, vh=vh,
                bk=bf(jnp.concatenate([bb * ec, kd * ec], axis=0)),
                eg=jnp.exp(gtot), pw=a_ab, inv=eye + a_ab,
                rhs_u=ls[:c] + dot(a_ak, vh), rs=ls[c:])

        for u in grp:
            st[u]["pw"] = dot(st[u]["pw"], hat(st[u]["pw"]))
        n = 4
        while n < c:
            for u in grp:
                e = st[u]
                both = dot(jnp.concatenate([e["pw"], e["inv"]], axis=0), hat(e["pw"]))
                e["pw"] = both[:c]
                e["inv"] = e["inv"] + both[c:]
            n *= 2
        for u in grp:
            st[u]["inv"] = st[u]["inv"] + dot(st[u]["inv"], hat(st[u]["pw"]))

        for u in grp:
            st[u]["u"] = dot(st[u]["inv"], hat(st[u]["rhs_u"]))
        for (d, b, p) in grp:
            e = st[(d, b, p)]
            o_ref = refs[d][3]
            uvh = jnp.concatenate([bf(hat(e["u"])), e["vh"]], axis=0)
            o_ref[b, :, e["cols"]] = (e["rs"] + dot(e["ark"], uvh)).astype(o_ref.dtype)
        for u in grp:
            e = st[u]
            uv = jnp.concatenate([e["u"], e["v"]], axis=0)
            s_ref[e["sidx"]] = e["s0"] * e["eg"] + jnp.where(same_head, dot_tn(uv, e["bk"]), 0.0)


def _rwkv_scan(sh, lw, bk, batch, seq, group=16):
    _, t, width = sh.shape
    c = SCAN_CHUNK
    nc = seq // c
    sh = sh.reshape(3, batch, seq, width)
    lw = lw.reshape(2, batch, seq, width)
    bk = bk.reshape(2, 2, batch, seq, width)
    sh_spec = lambda f: pl.BlockSpec((3, batch, c, width), lambda i: (0, 0, f(i), 0))
    lw_spec = lambda d, f: pl.BlockSpec((1, batch, c, width), lambda i: (d, 0, f(i), 0))
    bk_spec = lambda d, f: pl.BlockSpec((1, 2, batch, c, width), lambda i: (d, 0, 0, f(i), 0))
    o_spec = lambda f: pl.BlockSpec((batch, c, width), lambda i: (0, f(i), 0))
    fwd = lambda i: i
    bwd = lambda i: nc - 1 - i
    o_f, o_b = pl.pallas_call(
        functools.partial(_scan_kernel, group=group),
        grid=(nc,),
        in_specs=[sh_spec(fwd), sh_spec(bwd), lw_spec(0, fwd), lw_spec(1, bwd),
                  bk_spec(0, fwd), bk_spec(1, bwd)],
        out_specs=[o_spec(fwd), o_spec(bwd)],
        out_shape=[jax.ShapeDtypeStruct((batch, seq, width), BF16)] * 2,
        scratch_shapes=[pltpu.VMEM((2 * batch * (width // LANES), LANES, LANES), F32)],
        compiler_params=_cparams("arbitrary"),
        name="rwkv_scan",
    )(sh, sh, lw, lw, bk, bk)
    return o_f.reshape(t, width), o_b.reshape(t, width)


def _post_kernel(of_ref, ob_ref, bonus_ref, gg_ref, gng_ref, gnb_ref, y_ref):
    width = y_ref.shape[1]
    avg = _head_block_ones(1.0 / HEAD_DIM)
    for j in range(width // LANES):
        cols = slice(j * LANES, (j + 1) * LANES)
        o = of_ref[:, cols].astype(F32) + ob_ref[:, cols].astype(F32)
        oc = o - _head_sum(o, avg)
        var = _head_sum(oc * oc, avg)
        y = oc * lax.rsqrt(var + GN_EPS) * gng_ref[:, cols] + gnb_ref[:, cols]
        y_ref[:, cols] = ((y + bonus_ref[:, cols].astype(F32))
                          * gg_ref[:, cols].astype(F32)).astype(y_ref.dtype)


def _rwkv_post(o_f, o_b, bonus, gg, gn_g, gn_b, tm=512, out_dtype=BF16):
    t, width = o_f.shape
    return pl.pallas_call(
        _post_kernel,
        grid=(t // tm,),
        in_specs=[
            pl.BlockSpec((tm, width), lambda i: (i, 0)),
            pl.BlockSpec((tm, width), lambda i: (i, 0)),
            pl.BlockSpec((tm, width), lambda i: (i, 0)),
            pl.BlockSpec((tm, width), lambda i: (i, 0)),
            pl.BlockSpec((1, width), lambda i: (0, 0)),
            pl.BlockSpec((1, width), lambda i: (0, 0)),
        ],
        out_specs=pl.BlockSpec((tm, width), lambda i: (i, 0)),
        out_shape=jax.ShapeDtypeStruct((t, width), out_dtype),
        compiler_params=_cparams("parallel"),
        name="rwkv_post",
    )(o_f, o_b, bonus, gg, gn_g, gn_b)


def _merge_kernel(ya_ref, yb_ref, wa_ref, wb_ref, ga_ref, gb_ref, o_ref):
    pa = jnp.dot(ya_ref[...], wa_ref[...], preferred_element_type=F32)
    pb = jnp.dot(yb_ref[...], wb_ref[...], preferred_element_type=F32)
    o_ref[...] = (ga_ref[...].astype(F32) * pa + gb_ref[...].astype(F32) * pb).astype(o_ref.dtype)


def _merge(ya, yb, wa, wb, gates, tm=1024, tn=512):
    t, k = ya.shape
    n = wa.shape[1]
    nb = n // tn
    return pl.pallas_call(
        _merge_kernel,
        grid=(t // tm, nb),
        in_specs=[
            pl.BlockSpec((tm, k), lambda i, j: (i, 0)),
            pl.BlockSpec((tm, k), lambda i, j: (i, 0)),
            pl.BlockSpec((k, tn), lambda i, j: (0, j)),
            pl.BlockSpec((k, tn), lambda i, j: (0, j)),
            pl.BlockSpec((tm, tn), lambda i, j: (i, j)),
            pl.BlockSpec((tm, tn), lambda i, j: (i, j + nb)),
        ],
        out_specs=pl.BlockSpec((tm, tn), lambda i, j: (i, j)),
        out_shape=jax.ShapeDtypeStruct((t, n), BF16),
        compiler_params=_cparams("parallel", "arbitrary"),
        name="merge_proj",
    )(ya, yb, wa, wb, gates, gates)


def _outproj_kernel(m_ref, w_ref, x_ref, g_ref, x1_ref, h2_ref):
    x1 = x_ref[...] + jnp.dot(m_ref[...], w_ref[...], preferred_element_type=F32)
    x1_ref[...] = x1
    ms = jnp.mean(x1 * x1, axis=-1, keepdims=True)
    h2_ref[...] = (x1 * lax.rsqrt(ms + NORM_EPS) * g_ref[...]).astype(h2_ref.dtype)


def _outproj(m, w, x, g, tm=256):
    t, d = x.shape
    return pl.pallas_call(
        _outproj_kernel,
        grid=(t // tm,),
        in_specs=[
            pl.BlockSpec((tm, d), lambda i: (i, 0)),
            pl.BlockSpec((d, d), lambda i: (0, 0)),
            pl.BlockSpec((tm, d), lambda i: (i, 0)),
            pl.BlockSpec((1, d), lambda i: (0, 0)),
        ],
        out_specs=[
            pl.BlockSpec((tm, d), lambda i: (i, 0)),
            pl.BlockSpec((tm, d), lambda i: (i, 0)),
        ],
        out_shape=[jax.ShapeDtypeStruct((t, d), F32), jax.ShapeDtypeStruct((t, d), BF16)],
        compiler_params=_cparams("parallel"),
        name="out_proj",
    )(m, w, x, g)


def _ffn1_kernel(h_ref, hb_ref, ha_ref, wg_ref, wu_ref, cw_ref, cb_ref, a_ref, *, seq):
    tm = h_ref.shape[0]
    halo = hb_ref.shape[0]
    i = pl.program_id(0)
    tiles_per_seq = seq // tm
    first = (i % tiles_per_seq) == 0
    last = (i % tiles_per_seq) == tiles_per_seq - 1

    h = h_ref[...]
    h_ext = jnp.concatenate([hb_ref[...], h, ha_ref[...]], axis=0)
    g_ext = jnp.dot(h_ext, wg_ref[...], preferred_element_type=F32)
    up = jnp.dot(h, wu_ref[...], preferred_element_type=F32)
    rows = tm + 2 * halo
    row = lax.broadcasted_iota(jnp.int32, (tm, 1), 0)
    g = g_ext[halo:halo + tm]
    prev = pltpu.roll(g_ext, 1, 0)[halo:halo + tm]
    prev = jnp.where(jnp.logical_and(row == 0, first), 0.0, prev)
    nxt = pltpu.roll(g_ext, rows - 1, 0)[halo:halo + tm]
    nxt = jnp.where(jnp.logical_and(row == tm - 1, last), 0.0, nxt)
    gt = prev * cw_ref[0:1, :] + g * cw_ref[1:2, :] + nxt * cw_ref[2:3, :] + cb_ref[...]
    a_ref[...] = (jax.nn.silu(gt) * up).astype(a_ref.dtype)


def _ffn1(h, wg, wu, cw, cb, seq, tm=1024, tn=512, halo=16):
    t, d = h.shape
    f = wg.shape[1]
    nbh = t // halo
    return pl.pallas_call(
        functools.partial(_ffn1_kernel, seq=seq),
        grid=(t // tm, f // tn),
        in_specs=[
            pl.BlockSpec((tm, d), lambda i, j: (i, 0)),
            pl.BlockSpec((halo, d), lambda i, j: (jnp.maximum(i * (tm // halo) - 1, 0), 0)),
            pl.BlockSpec((halo, d), lambda i, j: (jnp.minimum((i + 1) * (tm // halo), nbh - 1), 0)),
            pl.BlockSpec((d, tn), lambda i, j: (0, j)),
            pl.BlockSpec((d, tn), lambda i, j: (0, j)),
            pl.BlockSpec((cw.shape[0], tn), lambda i, j: (0, j)),
            pl.BlockSpec((1, tn), lambda i, j: (0, j)),
        ],
        out_specs=pl.BlockSpec((tm, tn), lambda i, j: (i, j)),
        out_shape=jax.ShapeDtypeStruct((t, f), BF16),
        compiler_params=_cparams("parallel", "arbitrary"),
        name="ffn_gate_up",
    )(h, h, h, wg, wu, cw, cb)


def _ffn2_kernel(a_ref, wd_ref, x1_ref, nf_ref, o_ref):
    kidx = pl.program_id(1)

    @pl.when(kidx == 0)
    def _():
        o_ref[...] = x1_ref[...]

    o_ref[...] += jnp.dot(a_ref[...], wd_ref[...], preferred_element_type=F32)

    @pl.when(kidx == pl.num_programs(1) - 1)
    def _():
        xf = o_ref[...]
        ms = jnp.mean(xf * xf, axis=-1, keepdims=True)
        o_ref[...] = xf * lax.rsqrt(ms + NORM_EPS) * nf_ref[...]


def _ffn2(act, wd, x1, nf, tm=512, tk=1408):
    t, f = act.shape
    d = x1.shape[1]
    return pl.pallas_call(
        _ffn2_kernel,
        grid=(t // tm, f // tk),
        in_specs=[
            pl.BlockSpec((tm, tk), lambda i, k: (i, k)),
            pl.BlockSpec((tk, d), lambda i, k: (k, 0)),
            pl.BlockSpec((tm, d), lambda i, k: (i, 0)),
            pl.BlockSpec((1, d), lambda i, k: (0, 0)),
        ],
        out_specs=pl.BlockSpec((tm, d), lambda i, k: (i, 0)),
        out_shape=jax.ShapeDtypeStruct((t, d), F32),
        compiler_params=_cparams("parallel", "arbitrary"),
        name="ffn_down",
    )(act, wd, x1, nf)


def _pad_cols(a, n):
    return jnp.pad(a, ((0, 0), (0, n - a.shape[1])))


def _pad_rows(a, n):
    return jnp.pad(a, ((0, n - a.shape[0]), (0, 0)))


def _layer(x, batch, seq, norm1_g, w_in, sgu_ln_g, sgu_ln_b, sgu_w, sgu_b,
           mu_prev, mu_next, w0_f, w2_f, a0_f, a2_f, w0_b, w2_b, a0_b, a2_b, k_k, k_a, r_k,
           g2, gn_g, gn_b, w_proj_a, w_proj_b, w_out, norm2_g, ffn_w_gate, ffn_w_up,
           ffn_conv_w, ffn_conv_b, ffn_w_down, norm_out_g):
    d = x.shape[1]
    sgu_width = sgu_ln_g.shape[0]
    width = k_k.shape[0]
    dl = w2_f.shape[0]
    al = a2_f.shape[0]
    feat_w = mu_prev.shape[0]
    feat_pad = -(-feat_w // (4 * LANES)) * (4 * LANES)
    assert dl == HEAD_DIM and al == HEAD_DIM and 2 * dl == LANES
    row = lambda a: a.reshape(1, -1)

    o1 = 2 * sgu_width
    o2 = o1 + feat_w
    w_gates = w_in[:, o2:].astype(BF16)
    hi_lo = lambda w: jnp.stack(_split_bf16(w))
    zw = jnp.zeros_like(w2_f)
    w2c = hi_lo(jnp.concatenate([jnp.concatenate([w2_f, zw], 1),
                                 jnp.concatenate([zw, w2_b], 1)], 0))
    za = jnp.zeros_like(a2_f)
    a2c = hi_lo(jnp.concatenate([jnp.concatenate([a2_f, za], 1),
                                 jnp.concatenate([za, a2_b], 1)], 0))
    g2p = hi_lo(_pad_rows(g2, feat_pad - 3 * width - 2 * LANES))
    w0c = row(jnp.concatenate([w0_f, w0_b]))
    a0c = row(jnp.concatenate([a0_f, a0_b]))
    mup = _pad_cols(row(mu_prev), feat_pad)
    mun = _pad_cols(row(mu_next), feat_pad)
    bs_full = jnp.repeat(sgu_b.T, sgu_width // sgu_b.shape[0], axis=1)

    g1 = row(norm1_g)
    uv, feat, gates = _in_proj(x, g1, w_in, w_gates, o1, feat_pad)

    y_a = _sgu(uv, row(sgu_ln_g), row(sgu_ln_b), sgu_w.astype(BF16), bs_full)

    sh, lw, bk, bonus, gg = _rwkv_prep(feat, seq, width, mup, mun, w0c, w2c, a0c, a2c, g2p,
                                       row(k_k), row(k_a), row(r_k))
    o_f, o_b = _rwkv_scan(sh, lw, bk, batch, seq)
    y_b = _rwkv_post(o_f, o_b, bonus, gg, row(gn_g), row(gn_b))

    merged = _merge(y_a, y_b, w_proj_a.astype(BF16), w_proj_b.astype(BF16), gates)
    x1, h2 = _outproj(merged, w_out.astype(BF16), x, row(norm2_g))

    act = _ffn1(h2, ffn_w_gate.astype(BF16), ffn_w_up.astype(BF16), ffn_conv_w,
                row(ffn_conv_b), seq)
    return _ffn2(act, ffn_w_down.astype(BF16), x1, norm_out_g)


def kernel(x, norm1_g, w_in, sgu_ln_g, sgu_ln_b, sgu_w, sgu_b, rwkv_mu_prev, rwkv_mu_next, rwkv_w0_f, rwkv_w2_f, rwkv_a0_f, rwkv_a2_f, rwkv_w0_b, rwkv_w2_b, rwkv_a0_b, rwkv_a2_b, rwkv_k_k, rwkv_k_a, rwkv_r_k, rwkv_g2, rwkv_gn_g, rwkv_gn_b, w_proj_a, w_proj_b, w_out, norm2_g, ffn_w_gate, ffn_w_up, ffn_conv_w, ffn_conv_b, ffn_w_down, norm_f_g):
    batch, seq, d = x.shape
    depth = norm1_g.shape[0]
    assert depth == 1, "the fused final RMSNorm assumes a single layer"
    xf = x.reshape(batch * seq, d)
    per_layer = (norm1_g, w_in, sgu_ln_g, sgu_ln_b, sgu_w, sgu_b, rwkv_mu_prev, rwkv_mu_next,
                 rwkv_w0_f, rwkv_w2_f, rwkv_a0_f, rwkv_a2_f, rwkv_w0_b, rwkv_w2_b, rwkv_a0_b,
                 rwkv_a2_b, rwkv_k_k, rwkv_k_a, rwkv_r_k, rwkv_g2, rwkv_gn_g, rwkv_gn_b,
                 w_proj_a, w_proj_b, w_out, norm2_g, ffn_w_gate, ffn_w_up, ffn_conv_w,
                 ffn_conv_b, ffn_w_down)
    out = _layer(xf, batch, seq, *(p[0] for p in per_layer), norm_f_g.reshape(1, d))
    return out.reshape(batch, seq, d)
```

```python
import functools

import jax
import jax.numpy as jnp
from jax import lax
from jax.experimental import pallas as pl
from jax.experimental.pallas import tpu as pltpu

F32 = jnp.float32
BF16 = jnp.bfloat16

NORM_EPS = 1e-6
LN_EPS = 1e-5
GN_EPS = 64e-5

HEAD_DIM = 64
LANES = 128
SGU_CHUNK = 128
SCAN_CHUNK = 64
VMEM_LIMIT = 48 * 1024 * 1024


def _cparams(*sem):
    return pltpu.CompilerParams(dimension_semantics=sem, vmem_limit_bytes=VMEM_LIMIT)


def _rmsnorm_kernel(x_ref, g_ref, h_ref):
    x = x_ref[...]
    ms = jnp.mean(x * x, axis=-1, keepdims=True)
    h_ref[...] = (x * lax.rsqrt(ms + NORM_EPS) * g_ref[...]).astype(h_ref.dtype)


def _rmsnorm(x, g, tm=512):
    t, d = x.shape
    return pl.pallas_call(
        _rmsnorm_kernel,
        grid=(t // tm,),
        in_specs=[pl.BlockSpec((tm, d), lambda i: (i, 0)), pl.BlockSpec((1, d), lambda i: (0, 0))],
        out_specs=pl.BlockSpec((tm, d), lambda i: (i, 0)),
        out_shape=jax.ShapeDtypeStruct((t, d), BF16),
        compiler_params=_cparams("parallel"),
        name="rmsnorm",
    )(x, g)


def _in_proj_kernel(h_ref, w_ref, uv_ref, ft_ref, *, n_uv):
    j = pl.program_id(1)
    acc = lax.dot_general(h_ref[...], w_ref[...].astype(BF16), (((1,), (1,)), ((), ())),
                          preferred_element_type=F32)

    @pl.when(j < n_uv)
    def _():
        uv_ref[...] = acc.astype(uv_ref.dtype)

    @pl.when(j >= n_uv)
    def _():
        ft_ref[...] = acc.astype(ft_ref.dtype)


def _in_proj(h, w_t, w_uv, w_ft, tm=2048, tn=512):
    t, d = h.shape
    n_uv, n_ft = w_uv // tn, w_ft // tn
    return pl.pallas_call(
        functools.partial(_in_proj_kernel, n_uv=n_uv),
        grid=(t // tm, n_uv + n_ft),
        in_specs=[
            pl.BlockSpec((tm, d), lambda i, j: (i, 0), pipeline_mode=pl.Buffered(1)),
            pl.BlockSpec((tn, d), lambda i, j: (j, 0)),
        ],
        out_specs=[
            pl.BlockSpec((tm, tn), lambda i, j: (i, jnp.minimum(j, n_uv - 1))),
            pl.BlockSpec((tm, tn), lambda i, j: (i, jnp.maximum(j - n_uv, 0))),
        ],
        out_shape=[
            jax.ShapeDtypeStruct((t, n_uv * tn), BF16),
            jax.ShapeDtypeStruct((t, n_ft * tn), BF16),
        ],
        compiler_params=_cparams("parallel", "arbitrary"),
        name="in_proj",
    )(h, w_t)


def _gelu(x):
    return 0.5 * x * (1.0 + lax.erf(x * (2.0 ** -0.5)))


def _sgu_kernel(uv_ref, lng_ref, lnb_ref, ws_ref, bs_ref, o_ref, *, width, groups):
    tm = uv_ref.shape[0]
    gv = _gelu(uv_ref[:, width:].astype(F32))
    mu = jnp.mean(gv, axis=-1, keepdims=True)
    vc = gv - mu
    var = jnp.mean(vc * vc, axis=-1, keepdims=True)
    vn = (vc * lax.rsqrt(var + LN_EPS) * lng_ref[...] + lnb_ref[...]).astype(BF16)
    gd = width // groups
    for c in range(tm // SGU_CHUNK):
        rows = slice(c * SGU_CHUNK, (c + 1) * SGU_CHUNK)
        for g in range(groups):
            cols = slice(g * gd, (g + 1) * gd)
            mixed = jnp.dot(ws_ref[g], vn[rows, cols], preferred_element_type=F32)
            gu = _gelu(uv_ref[rows, cols].astype(F32))
            o_ref[rows, cols] = (gu * (mixed + bs_ref[:, cols])).astype(o_ref.dtype)


def _sgu(uv, ln_g, ln_b, ws, bs_full, tm=256):
    t, w2 = uv.shape
    width = w2 // 2
    groups = ws.shape[0]
    return pl.pallas_call(
        functools.partial(_sgu_kernel, width=width, groups=groups),
        grid=(t // tm,),
        in_specs=[
            pl.BlockSpec((tm, w2), lambda i: (i, 0)),
            pl.BlockSpec((1, width), lambda i: (0, 0)),
            pl.BlockSpec((1, width), lambda i: (0, 0)),
            pl.BlockSpec(ws.shape, lambda i: (0, 0, 0)),
            pl.BlockSpec(bs_full.shape, lambda i: (0, 0)),
        ],
        out_specs=pl.BlockSpec((tm, width), lambda i: (i, 0)),
        out_shape=jax.ShapeDtypeStruct((t, width), BF16),
        compiler_params=_cparams("parallel"),
        name="sgu_mixer",
    )(uv, ln_g, ln_b, ws, bs_full)


def _head_block_ones(scale):
    r = lax.broadcasted_iota(jnp.int32, (LANES, LANES), 0) // HEAD_DIM
    c = lax.broadcasted_iota(jnp.int32, (LANES, LANES), 1) // HEAD_DIM
    return jnp.where(r == c, scale, 0.0).astype(BF16)


def _split_bf16(x):
    hi = x.astype(BF16)
    return hi, (x - hi.astype(F32)).astype(BF16)


def _head_sum(x, bd):
    return jnp.dot(x.astype(BF16), bd, preferred_element_type=F32)


def _stack_split3(w):
    hi, lo = _split_bf16(w)
    return jnp.concatenate([hi, lo, hi], axis=0)


def _dot_split3(x, w_stack):
    hi, lo = _split_bf16(x)
    return jnp.dot(jnp.concatenate([hi, hi, lo], axis=1), w_stack, preferred_element_type=F32)


def _prep_kernel(f_ref, fb_ref, fa_ref, mup_ref, mun_ref, w0_ref, w2_ref, a0_ref, a2_ref,
                 g2_ref, kk_ref, ka_ref, rk_ref, h_ref, wg_ref,
                 sh_ref, lw_ref, bk_ref, bonus_ref, gg_ref, gt_ref, *, seq, width):
    gt_ref[...] = jax.nn.sigmoid(
        jnp.dot(h_ref[...], wg_ref[...], preferred_element_type=F32)).astype(gt_ref.dtype)

    tm = f_ref.shape[0]
    i = 2 * pl.program_id(1) + pl.program_id(0)
    tiles_per_seq = seq // tm
    first = (i % tiles_per_seq) == 0
    last = (i % tiles_per_seq) == tiles_per_seq - 1
    halo = fb_ref.shape[0]
    x_ext = jnp.concatenate([fb_ref[...], f_ref[...], fa_ref[...]], axis=0)
    ri = lax.broadcasted_iota(jnp.int32, (2 * tm, tm + 2 * halo), 0)
    ci = lax.broadcasted_iota(jnp.int32, (2 * tm, tm + 2 * halo), 1)
    src = jnp.where(ri < tm, ri + (halo - 1), ri + (halo + 1 - tm))
    outside = jnp.logical_or(jnp.logical_and(ri == 0, first),
                             jnp.logical_and(ri == 2 * tm - 1, last))
    pick = jnp.where(jnp.logical_and(ci == src, jnp.logical_not(outside)), 1.0, 0.0).astype(BF16)
    shifted = jnp.dot(pick, x_ext, preferred_element_type=F32)
    x = f_ref[...].astype(F32)
    xs = x + mup_ref[...] * (shifted[:tm] - x) + mun_ref[...] * (shifted[tm:] - x)

    r = xs[:, 0:width]
    k = xs[:, width:2 * width]
    v = xs[:, 2 * width:3 * width]
    o = 3 * width
    lo_w = xs[:, o:o + LANES]
    lo_a = xs[:, o + LANES:o + 2 * LANES]
    lo_g = xs[:, o + 2 * LANES:]

    wpre = w0_ref[...] + _dot_split3(jnp.tanh(lo_w), w2_ref[...])
    lw = -(2.718281828459045 ** -0.5) * jax.nn.sigmoid(wpre)
    a = jax.nn.sigmoid(a0_ref[...] + _dot_split3(lo_a, a2_ref[...]))
    gg_ref[...] = _dot_split3(jax.nn.sigmoid(lo_g), g2_ref[...]).astype(gg_ref.dtype)
    lw_ref[0] = lw[:, :width]
    lw_ref[1] = lw[:, width:]

    bd = _head_block_ones(1.0)
    kk = k * kk_ref[...]
    sh_ref[0] = r.astype(sh_ref.dtype)
    sh_ref[1] = v.astype(sh_ref.dtype)
    for j in range(width // LANES):
        cols = slice(j * LANES, (j + 1) * LANES)
        kkj = kk[:, cols]
        ss = _head_sum(kkj * kkj, bd)
        kkn = kkj * lax.rsqrt(jnp.maximum(ss, 1e-24))
        sh_ref[2, :, cols] = kkn.astype(sh_ref.dtype)
        ksum = jnp.zeros_like(kkj)
        for d in range(2):
            dcols = slice(d * width + j * LANES, d * width + (j + 1) * LANES)
            a_d = a[:, dcols]
            k_d = k[:, cols] * (1.0 + (a_d - 1.0) * ka_ref[:, cols])
            bk_ref[d, 0, :, cols] = (kkn * a_d).astype(bk_ref.dtype)
            bk_ref[d, 1, :, cols] = k_d.astype(bk_ref.dtype)
            ksum = ksum + k_d
        bsum = _head_sum(r[:, cols] * ksum * rk_ref[:, cols], bd)
        bonus_ref[:, cols] = (bsum * v[:, cols]).astype(bonus_ref.dtype)


def _rwkv_prep(feat, h, wg, seq, width, mup, mun, w0c, w2c, a0c, a2c, g2p, k_k, k_a, r_k,
               tm=128, halo=16):
    t, fw = feat.shape
    d, ng = wg.shape
    nbh = t // halo
    r = lambda j, i: 2 * i + j
    row = lambda j, i: (0, 0)
    resident = lambda a: pl.BlockSpec(a.shape, row, pipeline_mode=pl.Buffered(1))
    rows = lambda w: pl.BlockSpec((tm, w), lambda j, i: (r(j, i), 0))
    return pl.pallas_call(
        functools.partial(_prep_kernel, seq=seq, width=width),
        grid=(2, t // (2 * tm)),
        in_specs=[
            rows(fw),
            pl.BlockSpec((halo, fw),
                         lambda j, i: (jnp.maximum(r(j, i) * (tm // halo) - 1, 0), 0)),
            pl.BlockSpec((halo, fw),
                         lambda j, i: (jnp.minimum((r(j, i) + 1) * (tm // halo), nbh - 1), 0)),
            pl.BlockSpec((1, fw), row),
            pl.BlockSpec((1, fw), row),
            pl.BlockSpec(w0c.shape, row),
            resident(w2c),
            pl.BlockSpec(a0c.shape, row),
            resident(a2c),
            resident(g2p),
            pl.BlockSpec((1, width), row),
            pl.BlockSpec((1, width), row),
            pl.BlockSpec((1, width), row),
            pl.BlockSpec((2 * tm, d), lambda j, i: (i, 0)),
            pl.BlockSpec((d, ng // 2), lambda j, i: (0, j), pipeline_mode=pl.Buffered(1)),
        ],
        out_specs=[
            pl.BlockSpec((3, tm, width), lambda j, i: (0, r(j, i), 0)),
            pl.BlockSpec((2, tm, width), lambda j, i: (0, r(j, i), 0)),
            pl.BlockSpec((2, 2, tm, width), lambda j, i: (0, 0, r(j, i), 0)),
            rows(width),
            rows(width),
            pl.BlockSpec((2 * tm, ng // 2), lambda j, i: (i, j)),
        ],
        out_shape=[
            jax.ShapeDtypeStruct((3, t, width), BF16),
            jax.ShapeDtypeStruct((2, t, width), F32),
            jax.ShapeDtypeStruct((2, 2, t, width), BF16),
            jax.ShapeDtypeStruct((t, width), BF16),
            jax.ShapeDtypeStruct((t, width), BF16),
            jax.ShapeDtypeStruct((t, ng), BF16),
        ],
        compiler_params=_cparams("arbitrary", "arbitrary"),
        name="rwkv_prep",
    )(feat, feat, feat, mup, mun, w0c, w2c, a0c, a2c, g2p, k_k, k_a, r_k, h, wg)


def _hat(x, first_half):
    return jnp.concatenate([jnp.where(first_half, x, 0.0), jnp.where(first_half, 0.0, x)],
                           axis=0)


def _scan_kernel(shf_ref, shb_ref, lwf_ref, lwb_ref, bkf_ref, bkb_ref, of_ref, ob_ref, s_ref,
                 *, group):
    c = SCAN_CHUNK
    batch, width = shf_ref.shape[1], shf_ref.shape[3]
    npair = width // LANES
    assert 2 * c == LANES and c == HEAD_DIM

    @pl.when(pl.program_id(0) == 0)
    def _():
        s_ref[...] = jnp.zeros_like(s_ref)

    bf = lambda t: t.astype(BF16)
    f32 = lambda t: t.astype(F32)
    dg = lambda p, q, dims: lax.dot_general(bf(p), bf(q), (dims, ((), ())),
                                            preferred_element_type=F32)
    dot = lambda p, q: dg(p, q, ((1,), (0,)))
    dot_nt = lambda p, q: dg(p, q, ((1,), (1,)))
    dot_tn = lambda p, q: dg(p, q, ((0,), (0,)))

    wt = lax.broadcasted_iota(jnp.int32, (c, LANES), 0)
    ws = lax.broadcasted_iota(jnp.int32, (c, LANES), 1) % c
    masks = ((wt > ws, wt >= ws), (wt < ws, wt <= ws))
    eye = jnp.where(wt == ws, 1.0, 0.0).astype(F32)
    half = lax.broadcasted_iota(jnp.int32, (c, LANES), 1) < c
    same_head = (lax.broadcasted_iota(jnp.int32, (LANES, LANES), 0) // HEAD_DIM
                 == lax.broadcasted_iota(jnp.int32, (LANES, LANES), 1) // HEAD_DIM)
    hat = lambda t: _hat(t, half)
    refs = ((shf_ref, lwf_ref, bkf_ref, of_ref), (shb_ref, lwb_ref, bkb_ref, ob_ref))

    units = [(d, b, p) for d in range(2) for b in range(batch) for p in range(npair)]
    for g0 in range(0, len(units), group):
        grp = units[g0:g0 + group]
        st = {}
        for (d, b, p) in grp:
            sh_ref, lw_ref, bk_ref, _ = refs[d]
            strict, incl = masks[d]
            cols = slice(p * LANES, (p + 1) * LANES)
            r = f32(sh_ref[0, b, :, cols])
            v = f32(sh_ref[1, b, :, cols])
            kk = f32(sh_ref[2, b, :, cols])
            lw = lw_ref[0, b, :, cols]
            bb = f32(bk_ref[0, 0, b, :, cols])
            kd = f32(bk_ref[0, 1, b, :, cols])
            pre = lw
            s = 1
            while s < c:
                pre = pre + jnp.where(wt >= s, pltpu.roll(pre, s, 0), 0.0)
                s *= 2
            gtot = pre[c - 1:c, :]
            gi = pre if d == 0 else gtot - pre + lw
            en = jnp.exp(-gi)
            ec = jnp.exp(gtot - gi)
            lhs = bf(jnp.concatenate([-kk * jnp.exp(gi - lw), r * jnp.exp(gi)], axis=0))
            rhs = jnp.concatenate([hat(bb * en), hat(kd * en)], axis=0)
            aa = dot_nt(lhs, rhs)
            a_ab = jnp.where(strict, aa[:c, :LANES], 0.0)
            a_ak = jnp.where(strict, aa[:c, LANES:], 0.0)
            ark = bf(jnp.concatenate([jnp.where(incl, aa[c:, :LANES], 0.0),
                                      jnp.where(incl, aa[c:, LANES:], 0.0)], axis=1))
            vh = bf(hat(v))
            sidx = (d * batch + b) * npair + p
            s0 = s_ref[sidx]
            ls = dot_nt(lhs, s0)
            st[(d, b, p)] = dict(
                sidx=sidx, cols=cols, v=v, s0=s0, ark=ark, vh=vh,
                bk=bf(jnp.concatenate([bb * ec, kd * ec], axis=0)),
                eg=jnp.exp(gtot), pw=a_ab, inv=eye + a_ab,
                rhs_u=ls[:c] + dot(a_ak, vh), rs=ls[c:])

        for u in grp:
            st[u]["pw"] = dot(st[u]["pw"], hat(st[u]["pw"]))
        n = 4
        while n < c:
            for u in grp:
                e = st[u]
                both = dot(jnp.concatenate([e["pw"], e["inv"]], axis=0), hat(e["pw"]))
                e["pw"] = both[:c]
                e["inv"] = e["inv"] + both[c:]
            n *= 2
        for u in grp:
            st[u]["inv"] = st[u]["inv"] + dot(st[u]["inv"], hat(st[u]["pw"]))

        for u in grp:
            st[u]["u"] = dot(st[u]["inv"], hat(st[u]["rhs_u"]))
        for (d, b, p) in grp:
            e = st[(d, b, p)]
            o_ref = refs[d][3]
            uvh = jnp.concatenate([bf(hat(e["u"])), e["vh"]], axis=0)
            o_ref[b, :, e["cols"]] = (e["rs"] + dot(e["ark"], uvh)).astype(o_ref.dtype)
        for u in grp:
            e = st[u]
            uv = jnp.concatenate([e["u"], e["v"]], axis=0)
            s_ref[e["sidx"]] = e["s0"] * e["eg"] + jnp.where(same_head, dot_tn(uv, e["bk"]), 0.0)


def _rwkv_scan(sh, lw, bk, batch, seq, group=16):
    _, t, width = sh.shape
    c = SCAN_CHUNK
    nc = seq // c
    sh = sh.reshape(3, batch, seq, width)
    lw = lw.reshape(2, batch, seq, width)
    bk = bk.reshape(2, 2, batch, seq, width)
    sh_spec = lambda f: pl.BlockSpec((3, batch, c, width), lambda i: (0, 0, f(i), 0))
    lw_spec = lambda d, f: pl.BlockSpec((1, batch, c, width), lambda i: (d, 0, f(i), 0))
    bk_spec = lambda d, f: pl.BlockSpec((1, 2, batch, c, width), lambda i: (d, 0, 0, f(i), 0))
    o_spec = lambda f: pl.BlockSpec((batch, c, width), lambda i: (0, f(i), 0))
    fwd = lambda i: i
    bwd = lambda i: nc - 1 - i
    o_f, o_b = pl.pallas_call(
        functools.partial(_scan_kernel, group=group),
        grid=(nc,),
        in_specs=[sh_spec(fwd), sh_spec(bwd), lw_spec(0, fwd), lw_spec(1, bwd),
                  bk_spec(0, fwd), bk_spec(1, bwd)],
        out_specs=[o_spec(fwd), o_spec(bwd)],
        out_shape=[jax.ShapeDtypeStruct((batch, seq, width), BF16)] * 2,
        scratch_shapes=[pltpu.VMEM((2 * batch * (width // LANES), LANES, LANES), F32)],
        compiler_params=_cparams("arbitrary"),
        name="rwkv_scan",
    )(sh, sh, lw, lw, bk, bk)
    return o_f.reshape(t, width), o_b.reshape(t, width)


def _rwkv_out_rows(of_ref, ob_ref, bonus_ref, gg_ref, gng_ref, gnb_ref):
    width = of_ref.shape[1]
    avg = _head_block_ones(1.0 / HEAD_DIM)
    tiles = []
    for j in range(width // LANES):
        cols = slice(j * LANES, (j + 1) * LANES)
        o = of_ref[:, cols].astype(F32) + ob_ref[:, cols].astype(F32)
        oc = o - _head_sum(o, avg)
        var = _head_sum(oc * oc, avg)
        y = oc * lax.rsqrt(var + GN_EPS) * gng_ref[:, cols] + gnb_ref[:, cols]
        tiles.append(((y + bonus_ref[:, cols].astype(F32))
                      * gg_ref[:, cols].astype(F32)).astype(BF16))
    return jnp.concatenate(tiles, axis=1)


def _mix_out_kernel(ya_ref, of_ref, ob_ref, bonus_ref, gg_ref, gng_ref, gnb_ref, wa_ref, wb_ref,
                    ga_ref, gb_ref, wo_ref, x_ref, g_ref, x1_ref, h2_ref):
    yb = _rwkv_out_rows(of_ref, ob_ref, bonus_ref, gg_ref, gng_ref, gnb_ref)
    pa = jnp.dot(ya_ref[...], wa_ref[...], preferred_element_type=F32)
    pb = jnp.dot(yb, wb_ref[...], preferred_element_type=F32)
    merged = (ga_ref[...].astype(F32) * pa + gb_ref[...].astype(F32) * pb).astype(BF16)
    x1 = x_ref[...] + jnp.dot(merged, wo_ref[...], preferred_element_type=F32)
    x1_ref[...] = x1
    ms = jnp.mean(x1 * x1, axis=-1, keepdims=True)
    h2_ref[...] = (x1 * lax.rsqrt(ms + NORM_EPS) * g_ref[...]).astype(h2_ref.dtype)


def _mix_out(ya, o_f, o_b, bonus, gg, gn_g, gn_b, wa, wb, gates, wo, x, g, tm=256):
    t, k = ya.shape
    d = x.shape[1]
    resident = lambda shape: pl.BlockSpec(shape, lambda i: (0, 0), pipeline_mode=pl.Buffered(1))
    rows = lambda w: pl.BlockSpec((tm, w), lambda i: (i, 0))
    return pl.pallas_call(
        _mix_out_kernel,
        grid=(t // tm,),
        in_specs=[
            rows(k), rows(k), rows(k), rows(k), rows(k),
            pl.BlockSpec((1, k), lambda i: (0, 0)),
            pl.BlockSpec((1, k), lambda i: (0, 0)),
            resident((k, d)),
            resident((k, d)),
            pl.BlockSpec((tm, d), lambda i: (i, 0)),
            pl.BlockSpec((tm, d), lambda i: (i, 1)),
            resident((d, d)),
            rows(d),
            pl.BlockSpec((1, d), lambda i: (0, 0)),
        ],
        out_specs=[rows(d), rows(d)],
        out_shape=[jax.ShapeDtypeStruct((t, d), F32), jax.ShapeDtypeStruct((t, d), BF16)],
        compiler_params=_cparams("parallel"),
        name="mix_out",
    )(ya, o_f, o_b, bonus, gg, gn_g, gn_b, wa, wb, gates, gates, wo, x, g)


def _ffn1_kernel(h_ref, hb_ref, ha_ref, wg_ref, wu_ref, cw_ref, cb_ref, a_ref, *, seq):
    tm = h_ref.shape[0]
    halo = hb_ref.shape[0]
    i = pl.program_id(0)
    tiles_per_seq = seq // tm
    first = (i % tiles_per_seq) == 0
    last = (i % tiles_per_seq) == tiles_per_seq - 1

    h = h_ref[...]
    h_ext = jnp.concatenate([hb_ref[...], h, ha_ref[...]], axis=0)
    g_ext = jnp.dot(h_ext, wg_ref[...].astype(BF16), preferred_element_type=F32)
    up = jnp.dot(h, wu_ref[...].astype(BF16), preferred_element_type=F32)
    rows = tm + 2 * halo
    row = lax.broadcasted_iota(jnp.int32, (tm, 1), 0)
    g = g_ext[halo:halo + tm]
    prev = pltpu.roll(g_ext, 1, 0)[halo:halo + tm]
    prev = jnp.where(jnp.logical_and(row == 0, first), 0.0, prev)
    nxt = pltpu.roll(g_ext, rows - 1, 0)[halo:halo + tm]
    nxt = jnp.where(jnp.logical_and(row == tm - 1, last), 0.0, nxt)
    gt = prev * cw_ref[0:1, :] + g * cw_ref[1:2, :] + nxt * cw_ref[2:3, :] + cb_ref[...]
    a_ref[...] = (jax.nn.silu(gt) * up).astype(a_ref.dtype)


def _ffn1(h, wg, wu, cw, cb, seq, tm=1024, tn=512, halo=16):
    t, d = h.shape
    f = wg.shape[1]
    nbh = t // halo
    return pl.pallas_call(
        functools.partial(_ffn1_kernel, seq=seq),
        grid=(t // tm, f // tn),
        in_specs=[
            pl.BlockSpec((tm, d), lambda i, j: (i, 0)),
            pl.BlockSpec((halo, d), lambda i, j: (jnp.maximum(i * (tm // halo) - 1, 0), 0)),
            pl.BlockSpec((halo, d), lambda i, j: (jnp.minimum((i + 1) * (tm // halo), nbh - 1), 0)),
            pl.BlockSpec((d, tn), lambda i, j: (0, j)),
            pl.BlockSpec((d, tn), lambda i, j: (0, j)),
            pl.BlockSpec((cw.shape[0], tn), lambda i, j: (0, j)),
            pl.BlockSpec((1, tn), lambda i, j: (0, j)),
        ],
        out_specs=pl.BlockSpec((tm, tn), lambda i, j: (i, j)),
        out_shape=jax.ShapeDtypeStruct((t, f), BF16),
        compiler_params=_cparams("parallel", "arbitrary"),
        name="ffn_gate_up",
    )(h, h, h, wg, wu, cw, cb)


def _ffn2_kernel(a_ref, wd_ref, x1_ref, nf_ref, o_ref):
    kidx = pl.program_id(1)

    @pl.when(kidx == 0)
    def _():
        o_ref[...] = x1_ref[...]

    o_ref[...] += jnp.dot(a_ref[...], wd_ref[...], preferred_element_type=F32)

    @pl.when(kidx == pl.num_programs(1) - 1)
    def _():
        xf = o_ref[...]
        ms = jnp.mean(xf * xf, axis=-1, keepdims=True)
        o_ref[...] = xf * lax.rsqrt(ms + NORM_EPS) * nf_ref[...]


def _ffn2(act, wd, x1, nf, tm=512, tk=2816):
    t, f = act.shape
    d = x1.shape[1]
    return pl.pallas_call(
        _ffn2_kernel,
        grid=(t // tm, f // tk),
        in_specs=[
            pl.BlockSpec((tm, tk), lambda i, k: (i, k)),
            pl.BlockSpec((tk, d), lambda i, k: (k, 0)),
            pl.BlockSpec((tm, d), lambda i, k: (i, 0)),
            pl.BlockSpec((1, d), lambda i, k: (0, 0)),
        ],
        out_specs=pl.BlockSpec((tm, d), lambda i, k: (i, 0)),
        out_shape=jax.ShapeDtypeStruct((t, d), F32),
        compiler_params=_cparams("parallel", "arbitrary"),
        name="ffn_down",
    )(act, wd, x1, nf)


def _pad_cols(a, n):
    return jnp.pad(a, ((0, 0), (0, n - a.shape[1])))


def _pad_rows(a, n):
    return jnp.pad(a, ((0, n - a.shape[0]), (0, 0)))


def _layer(x, batch, seq, norm1_g, w_in, sgu_ln_g, sgu_ln_b, sgu_w, sgu_b,
           mu_prev, mu_next, w0_f, w2_f, a0_f, a2_f, w0_b, w2_b, a0_b, a2_b, k_k, k_a, r_k,
           g2, gn_g, gn_b, w_proj_a, w_proj_b, w_out, norm2_g, ffn_w_gate, ffn_w_up,
           ffn_conv_w, ffn_conv_b, ffn_w_down, norm_out_g):
    d = x.shape[1]
    sgu_width = sgu_ln_g.shape[0]
    width = k_k.shape[0]
    dl = w2_f.shape[0]
    al = a2_f.shape[0]
    feat_w = mu_prev.shape[0]
    feat_pad = -(-feat_w // (4 * LANES)) * (4 * LANES)
    assert dl == HEAD_DIM and al == HEAD_DIM and 2 * dl == LANES
    row = lambda a: a.reshape(1, -1)

    o1 = 2 * sgu_width
    o2 = o1 + feat_w
    w_t = jnp.swapaxes(w_in, 0, 1)
    w_gates = w_in[:, o2:].astype(BF16)
    zw = jnp.zeros_like(w2_f)
    w2c = _stack_split3(jnp.concatenate([jnp.concatenate([w2_f, zw], 1),
                                         jnp.concatenate([zw, w2_b], 1)], 0))
    za = jnp.zeros_like(a2_f)
    a2c = _stack_split3(jnp.concatenate([jnp.concatenate([a2_f, za], 1),
                                         jnp.concatenate([za, a2_b], 1)], 0))
    g2p = _stack_split3(_pad_rows(g2, feat_pad - 3 * width - 2 * LANES))
    w0c = row(jnp.concatenate([w0_f, w0_b]))
    a0c = row(jnp.concatenate([a0_f, a0_b]))
    mup = _pad_cols(row(mu_prev), feat_pad)
    mun = _pad_cols(row(mu_next), feat_pad)
    bs_full = jnp.repeat(sgu_b.T, sgu_width // sgu_b.shape[0], axis=1)

    h1 = _rmsnorm(x, row(norm1_g))
    uv, feat = _in_proj(h1, w_t, o1, feat_pad)

    y_a = _sgu(uv, row(sgu_ln_g), row(sgu_ln_b), sgu_w.astype(BF16), bs_full)

    sh, lw, bk, bonus, gg, gates = _rwkv_prep(feat, h1, w_gates, seq, width, mup, mun, w0c, w2c,
                                              a0c, a2c, g2p, row(k_k), row(k_a), row(r_k))
    o_f, o_b = _rwkv_scan(sh, lw, bk, batch, seq)

    x1, h2 = _mix_out(y_a, o_f, o_b, bonus, gg, row(gn_g), row(gn_b), w_proj_a.astype(BF16),
                      w_proj_b.astype(BF16), gates, w_out.astype(BF16), x, row(norm2_g))

    act = _ffn1(h2, ffn_w_gate, ffn_w_up, ffn_conv_w, row(ffn_conv_b), seq)
    return _ffn2(act, ffn_w_down.astype(BF16), x1, norm_out_g)


def kernel(x, norm1_g, w_in, sgu_ln_g, sgu_ln_b, sgu_w, sgu_b, rwkv_mu_prev, rwkv_mu_next, rwkv_w0_f, rwkv_w2_f, rwkv_a0_f, rwkv_a2_f, rwkv_w0_b, rwkv_w2_b, rwkv_a0_b, rwkv_a2_b, rwkv_k_k, rwkv_k_a, rwkv_r_k, rwkv_g2, rwkv_gn_g, rwkv_gn_b, w_proj_a, w_proj_b, w_out, norm2_g, ffn_w_gate, ffn_w_up, ffn_conv_w, ffn_conv_b, ffn_w_down, norm_f_g):
    batch, seq, d = x.shape
    depth = norm1_g.shape[0]
    assert depth == 1, "the fused final RMSNorm assumes a single layer"
    xf = x.reshape(batch * seq, d)
    per_layer = (norm1_g, w_in, sgu_ln_g, sgu_ln_b, sgu_w, sgu_b, rwkv_mu_prev, rwkv_mu_next,
                 rwkv_w0_f, rwkv_w2_f, rwkv_a0_f, rwkv_a2_f, rwkv_w0_b, rwkv_w2_b, rwkv_a0_b,
                 rwkv_a2_b, rwkv_k_k, rwkv_k_a, rwkv_r_k, rwkv_g2, rwkv_gn_g, rwkv_gn_b,
                 w_proj_a, w_proj_b, w_out, norm2_g, ffn_w_gate, ffn_w_up, ffn_conv_w,
                 ffn_conv_b, ffn_w_down)
    out = _layer(xf, batch, seq, *(p[0] for p in per_layer), norm_f_g.reshape(1, d))
    return out.reshape(batch, seq, d)
```

```python
import functools

import jax
import jax.numpy as jnp
from jax import lax
from jax.experimental import pallas as pl
from jax.experimental.pallas import tpu as pltpu

F32 = jnp.float32
BF16 = jnp.bfloat16

NORM_EPS = 1e-6
LN_EPS = 1e-5
GN_EPS = 64e-5

HEAD_DIM = 64
LANES = 128
SGU_CHUNK = 128
SCAN_CHUNK = 64
VMEM_LIMIT = 48 * 1024 * 1024


def _cparams(*sem):
    return pltpu.CompilerParams(dimension_semantics=sem, vmem_limit_bytes=VMEM_LIMIT)


def _rmsnorm_kernel(x_ref, g_ref, h_ref):
    x = x_ref[...]
    ms = jnp.mean(x * x, axis=-1, keepdims=True)
    h_ref[...] = (x * lax.rsqrt(ms + NORM_EPS) * g_ref[...]).astype(h_ref.dtype)


def _rmsnorm(x, g, tm=1024):
    t, d = x.shape
    return pl.pallas_call(
        _rmsnorm_kernel,
        grid=(t // tm,),
        in_specs=[pl.BlockSpec((tm, d), lambda i: (i, 0)), pl.BlockSpec((1, d), lambda i: (0, 0))],
        out_specs=pl.BlockSpec((tm, d), lambda i: (i, 0)),
        out_shape=jax.ShapeDtypeStruct((t, d), BF16),
        compiler_params=_cparams("parallel"),
        name="rmsnorm",
    )(x, g)


def _in_proj_kernel(h_ref, hb_ref, ha_ref, w_ref, mup_ref, mun_ref, uv_ref, xs_ref, hx_ref,
                    *, n_uv, seq):
    tm = h_ref.shape[0]
    halo = hb_ref.shape[0]
    i = pl.program_id(0)
    j = pl.program_id(1)
    tiles_per_seq = seq // tm
    first = (i % tiles_per_seq) == 0
    last = (i % tiles_per_seq) == tiles_per_seq - 1
    nt = (((1,), (1,)), ((), ()))

    @pl.when(j < n_uv)
    def _():
        uv_ref[...] = lax.dot_general(h_ref[...], w_ref[...].astype(BF16), nt,
                                      preferred_element_type=F32).astype(uv_ref.dtype)

    @pl.when(j == n_uv)
    def _():
        hx_ref[0:halo, :] = hb_ref[...]
        hx_ref[halo:halo + tm, :] = h_ref[...]
        hx_ref[halo + tm:, :] = ha_ref[...]

    @pl.when(j >= n_uv)
    def _():
        f_ext = lax.dot_general(hx_ref[...], w_ref[...].astype(BF16), nt,
                                preferred_element_type=F32)
        rows = tm + 2 * halo
        row = lax.broadcasted_iota(jnp.int32, (tm, 1), 0)
        f = f_ext[halo:halo + tm]
        prev = pltpu.roll(f_ext, 1, 0)[halo:halo + tm]
        prev = jnp.where(jnp.logical_and(row == 0, first), 0.0, prev)
        nxt = pltpu.roll(f_ext, rows - 1, 0)[halo:halo + tm]
        nxt = jnp.where(jnp.logical_and(row == tm - 1, last), 0.0, nxt)
        xs_ref[...] = (f + mup_ref[...] * (prev - f) + mun_ref[...] * (nxt - f)).astype(xs_ref.dtype)


def _in_proj(h, w_t, mup, mun, w_uv, w_ft, seq, tm=1024, tn=512, halo=16):
    t, d = h.shape
    n_uv, n_ft = w_uv // tn, w_ft // tn
    nbh = t // halo
    ft = lambda j: jnp.maximum(j - n_uv, 0)
    return pl.pallas_call(
        functools.partial(_in_proj_kernel, n_uv=n_uv, seq=seq),
        grid=(t // tm, n_uv + n_ft),
        in_specs=[
            pl.BlockSpec((tm, d), lambda i, j: (i, 0), pipeline_mode=pl.Buffered(1)),
            pl.BlockSpec((halo, d), lambda i, j: (jnp.maximum(i * (tm // halo) - 1, 0), 0)),
            pl.BlockSpec((halo, d), lambda i, j: (jnp.minimum((i + 1) * (tm // halo), nbh - 1), 0)),
            pl.BlockSpec((tn, d), lambda i, j: (j, 0)),
            pl.BlockSpec((1, tn), lambda i, j: (0, ft(j))),
            pl.BlockSpec((1, tn), lambda i, j: (0, ft(j))),
        ],
        out_specs=[
            pl.BlockSpec((tm, tn), lambda i, j: (i, jnp.minimum(j, n_uv - 1))),
            pl.BlockSpec((tm, tn), lambda i, j: (i, ft(j))),
        ],
        out_shape=[
            jax.ShapeDtypeStruct((t, n_uv * tn), BF16),
            jax.ShapeDtypeStruct((t, n_ft * tn), BF16),
        ],
        scratch_shapes=[pltpu.VMEM((tm + 2 * halo, d), BF16)],
        compiler_params=_cparams("parallel", "arbitrary"),
        name="in_proj",
    )(h, h, h, w_t, mup, mun)


def _gelu(x):
    return 0.5 * x * (1.0 + lax.erf(x * (2.0 ** -0.5)))


def _sgu_kernel(uv_ref, lng_ref, lnb_ref, ws_ref, bs_ref, o_ref, *, width, groups):
    tm = uv_ref.shape[0]
    gv = _gelu(uv_ref[:, width:].astype(F32))
    mu = jnp.mean(gv, axis=-1, keepdims=True)
    vc = gv - mu
    var = jnp.mean(vc * vc, axis=-1, keepdims=True)
    vn = (vc * lax.rsqrt(var + LN_EPS) * lng_ref[...] + lnb_ref[...]).astype(BF16)
    gd = width // groups
    for c in range(tm // SGU_CHUNK):
        rows = slice(c * SGU_CHUNK, (c + 1) * SGU_CHUNK)
        for g in range(groups):
            cols = slice(g * gd, (g + 1) * gd)
            mixed = jnp.dot(ws_ref[g], vn[rows, cols], preferred_element_type=F32)
            gu = _gelu(uv_ref[rows, cols].astype(F32))
            o_ref[rows, cols] = (gu * (mixed + bs_ref[:, cols])).astype(o_ref.dtype)


def _sgu(uv, ln_g, ln_b, ws, bs_full, tm=512):
    t, w2 = uv.shape
    width = w2 // 2
    groups = ws.shape[0]
    return pl.pallas_call(
        functools.partial(_sgu_kernel, width=width, groups=groups),
        grid=(t // tm,),
        in_specs=[
            pl.BlockSpec((tm, w2), lambda i: (i, 0)),
            pl.BlockSpec((1, width), lambda i: (0, 0)),
            pl.BlockSpec((1, width), lambda i: (0, 0)),
            pl.BlockSpec(ws.shape, lambda i: (0, 0, 0)),
            pl.BlockSpec(bs_full.shape, lambda i: (0, 0)),
        ],
        out_specs=pl.BlockSpec((tm, width), lambda i: (i, 0)),
        out_shape=jax.ShapeDtypeStruct((t, width), BF16),
        compiler_params=_cparams("parallel"),
        name="sgu_mixer",
    )(uv, ln_g, ln_b, ws, bs_full)


def _head_block_ones(scale):
    r = lax.broadcasted_iota(jnp.int32, (LANES, LANES), 0) // HEAD_DIM
    c = lax.broadcasted_iota(jnp.int32, (LANES, LANES), 1) // HEAD_DIM
    return jnp.where(r == c, scale, 0.0).astype(BF16)


def _split_bf16(x):
    hi = x.astype(BF16)
    return hi, (x - hi.astype(F32)).astype(BF16)


def _head_sum(x, bd):
    return jnp.dot(x.astype(BF16), bd, preferred_element_type=F32)


def _stack_split3(w):
    hi, lo = _split_bf16(w)
    return jnp.concatenate([hi, lo, hi], axis=0)


def _dot_split3(x, w_stack):
    hi, lo = _split_bf16(x)
    return jnp.dot(jnp.concatenate([hi, hi, lo], axis=1), w_stack, preferred_element_type=F32)


def _prep_kernel(xs_ref, w0_ref, w2_ref, a0_ref, a2_ref, g2_ref, kk_ref, ka_ref, rk_ref,
                 h_ref, wg_ref, sh_ref, lw_ref, bk_ref, bonus_ref, gg_ref, gt_ref, *, width):
    gt_ref[...] = jax.nn.sigmoid(
        lax.dot_general(h_ref[...], wg_ref[...], (((1,), (1,)), ((), ())),
                        preferred_element_type=F32)).astype(gt_ref.dtype)

    xs = xs_ref[...].astype(F32)
    r = xs[:, 0:width]
    k = xs[:, width:2 * width]
    v = xs[:, 2 * width:3 * width]
    o = 3 * width
    lo_w = xs[:, o:o + LANES]
    lo_a = xs[:, o + LANES:o + 2 * LANES]
    lo_g = xs[:, o + 2 * LANES:]

    wpre = w0_ref[...] + _dot_split3(jnp.tanh(lo_w), w2_ref[...])
    lw = -(2.718281828459045 ** -0.5) * jax.nn.sigmoid(wpre)
    a = jax.nn.sigmoid(a0_ref[...] + jnp.dot(lo_a.astype(BF16), a2_ref[...],
                                             preferred_element_type=F32))
    gg_ref[...] = jnp.dot(jax.nn.sigmoid(lo_g).astype(BF16), g2_ref[...],
                          preferred_element_type=F32).astype(gg_ref.dtype)
    lw_ref[0] = lw[:, :width]
    lw_ref[1] = lw[:, width:]

    bd = _head_block_ones(1.0)
    kk = k * kk_ref[...]
    sh_ref[0] = r.astype(sh_ref.dtype)
    sh_ref[1] = v.astype(sh_ref.dtype)
    for j in range(width // LANES):
        cols = slice(j * LANES, (j + 1) * LANES)
        kkj = kk[:, cols]
        ss = _head_sum(kkj * kkj, bd)
        kkn = kkj * lax.rsqrt(jnp.maximum(ss, 1e-24))
        sh_ref[2, :, cols] = kkn.astype(sh_ref.dtype)
        ksum = jnp.zeros_like(kkj)
        for d in range(2):
            dcols = slice(d * width + j * LANES, d * width + (j + 1) * LANES)
            a_d = a[:, dcols]
            k_d = k[:, cols] * (1.0 + (a_d - 1.0) * ka_ref[:, cols])
            bk_ref[d, 0, :, cols] = (kkn * a_d).astype(bk_ref.dtype)
            bk_ref[d, 1, :, cols] = k_d.astype(bk_ref.dtype)
            ksum = ksum + k_d
        bsum = _head_sum(r[:, cols] * ksum * rk_ref[:, cols], bd)
        bonus_ref[:, cols] = (bsum * v[:, cols]).astype(bonus_ref.dtype)


def _rwkv_prep(xs, h, wg, width, w0c, w2c, a0c, a2c, g2p, k_k, k_a, r_k, tm=128):
    t, fw = xs.shape
    ng, d = wg.shape
    r = lambda j, i: 2 * i + j
    row = lambda j, i: (0, 0)
    resident = lambda a: pl.BlockSpec(a.shape, row, pipeline_mode=pl.Buffered(1))
    rows = lambda w: pl.BlockSpec((tm, w), lambda j, i: (r(j, i), 0))
    return pl.pallas_call(
        functools.partial(_prep_kernel, width=width),
        grid=(2, t // (2 * tm)),
        in_specs=[
            rows(fw),
            pl.BlockSpec(w0c.shape, row),
            resident(w2c),
            pl.BlockSpec(a0c.shape, row),
            resident(a2c),
            resident(g2p),
            pl.BlockSpec((1, width), row),
            pl.BlockSpec((1, width), row),
            pl.BlockSpec((1, width), row),
            pl.BlockSpec((2 * tm, d), lambda j, i: (i, 0)),
            pl.BlockSpec((ng // 2, d), lambda j, i: (j, 0), pipeline_mode=pl.Buffered(1)),
        ],
        out_specs=[
            pl.BlockSpec((3, tm, width), lambda j, i: (0, r(j, i), 0)),
            pl.BlockSpec((2, tm, width), lambda j, i: (0, r(j, i), 0)),
            pl.BlockSpec((2, 2, tm, width), lambda j, i: (0, 0, r(j, i), 0)),
            rows(width),
            rows(width),
            pl.BlockSpec((2 * tm, ng // 2), lambda j, i: (i, j)),
        ],
        out_shape=[
            jax.ShapeDtypeStruct((3, t, width), BF16),
            jax.ShapeDtypeStruct((2, t, width), F32),
            jax.ShapeDtypeStruct((2, 2, t, width), BF16),
            jax.ShapeDtypeStruct((t, width), BF16),
            jax.ShapeDtypeStruct((t, width), BF16),
            jax.ShapeDtypeStruct((t, ng), BF16),
        ],
        compiler_params=_cparams("arbitrary", "arbitrary"),
        name="rwkv_prep",
    )(xs, w0c, w2c, a0c, a2c, g2p, k_k, k_a, r_k, h, wg)


def _hat(x, first_half):
    return jnp.concatenate([jnp.where(first_half, x, 0.0), jnp.where(first_half, 0.0, x)],
                           axis=0)


def _scan_kernel(shf_ref, shb_ref, lwf_ref, lwb_ref, bkf_ref, bkb_ref, *rest, group, n_cast,
                 cast_steps):
    cast_in, (of_ref, ob_ref), cast_out, s_ref = (rest[:n_cast], rest[n_cast:n_cast + 2],
                                                  rest[n_cast + 2:2 * n_cast + 2], rest[-1])
    c = SCAN_CHUNK
    batch, width = shf_ref.shape[1], shf_ref.shape[3]
    npair = width // LANES
    assert 2 * c == LANES and c == HEAD_DIM

    @pl.when(pl.program_id(0) == 0)
    def _():
        s_ref[...] = jnp.zeros_like(s_ref)

    @pl.when(pl.program_id(0) < cast_steps)
    def _():
        for src, dst in zip(cast_in, cast_out):
            dst[...] = src[...].astype(dst.dtype)

    bf = lambda t: t.astype(BF16)
    f32 = lambda t: t.astype(F32)
    dg = lambda p, q, dims: lax.dot_general(bf(p), bf(q), (dims, ((), ())),
                                            preferred_element_type=F32)
    dot = lambda p, q: dg(p, q, ((1,), (0,)))
    dot_nt = lambda p, q: dg(p, q, ((1,), (1,)))
    dot_tn = lambda p, q: dg(p, q, ((0,), (0,)))

    wt = lax.broadcasted_iota(jnp.int32, (c, LANES), 0)
    ws = lax.broadcasted_iota(jnp.int32, (c, LANES), 1) % c
    masks = ((wt > ws, wt >= ws), (wt < ws, wt <= ws))
    eye = jnp.where(wt == ws, 1.0, 0.0).astype(F32)
    half = lax.broadcasted_iota(jnp.int32, (c, LANES), 1) < c
    same_head = (lax.broadcasted_iota(jnp.int32, (LANES, LANES), 0) // HEAD_DIM
                 == lax.broadcasted_iota(jnp.int32, (LANES, LANES), 1) // HEAD_DIM)
    hat = lambda t: _hat(t, half)
    refs = ((shf_ref, lwf_ref, bkf_ref, of_ref), (shb_ref, lwb_ref, bkb_ref, ob_ref))

    units = [(d, b, p) for d in range(2) for b in range(batch) for p in range(npair)]
    for g0 in range(0, len(units), group):
        grp = units[g0:g0 + group]
        st = {}
        for (d, b, p) in grp:
            sh_ref, lw_ref, bk_ref, _ = refs[d]
            strict, incl = masks[d]
            cols = slice(p * LANES, (p + 1) * LANES)
            r = f32(sh_ref[0, b, :, cols])
            v = f32(sh_ref[1, b, :, cols])
            kk = f32(sh_ref[2, b, :, cols])
            lw = lw_ref[0, b, :, cols]
            bb = f32(bk_ref[0, 0, b, :, cols])
            kd = f32(bk_ref[0, 1, b, :, cols])
            pre = lw
            s = 1
            while s < c:
                pre = pre + jnp.where(wt >= s, pltpu.roll(pre, s, 0), 0.0)
                s *= 2
            gtot = pre[c - 1:c, :]
            gi = pre if d == 0 else gtot - pre + lw
            en = jnp.exp(-gi)
            ec = jnp.exp(gtot - gi)
            lhs = bf(jnp.concatenate([-kk * jnp.exp(gi - lw), r * jnp.exp(gi)], axis=0))
            rhs = jnp.concatenate([hat(bb * en), hat(kd * en)], axis=0)
            aa = dot_nt(lhs, rhs)
            a_ab = jnp.where(strict, aa[:c, :LANES], 0.0)
            a_ak = jnp.where(strict, aa[:c, LANES:], 0.0)
            ark = bf(jnp.concatenate([jnp.where(incl, aa[c:, :LANES], 0.0),
                                      jnp.where(incl, aa[c:, LANES:], 0.0)], axis=1))
            vh = bf(hat(v))
            sidx = (d * batch + b) * npair + p
            s0 = s_ref[sidx]
            ls = dot_nt(lhs, s0)
            st[(d, b, p)] = dict(
                sidx=sidx, cols=cols, v=v, s0=s0, ark=ark, vh=vh,
                bk=bf(jnp.concatenate([bb * ec, kd * ec], axis=0)),
                eg=jnp.exp(gtot), pw=a_ab, inv=eye + a_ab,
                rhs_u=ls[:c] + dot(a_ak, vh), rs=ls[c:])

        for u in grp:
            st[u]["pw"] = dot(st[u]["pw"], hat(st[u]["pw"]))
        n = 4
        while n < c:
            for u in grp:
                e = st[u]
                both = dot(jnp.concatenate([e["pw"], e["inv"]], axis=0), hat(e["pw"]))
                e["pw"] = both[:c]
                e["inv"] = e["inv"] + both[c:]
            n *= 2
        for u in grp:
            st[u]["inv"] = st[u]["inv"] + dot(st[u]["inv"], hat(st[u]["pw"]))

        for u in grp:
            st[u]["u"] = dot(st[u]["inv"], hat(st[u]["rhs_u"]))
        for (d, b, p) in grp:
            e = st[(d, b, p)]
            o_ref = refs[d][3]
            uvh = jnp.concatenate([bf(hat(e["u"])), e["vh"]], axis=0)
            o_ref[b, :, e["cols"]] = (e["rs"] + dot(e["ark"], uvh)).astype(o_ref.dtype)
        for u in grp:
            e = st[u]
            uv = jnp.concatenate([e["u"], e["v"]], axis=0)
            s_ref[e["sidx"]] = e["s0"] * e["eg"] + jnp.where(same_head, dot_tn(uv, e["bk"]), 0.0)


def _rwkv_scan(sh, lw, bk, batch, seq, to_bf16=(), group=16, cast_steps=32):
    _, t, width = sh.shape
    c = SCAN_CHUNK
    nc = seq // c
    sh = sh.reshape(3, batch, seq, width)
    lw = lw.reshape(2, batch, seq, width)
    bk = bk.reshape(2, 2, batch, seq, width)
    sh_spec = lambda f: pl.BlockSpec((3, batch, c, width), lambda i: (0, 0, f(i), 0))
    lw_spec = lambda d, f: pl.BlockSpec((1, batch, c, width), lambda i: (d, 0, f(i), 0))
    bk_spec = lambda d, f: pl.BlockSpec((1, 2, batch, c, width), lambda i: (d, 0, 0, f(i), 0))
    o_spec = lambda f: pl.BlockSpec((batch, c, width), lambda i: (0, f(i), 0))
    fwd = lambda i: i
    bwd = lambda i: nc - 1 - i
    cast_specs = [pl.BlockSpec((a.shape[0] // cast_steps, a.shape[1]),
                               lambda i: (jnp.minimum(i, cast_steps - 1), 0)) for a in to_bf16]
    outs = pl.pallas_call(
        functools.partial(_scan_kernel, group=group, n_cast=len(to_bf16), cast_steps=cast_steps),
        grid=(nc,),
        in_specs=[sh_spec(fwd), sh_spec(bwd), lw_spec(0, fwd), lw_spec(1, bwd),
                  bk_spec(0, fwd), bk_spec(1, bwd)] + cast_specs,
        out_specs=[o_spec(fwd), o_spec(bwd)] + cast_specs,
        out_shape=([jax.ShapeDtypeStruct((batch, seq, width), BF16)] * 2
                   + [jax.ShapeDtypeStruct(a.shape, BF16) for a in to_bf16]),
        scratch_shapes=[pltpu.VMEM((2 * batch * (width // LANES), LANES, LANES), F32)],
        compiler_params=_cparams("arbitrary"),
        name="rwkv_scan",
    )(sh, sh, lw, lw, bk, bk, *to_bf16)
    return (outs[0].reshape(t, width), outs[1].reshape(t, width)) + tuple(outs[2:])


def _rwkv_out_rows(of_ref, ob_ref, bonus_ref, gg_ref, gng_ref, gnb_ref):
    width = of_ref.shape[1]
    avg = _head_block_ones(1.0 / HEAD_DIM)
    tiles = []
    for j in range(width // LANES):
        cols = slice(j * LANES, (j + 1) * LANES)
        o = of_ref[:, cols].astype(F32) + ob_ref[:, cols].astype(F32)
        oc = o - _head_sum(o, avg)
        var = _head_sum(oc * oc, avg)
        y = oc * lax.rsqrt(var + GN_EPS) * gng_ref[:, cols] + gnb_ref[:, cols]
        tiles.append(((y + bonus_ref[:, cols].astype(F32))
                      * gg_ref[:, cols].astype(F32)).astype(BF16))
    return jnp.concatenate(tiles, axis=1)


def _mix_out_kernel(ya_ref, of_ref, ob_ref, bonus_ref, gg_ref, gng_ref, gnb_ref, wa_ref, wb_ref,
                    ga_ref, gb_ref, wo_ref, x_ref, g_ref, x1_ref, h2_ref):
    yb = _rwkv_out_rows(of_ref, ob_ref, bonus_ref, gg_ref, gng_ref, gnb_ref)
    pa = jnp.dot(ya_ref[...], wa_ref[...], preferred_element_type=F32)
    pb = jnp.dot(yb, wb_ref[...], preferred_element_type=F32)
    merged = (ga_ref[...].astype(F32) * pa + gb_ref[...].astype(F32) * pb).astype(BF16)
    x1 = x_ref[...] + jnp.dot(merged, wo_ref[...], preferred_element_type=F32)
    x1_ref[...] = x1
    ms = jnp.mean(x1 * x1, axis=-1, keepdims=True)
    h2_ref[...] = (x1 * lax.rsqrt(ms + NORM_EPS) * g_ref[...]).astype(h2_ref.dtype)


def _mix_out(ya, o_f, o_b, bonus, gg, gn_g, gn_b, wa, wb, gates, wo, x, g, tm=256):
    t, k = ya.shape
    d = x.shape[1]
    resident = lambda shape: pl.BlockSpec(shape, lambda i: (0, 0), pipeline_mode=pl.Buffered(1))
    rows = lambda w: pl.BlockSpec((tm, w), lambda i: (i, 0))
    return pl.pallas_call(
        _mix_out_kernel,
        grid=(t // tm,),
        in_specs=[
            rows(k), rows(k), rows(k), rows(k), rows(k),
            pl.BlockSpec((1, k), lambda i: (0, 0)),
            pl.BlockSpec((1, k), lambda i: (0, 0)),
            resident((k, d)),
            resident((k, d)),
            pl.BlockSpec((tm, d), lambda i: (i, 0)),
            pl.BlockSpec((tm, d), lambda i: (i, 1)),
            resident((d, d)),
            rows(d),
            pl.BlockSpec((1, d), lambda i: (0, 0)),
        ],
        out_specs=[rows(d), rows(d)],
        out_shape=[jax.ShapeDtypeStruct((t, d), F32), jax.ShapeDtypeStruct((t, d), BF16)],
        compiler_params=_cparams("parallel"),
        name="mix_out",
    )(ya, o_f, o_b, bonus, gg, gn_g, gn_b, wa, wb, gates, gates, wo, x, g)


def _ffn1_kernel(h_ref, hb_ref, ha_ref, wg_ref, wu_ref, cw_ref, cb_ref, a_ref, *, seq):
    tm = h_ref.shape[0]
    halo = hb_ref.shape[0]
    i = pl.program_id(0)
    tiles_per_seq = seq // tm
    first = (i % tiles_per_seq) == 0
    last = (i % tiles_per_seq) == tiles_per_seq - 1

    h = h_ref[...]
    h_ext = jnp.concatenate([hb_ref[...], h, ha_ref[...]], axis=0)
    g_ext = jnp.dot(h_ext, wg_ref[...].astype(BF16), preferred_element_type=F32)
    up = jnp.dot(h, wu_ref[...].astype(BF16), preferred_element_type=F32)
    rows = tm + 2 * halo
    row = lax.broadcasted_iota(jnp.int32, (tm, 1), 0)
    g = g_ext[halo:halo + tm]
    prev = pltpu.roll(g_ext, 1, 0)[halo:halo + tm]
    prev = jnp.where(jnp.logical_and(row == 0, first), 0.0, prev)
    nxt = pltpu.roll(g_ext, rows - 1, 0)[halo:halo + tm]
    nxt = jnp.where(jnp.logical_and(row == tm - 1, last), 0.0, nxt)
    gt = prev * cw_ref[0:1, :] + g * cw_ref[1:2, :] + nxt * cw_ref[2:3, :] + cb_ref[...]
    a_ref[...] = (jax.nn.silu(gt) * up).astype(a_ref.dtype)


def _ffn1(h, wg, wu, cw, cb, seq, tm=1024, tn=512, halo=16):
    t, d = h.shape
    f = wg.shape[1]
    nbh = t // halo
    return pl.pallas_call(
        functools.partial(_ffn1_kernel, seq=seq),
        grid=(t // tm, f // tn),
        in_specs=[
            pl.BlockSpec((tm, d), lambda i, j: (i, 0)),
            pl.BlockSpec((halo, d), lambda i, j: (jnp.maximum(i * (tm // halo) - 1, 0), 0)),
            pl.BlockSpec((halo, d), lambda i, j: (jnp.minimum((i + 1) * (tm // halo), nbh - 1), 0)),
            pl.BlockSpec((d, tn), lambda i, j: (0, j)),
            pl.BlockSpec((d, tn), lambda i, j: (0, j)),
            pl.BlockSpec((cw.shape[0], tn), lambda i, j: (0, j)),
            pl.BlockSpec((1, tn), lambda i, j: (0, j)),
        ],
        out_specs=pl.BlockSpec((tm, tn), lambda i, j: (i, j)),
        out_shape=jax.ShapeDtypeStruct((t, f), BF16),
        compiler_params=_cparams("parallel", "arbitrary"),
        name="ffn_gate_up",
    )(h, h, h, wg, wu, cw, cb)


def _ffn2_kernel(a_ref, wd_ref, x1_ref, nf_ref, o_ref):
    kidx = pl.program_id(1)

    @pl.when(kidx == 0)
    def _():
        o_ref[...] = x1_ref[...]

    o_ref[...] += jnp.dot(a_ref[...], wd_ref[...], preferred_element_type=F32)

    @pl.when(kidx == pl.num_programs(1) - 1)
    def _():
        xf = o_ref[...]
        ms = jnp.mean(xf * xf, axis=-1, keepdims=True)
        o_ref[...] = xf * lax.rsqrt(ms + NORM_EPS) * nf_ref[...]


def _ffn2(act, wd, x1, nf, tm=512, tk=2816):
    t, f = act.shape
    d = x1.shape[1]
    return pl.pallas_call(
        _ffn2_kernel,
        grid=(t // tm, f // tk),
        in_specs=[
            pl.BlockSpec((tm, tk), lambda i, k: (i, k)),
            pl.BlockSpec((tk, d), lambda i, k: (k, 0)),
            pl.BlockSpec((tm, d), lambda i, k: (i, 0)),
            pl.BlockSpec((1, d), lambda i, k: (0, 0)),
        ],
        out_specs=pl.BlockSpec((tm, d), lambda i, k: (i, 0)),
        out_shape=jax.ShapeDtypeStruct((t, d), F32),
        compiler_params=_cparams("parallel", "arbitrary"),
        name="ffn_down",
    )(act, wd, x1, nf)


def _pad_cols(a, n):
    return jnp.pad(a, ((0, 0), (0, n - a.shape[1])))


def _pad_rows(a, n):
    return jnp.pad(a, ((0, n - a.shape[0]), (0, 0)))


def _layer(x, batch, seq, norm1_g, w_in, sgu_ln_g, sgu_ln_b, sgu_w, sgu_b,
           mu_prev, mu_next, w0_f, w2_f, a0_f, a2_f, w0_b, w2_b, a0_b, a2_b, k_k, k_a, r_k,
           g2, gn_g, gn_b, w_proj_a, w_proj_b, w_out, norm2_g, ffn_w_gate, ffn_w_up,
           ffn_conv_w, ffn_conv_b, ffn_w_down, norm_out_g):
    d = x.shape[1]
    sgu_width = sgu_ln_g.shape[0]
    width = k_k.shape[0]
    dl = w2_f.shape[0]
    al = a2_f.shape[0]
    feat_w = mu_prev.shape[0]
    feat_pad = -(-feat_w // (4 * LANES)) * (4 * LANES)
    assert dl == HEAD_DIM and al == HEAD_DIM and 2 * dl == LANES
    row = lambda a: a.reshape(1, -1)

    o1 = 2 * sgu_width
    o2 = o1 + feat_w
    w_t = jnp.swapaxes(w_in, 0, 1)
    w_gates = w_t[o2:].astype(BF16)
    zw = jnp.zeros_like(w2_f)
    w2c = _stack_split3(jnp.concatenate([jnp.concatenate([w2_f, zw], 1),
                                         jnp.concatenate([zw, w2_b], 1)], 0))
    za = jnp.zeros_like(a2_f)
    a2c = jnp.concatenate([jnp.concatenate([a2_f, za], 1),
                           jnp.concatenate([za, a2_b], 1)], 0).astype(BF16)
    g2p = _pad_rows(g2, feat_pad - 3 * width - 2 * LANES).astype(BF16)
    w0c = row(jnp.concatenate([w0_f, w0_b]))
    a0c = row(jnp.concatenate([a0_f, a0_b]))
    mup = _pad_cols(row(mu_prev), feat_pad)
    mun = _pad_cols(row(mu_next), feat_pad)
    bs_full = jnp.repeat(sgu_b.T, sgu_width // sgu_b.shape[0], axis=1)

    h1 = _rmsnorm(x, row(norm1_g))
    uv, xs = _in_proj(h1, w_t, mup, mun, o1, feat_pad, seq)

    y_a = _sgu(uv, row(sgu_ln_g), row(sgu_ln_b), sgu_w.astype(BF16), bs_full)

    sh, lw, bk, bonus, gg, gates = _rwkv_prep(xs, h1, w_gates, width, w0c, w2c, a0c, a2c, g2p,
                                              row(k_k), row(k_a), row(r_k))
    o_f, o_b, wa_bf, wb_bf, wo_bf, wd_bf = _rwkv_scan(
        sh, lw, bk, batch, seq, to_bf16=(w_proj_a, w_proj_b, w_out, ffn_w_down))

    x1, h2 = _mix_out(y_a, o_f, o_b, bonus, gg, row(gn_g), row(gn_b), wa_bf, wb_bf, gates, wo_bf,
                      x, row(norm2_g))

    act = _ffn1(h2, ffn_w_gate, ffn_w_up, ffn_conv_w, row(ffn_conv_b), seq)
    return _ffn2(act, wd_bf, x1, norm_out_g)


def kernel(x, norm1_g, w_in, sgu_ln_g, sgu_ln_b, sgu_w, sgu_b, rwkv_mu_prev, rwkv_mu_next, rwkv_w0_f, rwkv_w2_f, rwkv_a0_f, rwkv_a2_f, rwkv_w0_b, rwkv_w2_b, rwkv_a0_b, rwkv_a2_b, rwkv_k_k, rwkv_k_a, rwkv_r_k, rwkv_g2, rwkv_gn_g, rwkv_gn_b, w_proj_a, w_proj_b, w_out, norm2_g, ffn_w_gate, ffn_w_up, ffn_conv_w, ffn_conv_b, ffn_w_down, norm_f_g):
    batch, seq, d = x.shape
    depth = norm1_g.shape[0]
    assert depth == 1, "the fused final RMSNorm assumes a single layer"
    xf = x.reshape(batch * seq, d)
    per_layer = (norm1_g, w_in, sgu_ln_g, sgu_ln_b, sgu_w, sgu_b, rwkv_mu_prev, rwkv_mu_next,
                 rwkv_w0_f, rwkv_w2_f, rwkv_a0_f, rwkv_a2_f, rwkv_w0_b, rwkv_w2_b, rwkv_a0_b,
                 rwkv_a2_b, rwkv_k_k, rwkv_k_a, rwkv_r_k, rwkv_g2, rwkv_gn_g, rwkv_gn_b,
                 w_proj_a, w_proj_b, w_out, norm2_g, ffn_w_gate, ffn_w_up, ffn_conv_w,
                 ffn_conv_b, ffn_w_down)
    out = _layer(xf, batch, seq, *(p[0] for p in per_layer), norm_f_g.reshape(1, d))
    return out.reshape(batch, seq, d)
```

```python
import functools

import jax
import jax.numpy as jnp
from jax import lax
from jax.experimental import pallas as pl
from jax.experimental.pallas import tpu as pltpu

F32 = jnp.float32
BF16 = jnp.bfloat16

NORM_EPS = 1e-6
LN_EPS = 1e-5
GN_EPS = 64e-5

HEAD_DIM = 64
LANES = 128
SGU_CHUNK = 128
SCAN_CHUNK = 64
VMEM_LIMIT = 48 * 1024 * 1024


def _cparams(*sem):
    return pltpu.CompilerParams(dimension_semantics=sem, vmem_limit_bytes=VMEM_LIMIT)


def _rmsnorm_kernel(x_ref, g_ref, h_ref):
    x = x_ref[...]
    ms = jnp.mean(x * x, axis=-1, keepdims=True)
    h_ref[...] = (x * lax.rsqrt(ms + NORM_EPS) * g_ref[...]).astype(h_ref.dtype)


def _rmsnorm(x, g, tm=1024):
    t, d = x.shape
    return pl.pallas_call(
        _rmsnorm_kernel,
        grid=(t // tm,),
        in_specs=[pl.BlockSpec((tm, d), lambda i: (i, 0)), pl.BlockSpec((1, d), lambda i: (0, 0))],
        out_specs=pl.BlockSpec((tm, d), lambda i: (i, 0)),
        out_shape=jax.ShapeDtypeStruct((t, d), BF16),
        compiler_params=_cparams("parallel"),
        name="rmsnorm",
    )(x, g)


def _in_proj_kernel(h_ref, hb_ref, ha_ref, w_ref, mup_ref, mun_ref, ga_ref, gb_ref,
                    uv_ref, xs_ref, wg_ref, hx_ref, *, n_uv, seq, cast_steps):
    tm = h_ref.shape[0]
    halo = hb_ref.shape[0]
    i = pl.program_id(0)
    j = pl.program_id(1)
    tiles_per_seq = seq // tm
    first = (i % tiles_per_seq) == 0
    last = (i % tiles_per_seq) == tiles_per_seq - 1
    nt = (((1,), (1,)), ((), ()))

    @pl.when(i * pl.num_programs(1) + j < cast_steps)
    def _():
        wg_ref[0] = ga_ref[...].astype(wg_ref.dtype)
        wg_ref[1] = gb_ref[...].astype(wg_ref.dtype)

    @pl.when(j < n_uv)
    def _():
        uv_ref[...] = lax.dot_general(h_ref[...], w_ref[...].astype(BF16), nt,
                                      preferred_element_type=F32).astype(uv_ref.dtype)

    @pl.when(j == n_uv)
    def _():
        zero = jnp.zeros_like(hb_ref[...])
        hx_ref[0:halo, :] = jnp.where(first, zero, hb_ref[...])
        hx_ref[halo:halo + tm, :] = h_ref[...]
        hx_ref[halo + tm:, :] = jnp.where(last, zero, ha_ref[...])

    @pl.when(j >= n_uv)
    def _():
        f_ext = lax.dot_general(hx_ref[...], w_ref[...].astype(BF16), nt,
                                preferred_element_type=F32)
        rows = tm + 2 * halo
        f = f_ext[halo:halo + tm]
        prev = pltpu.roll(f_ext, 1, 0)[halo:halo + tm]
        nxt = pltpu.roll(f_ext, rows - 1, 0)[halo:halo + tm]
        mup = mup_ref[...]
        mun = mun_ref[...]
        xs_ref[...] = (f * (1.0 - mup - mun) + mup * prev + mun * nxt).astype(xs_ref.dtype)


def _in_proj(h, w_t, mup, mun, w_uv, w_ft, gate_row0, seq, tm=1024, tn=512, halo=16,
             cast_rows=32):
    t, d = h.shape
    n_uv, n_ft = w_uv // tn, w_ft // tn
    nbh = t // halo
    ft = lambda j: jnp.maximum(j - n_uv, 0)
    half = (w_t.shape[0] - gate_row0) // 2
    cast_steps = half // cast_rows
    steps = n_uv + n_ft
    assert gate_row0 % cast_rows == 0 and half % cast_rows == 0 and cast_steps <= (t // tm) * steps
    cast = lambda i, j: jnp.minimum(i * steps + j, cast_steps - 1)
    return pl.pallas_call(
        functools.partial(_in_proj_kernel, n_uv=n_uv, seq=seq, cast_steps=cast_steps),
        grid=(t // tm, steps),
        in_specs=[
            pl.BlockSpec((tm, d), lambda i, j: (i, 0), pipeline_mode=pl.Buffered(1)),
            pl.BlockSpec((halo, d), lambda i, j: (jnp.maximum(i * (tm // halo) - 1, 0), 0)),
            pl.BlockSpec((halo, d), lambda i, j: (jnp.minimum((i + 1) * (tm // halo), nbh - 1), 0)),
            pl.BlockSpec((tn, d), lambda i, j: (j, 0)),
            pl.BlockSpec((1, tn), lambda i, j: (0, ft(j))),
            pl.BlockSpec((1, tn), lambda i, j: (0, ft(j))),
            pl.BlockSpec((cast_rows, d), lambda i, j: (gate_row0 // cast_rows + cast(i, j), 0)),
            pl.BlockSpec((cast_rows, d),
                         lambda i, j: ((gate_row0 + half) // cast_rows + cast(i, j), 0)),
        ],
        out_specs=[
            pl.BlockSpec((tm, tn), lambda i, j: (i, jnp.minimum(j, n_uv - 1))),
            pl.BlockSpec((tm, tn), lambda i, j: (i, ft(j))),
            pl.BlockSpec((2, cast_rows, d), lambda i, j: (0, cast(i, j), 0)),
        ],
        out_shape=[
            jax.ShapeDtypeStruct((t, n_uv * tn), BF16),
            jax.ShapeDtypeStruct((t, n_ft * tn), BF16),
            jax.ShapeDtypeStruct((2, half, d), BF16),
        ],
        scratch_shapes=[pltpu.VMEM((tm + 2 * halo, d), BF16)],
        compiler_params=_cparams("arbitrary", "arbitrary"),
        name="in_proj",
    )(h, h, h, w_t, mup, mun, w_t, w_t)


def _gelu(x):
    return 0.5 * x * (1.0 + lax.erf(x * (2.0 ** -0.5)))


def _sgu_kernel(uv_ref, lng_ref, lnb_ref, ws_ref, bs_ref, o_ref, *, width, groups):
    tm = uv_ref.shape[0]
    gv = _gelu(uv_ref[:, width:].astype(F32))
    mu = jnp.mean(gv, axis=-1, keepdims=True)
    vc = gv - mu
    var = jnp.mean(vc * vc, axis=-1, keepdims=True)
    vn = (vc * lax.rsqrt(var + LN_EPS) * lng_ref[...] + lnb_ref[...]).astype(BF16)
    gd = width // groups
    for c in range(tm // SGU_CHUNK):
        rows = slice(c * SGU_CHUNK, (c + 1) * SGU_CHUNK)
        for g in range(groups):
            cols = slice(g * gd, (g + 1) * gd)
            mixed = jnp.dot(ws_ref[g], vn[rows, cols], preferred_element_type=F32)
            gu = _gelu(uv_ref[rows, cols].astype(F32))
            o_ref[rows, cols] = (gu * (mixed + bs_ref[:, cols])).astype(o_ref.dtype)


def _sgu(uv, ln_g, ln_b, ws, bs_full, tm=512):
    t, w2 = uv.shape
    width = w2 // 2
    groups = ws.shape[0]
    return pl.pallas_call(
        functools.partial(_sgu_kernel, width=width, groups=groups),
        grid=(t // tm,),
        in_specs=[
            pl.BlockSpec((tm, w2), lambda i: (i, 0)),
            pl.BlockSpec((1, width), lambda i: (0, 0)),
            pl.BlockSpec((1, width), lambda i: (0, 0)),
            pl.BlockSpec(ws.shape, lambda i: (0, 0, 0)),
            pl.BlockSpec(bs_full.shape, lambda i: (0, 0)),
        ],
        out_specs=pl.BlockSpec((tm, width), lambda i: (i, 0)),
        out_shape=jax.ShapeDtypeStruct((t, width), BF16),
        compiler_params=_cparams("parallel"),
        name="sgu_mixer",
    )(uv, ln_g, ln_b, ws, bs_full)


def _head_block_ones(scale):
    r = lax.broadcasted_iota(jnp.int32, (LANES, LANES), 0) // HEAD_DIM
    c = lax.broadcasted_iota(jnp.int32, (LANES, LANES), 1) // HEAD_DIM
    return jnp.where(r == c, scale, 0.0).astype(BF16)


def _head_sum(x, bd):
    return jnp.dot(x.astype(BF16), bd, preferred_element_type=F32)


def _prep_kernel(xs_ref, w0_ref, w2_ref, a0_ref, a2_ref, g2_ref, kk_ref, ka_ref, rk_ref,
                 h_ref, wg_ref, sh_ref, lw_ref, bk_ref, bonus_ref, gg_ref, gt_ref, *, width):
    gt_ref[...] = jax.nn.sigmoid(
        lax.dot_general(h_ref[...], wg_ref[...], (((1,), (1,)), ((), ())),
                        preferred_element_type=F32)).astype(gt_ref.dtype)

    xs = xs_ref[...].astype(F32)
    r = xs[:, 0:width]
    k = xs[:, width:2 * width]
    v = xs[:, 2 * width:3 * width]
    o = 3 * width
    lo_w = xs[:, o:o + LANES]
    lo_a = xs[:, o + LANES:o + 2 * LANES]
    lo_g = xs[:, o + 2 * LANES:]

    wpre = w0_ref[...] + jnp.dot(jnp.tanh(lo_w).astype(BF16), w2_ref[...],
                                 preferred_element_type=F32)
    lw = -(2.718281828459045 ** -0.5) * jax.nn.sigmoid(wpre)
    a = jax.nn.sigmoid(a0_ref[...] + jnp.dot(lo_a.astype(BF16), a2_ref[...],
                                             preferred_element_type=F32))
    gg_ref[...] = jnp.dot(jax.nn.sigmoid(lo_g).astype(BF16), g2_ref[...],
                          preferred_element_type=F32).astype(gg_ref.dtype)
    lw_ref[0] = lw[:, :width]
    lw_ref[1] = lw[:, width:]

    bd = _head_block_ones(1.0)
    kk = k * kk_ref[...]
    sh_ref[0] = r.astype(sh_ref.dtype)
    sh_ref[1] = v.astype(sh_ref.dtype)
    for j in range(width // LANES):
        cols = slice(j * LANES, (j + 1) * LANES)
        kkj = kk[:, cols]
        ss = _head_sum(kkj * kkj, bd)
        kkn = kkj * lax.rsqrt(jnp.maximum(ss, 1e-24))
        sh_ref[2, :, cols] = kkn.astype(sh_ref.dtype)
        ksum = jnp.zeros_like(kkj)
        for d in range(2):
            dcols = slice(d * width + j * LANES, d * width + (j + 1) * LANES)
            a_d = a[:, dcols]
            k_d = k[:, cols] * (1.0 + (a_d - 1.0) * ka_ref[:, cols])
            bk_ref[d, 0, :, cols] = (kkn * a_d).astype(bk_ref.dtype)
            bk_ref[d, 1, :, cols] = k_d.astype(bk_ref.dtype)
            ksum = ksum + k_d
        bsum = _head_sum(r[:, cols] * ksum * rk_ref[:, cols], bd)
        bonus_ref[:, cols] = (bsum * v[:, cols]).astype(bonus_ref.dtype)


def _rwkv_prep(xs, h, wg, width, w0c, w2c, a0c, a2c, g2p, k_k, k_a, r_k, tm=128):
    t, fw = xs.shape
    _, ngh, d = wg.shape
    r = lambda j, i: 2 * i + j
    row = lambda j, i: (0, 0)
    resident = lambda a: pl.BlockSpec(a.shape, row, pipeline_mode=pl.Buffered(1))
    rows = lambda w: pl.BlockSpec((tm, w), lambda j, i: (r(j, i), 0))
    return pl.pallas_call(
        functools.partial(_prep_kernel, width=width),
        grid=(2, t // (2 * tm)),
        in_specs=[
            rows(fw),
            pl.BlockSpec(w0c.shape, row),
            resident(w2c),
            pl.BlockSpec(a0c.shape, row),
            resident(a2c),
            resident(g2p),
            pl.BlockSpec((1, width), row),
            pl.BlockSpec((1, width), row),
            pl.BlockSpec((1, width), row),
            pl.BlockSpec((2 * tm, d), lambda j, i: (i, 0)),
            pl.BlockSpec((None, ngh, d), lambda j, i: (j, 0, 0), pipeline_mode=pl.Buffered(1)),
        ],
        out_specs=[
            pl.BlockSpec((3, tm, width), lambda j, i: (0, r(j, i), 0)),
            pl.BlockSpec((2, tm, width), lambda j, i: (0, r(j, i), 0)),
            pl.BlockSpec((2, 2, tm, width), lambda j, i: (0, 0, r(j, i), 0)),
            rows(width),
            rows(width),
            pl.BlockSpec((2 * tm, ngh), lambda j, i: (i, j)),
        ],
        out_shape=[
            jax.ShapeDtypeStruct((3, t, width), BF16),
            jax.ShapeDtypeStruct((2, t, width), F32),
            jax.ShapeDtypeStruct((2, 2, t, width), BF16),
            jax.ShapeDtypeStruct((t, width), BF16),
            jax.ShapeDtypeStruct((t, width), BF16),
            jax.ShapeDtypeStruct((t, 2 * ngh), BF16),
        ],
        compiler_params=_cparams("arbitrary", "arbitrary"),
        name="rwkv_prep",
    )(xs, w0c, w2c, a0c, a2c, g2p, k_k, k_a, r_k, h, wg)


def _hat(x, first_half):
    return jnp.concatenate([jnp.where(first_half, x, 0.0), jnp.where(first_half, 0.0, x)],
                           axis=0)


def _scan_kernel(shf_ref, shb_ref, lwf_ref, lwb_ref, bkf_ref, bkb_ref, *rest, group, n_cast,
                 cast_steps):
    cast_in, (of_ref, ob_ref), cast_out, s_ref = (rest[:n_cast], rest[n_cast:n_cast + 2],
                                                  rest[n_cast + 2:2 * n_cast + 2], rest[-1])
    c = SCAN_CHUNK
    batch, width = shf_ref.shape[1], shf_ref.shape[3]
    npair = width // LANES
    assert 2 * c == LANES and c == HEAD_DIM

    @pl.when(pl.program_id(0) == 0)
    def _():
        s_ref[...] = jnp.zeros_like(s_ref)

    @pl.when(pl.program_id(0) < cast_steps)
    def _():
        for src, dst in zip(cast_in, cast_out):
            dst[...] = src[...].astype(dst.dtype)

    bf = lambda t: t.astype(BF16)
    f32 = lambda t: t.astype(F32)
    dg = lambda p, q, dims: lax.dot_general(bf(p), bf(q), (dims, ((), ())),
                                            preferred_element_type=F32)
    dot = lambda p, q: dg(p, q, ((1,), (0,)))
    dot_nt = lambda p, q: dg(p, q, ((1,), (1,)))
    dot_tn = lambda p, q: dg(p, q, ((0,), (0,)))

    wt = lax.broadcasted_iota(jnp.int32, (c, LANES), 0)
    ws = lax.broadcasted_iota(jnp.int32, (c, LANES), 1) % c
    masks = ((wt > ws, wt >= ws), (wt < ws, wt <= ws))
    eye = jnp.where(wt == ws, 1.0, 0.0).astype(F32)
    half = lax.broadcasted_iota(jnp.int32, (c, LANES), 1) < c
    same_head = (lax.broadcasted_iota(jnp.int32, (LANES, LANES), 0) // HEAD_DIM
                 == lax.broadcasted_iota(jnp.int32, (LANES, LANES), 1) // HEAD_DIM)
    hat = lambda t: _hat(t, half)
    refs = ((shf_ref, lwf_ref, bkf_ref, of_ref), (shb_ref, lwb_ref, bkb_ref, ob_ref))

    units = [(d, b, p) for d in range(2) for b in range(batch) for p in range(npair)]
    for g0 in range(0, len(units), group):
        grp = units[g0:g0 + group]
        st = {}
        for (d, b, p) in grp:
            sh_ref, lw_ref, bk_ref, _ = refs[d]
            strict, incl = masks[d]
            cols = slice(p * LANES, (p + 1) * LANES)
            r = f32(sh_ref[0, b, :, cols])
            v = f32(sh_ref[1, b, :, cols])
            kk = f32(sh_ref[2, b, :, cols])
            lw = lw_ref[0, b, :, cols]
            bb = f32(bk_ref[0, 0, b, :, cols])
            kd = f32(bk_ref[0, 1, b, :, cols])
            pre = lw
            s = 1
            while s < c:
                pre = pre + jnp.where(wt >= s, pltpu.roll(pre, s, 0), 0.0)
                s *= 2
            gtot = pre[c - 1:c, :]
            gi = pre if d == 0 else gtot - pre + lw
            en = jnp.exp(-gi)
            ec = jnp.exp(gtot - gi)
            lhs = bf(jnp.concatenate([-kk * jnp.exp(gi - lw), r * jnp.exp(gi)], axis=0))
            rhs = jnp.concatenate([hat(bb * en), hat(kd * en)], axis=0)
            aa = dot_nt(lhs, rhs)
            a_ab = jnp.where(strict, aa[:c, :LANES], 0.0)
            a_ak = jnp.where(strict, aa[:c, LANES:], 0.0)
            ark = bf(jnp.concatenate([jnp.where(incl, aa[c:, :LANES], 0.0),
                                      jnp.where(incl, aa[c:, LANES:], 0.0)], axis=1))
            vh = bf(hat(v))
            sidx = (d * batch + b) * npair + p
            s0 = s_ref[sidx]
            ls = dot_nt(lhs, s0)
            st[(d, b, p)] = dict(
                sidx=sidx, cols=cols, v=v, s0=s0, ark=ark, vh=vh,
                bk=bf(jnp.concatenate([bb * ec, kd * ec], axis=0)),
                eg=jnp.exp(gtot), pw=a_ab, inv=eye + a_ab,
                rhs_u=ls[:c] + dot(a_ak, vh), rs=ls[c:])

        for u in grp:
            st[u]["pw"] = dot(st[u]["pw"], hat(st[u]["pw"]))
        n = 4
        while n < c:
            for u in grp:
                e = st[u]
                both = dot(jnp.concatenate([e["pw"], e["inv"]], axis=0), hat(e["pw"]))
                e["pw"] = both[:c]
                e["inv"] = e["inv"] + both[c:]
            n *= 2
        for u in grp:
            st[u]["inv"] = st[u]["inv"] + dot(st[u]["inv"], hat(st[u]["pw"]))

        for u in grp:
            st[u]["u"] = dot(st[u]["inv"], hat(st[u]["rhs_u"]))
        for (d, b, p) in grp:
            e = st[(d, b, p)]
            o_ref = refs[d][3]
            uvh = jnp.concatenate([bf(hat(e["u"])), e["vh"]], axis=0)
            o_ref[b, :, e["cols"]] = (e["rs"] + dot(e["ark"], uvh)).astype(o_ref.dtype)
        for u in grp:
            e = st[u]
            uv = jnp.concatenate([e["u"], e["v"]], axis=0)
            s_ref[e["sidx"]] = e["s0"] * e["eg"] + jnp.where(same_head, dot_tn(uv, e["bk"]), 0.0)


def _rwkv_scan(sh, lw, bk, batch, seq, to_bf16=(), group=16, cast_steps=32):
    _, t, width = sh.shape
    c = SCAN_CHUNK
    nc = seq // c
    sh = sh.reshape(3, batch, seq, width)
    lw = lw.reshape(2, batch, seq, width)
    bk = bk.reshape(2, 2, batch, seq, width)
    sh_spec = lambda f: pl.BlockSpec((3, batch, c, width), lambda i: (0, 0, f(i), 0))
    lw_spec = lambda d, f: pl.BlockSpec((1, batch, c, width), lambda i: (d, 0, f(i), 0))
    bk_spec = lambda d, f: pl.BlockSpec((1, 2, batch, c, width), lambda i: (d, 0, 0, f(i), 0))
    o_spec = lambda f: pl.BlockSpec((batch, c, width), lambda i: (0, f(i), 0))
    fwd = lambda i: i
    bwd = lambda i: nc - 1 - i
    cast_specs = [pl.BlockSpec((a.shape[0] // cast_steps, a.shape[1]),
                               lambda i: (jnp.minimum(i, cast_steps - 1), 0)) for a in to_bf16]
    outs = pl.pallas_call(
        functools.partial(_scan_kernel, group=group, n_cast=len(to_bf16), cast_steps=cast_steps),
        grid=(nc,),
        in_specs=[sh_spec(fwd), sh_spec(bwd), lw_spec(0, fwd), lw_spec(1, bwd),
                  bk_spec(0, fwd), bk_spec(1, bwd)] + cast_specs,
        out_specs=[o_spec(fwd), o_spec(bwd)] + cast_specs,
        out_shape=([jax.ShapeDtypeStruct((batch, seq, width), BF16)] * 2
                   + [jax.ShapeDtypeStruct(a.shape, BF16) for a in to_bf16]),
        scratch_shapes=[pltpu.VMEM((2 * batch * (width // LANES), LANES, LANES), F32)],
        compiler_params=_cparams("arbitrary"),
        name="rwkv_scan",
    )(sh, sh, lw, lw, bk, bk, *to_bf16)
    return (outs[0].reshape(t, width), outs[1].reshape(t, width)) + tuple(outs[2:])


def _rwkv_out_rows(of_ref, ob_ref, bonus_ref, gg_ref, gng_ref, gnb_ref):
    width = of_ref.shape[1]
    avg = _head_block_ones(1.0 / HEAD_DIM)
    tiles = []
    for j in range(width // LANES):
        cols = slice(j * LANES, (j + 1) * LANES)
        o = of_ref[:, cols].astype(F32) + ob_ref[:, cols].astype(F32)
        oc = o - _head_sum(o, avg)
        var = _head_sum(oc * oc, avg)
        y = oc * lax.rsqrt(var + GN_EPS) * gng_ref[:, cols] + gnb_ref[:, cols]
        tiles.append(((y + bonus_ref[:, cols].astype(F32))
                      * gg_ref[:, cols].astype(F32)).astype(BF16))
    return jnp.concatenate(tiles, axis=1)


def _mix_out_kernel(ya_ref, of_ref, ob_ref, bonus_ref, gg_ref, gng_ref, gnb_ref, wa_ref, wb_ref,
                    ga_ref, gb_ref, wo_ref, x_ref, g_ref, x1_ref, h2_ref):
    yb = _rwkv_out_rows(of_ref, ob_ref, bonus_ref, gg_ref, gng_ref, gnb_ref)
    pa = jnp.dot(ya_ref[...], wa_ref[...], preferred_element_type=F32)
    pb = jnp.dot(yb, wb_ref[...], preferred_element_type=F32)
    merged = (ga_ref[...].astype(F32) * pa + gb_ref[...].astype(F32) * pb).astype(BF16)
    x1 = x_ref[...] + jnp.dot(merged, wo_ref[...], preferred_element_type=F32)
    x1_ref[...] = x1
    ms = jnp.mean(x1 * x1, axis=-1, keepdims=True)
    h2_ref[...] = (x1 * lax.rsqrt(ms + NORM_EPS) * g_ref[...]).astype(h2_ref.dtype)


def _mix_out(ya, o_f, o_b, bonus, gg, gn_g, gn_b, wa, wb, gates, wo, x, g, tm=256):
    t, k = ya.shape
    d = x.shape[1]
    resident = lambda shape: pl.BlockSpec(shape, lambda i: (0, 0), pipeline_mode=pl.Buffered(1))
    rows = lambda w: pl.BlockSpec((tm, w), lambda i: (i, 0))
    return pl.pallas_call(
        _mix_out_kernel,
        grid=(t // tm,),
        in_specs=[
            rows(k), rows(k), rows(k), rows(k), rows(k),
            pl.BlockSpec((1, k), lambda i: (0, 0)),
            pl.BlockSpec((1, k), lambda i: (0, 0)),
            resident((k, d)),
            resident((k, d)),
            pl.BlockSpec((tm, d), lambda i: (i, 0)),
            pl.BlockSpec((tm, d), lambda i: (i, 1)),
            resident((d, d)),
            rows(d),
            pl.BlockSpec((1, d), lambda i: (0, 0)),
        ],
        out_specs=[rows(d), rows(d)],
        out_shape=[jax.ShapeDtypeStruct((t, d), F32), jax.ShapeDtypeStruct((t, d), BF16)],
        compiler_params=_cparams("parallel"),
        name="mix_out",
    )(ya, o_f, o_b, bonus, gg, gn_g, gn_b, wa, wb, gates, gates, wo, x, g)


def _ffn1_kernel(h_ref, hb_ref, ha_ref, wg_ref, wu_ref, cw_ref, cb_ref, a_ref, *, seq):
    tm = h_ref.shape[0]
    halo = hb_ref.shape[0]
    i = pl.program_id(0)
    tiles_per_seq = seq // tm
    first = (i % tiles_per_seq) == 0
    last = (i % tiles_per_seq) == tiles_per_seq - 1

    h = h_ref[...]
    zero = jnp.zeros_like(hb_ref[...])
    h_ext = jnp.concatenate([jnp.where(first, zero, hb_ref[...]), h,
                             jnp.where(last, zero, ha_ref[...])], axis=0)
    g_ext = jnp.dot(h_ext, wg_ref[...].astype(BF16), preferred_element_type=F32)
    up = jnp.dot(h, wu_ref[...].astype(BF16), preferred_element_type=F32)
    rows = tm + 2 * halo
    g = g_ext[halo:halo + tm]
    prev = pltpu.roll(g_ext, 1, 0)[halo:halo + tm]
    nxt = pltpu.roll(g_ext, rows - 1, 0)[halo:halo + tm]
    gt = prev * cw_ref[0:1, :] + g * cw_ref[1:2, :] + nxt * cw_ref[2:3, :] + cb_ref[...]
    a_ref[...] = (jax.nn.silu(gt) * up).astype(a_ref.dtype)


def _ffn1(h, wg, wu, cw, cb, seq, tm=1024, tn=512, halo=16):
    t, d = h.shape
    f = wg.shape[1]
    nbh = t // halo
    return pl.pallas_call(
        functools.partial(_ffn1_kernel, seq=seq),
        grid=(t // tm, f // tn),
        in_specs=[
            pl.BlockSpec((tm, d), lambda i, j: (i, 0)),
            pl.BlockSpec((halo, d), lambda i, j: (jnp.maximum(i * (tm // halo) - 1, 0), 0)),
            pl.BlockSpec((halo, d), lambda i, j: (jnp.minimum((i + 1) * (tm // halo), nbh - 1), 0)),
            pl.BlockSpec((d, tn), lambda i, j: (0, j)),
            pl.BlockSpec((d, tn), lambda i, j: (0, j)),
            pl.BlockSpec((cw.shape[0], tn), lambda i, j: (0, j)),
            pl.BlockSpec((1, tn), lambda i, j: (0, j)),
        ],
        out_specs=pl.BlockSpec((tm, tn), lambda i, j: (i, j)),
        out_shape=jax.ShapeDtypeStruct((t, f), BF16),
        compiler_params=_cparams("parallel", "arbitrary"),
        name="ffn_gate_up",
    )(h, h, h, wg, wu, cw, cb)


def _ffn2_kernel(a_ref, wd_ref, x1_ref, nf_ref, o_ref):
    kidx = pl.program_id(1)

    @pl.when(kidx == 0)
    def _():
        o_ref[...] = x1_ref[...] + jnp.dot(a_ref[...], wd_ref[...], preferred_element_type=F32)

    @pl.when(kidx > 0)
    def _():
        o_ref[...] += jnp.dot(a_ref[...], wd_ref[...], preferred_element_type=F32)

    @pl.when(kidx == pl.num_programs(1) - 1)
    def _():
        xf = o_ref[...]
        ms = jnp.mean(xf * xf, axis=-1, keepdims=True)
        o_ref[...] = xf * lax.rsqrt(ms + NORM_EPS) * nf_ref[...]


def _ffn2(act, wd, x1, nf, tm=512, tk=2816):
    t, f = act.shape
    d = x1.shape[1]
    return pl.pallas_call(
        _ffn2_kernel,
        grid=(t // tm, f // tk),
        in_specs=[
            pl.BlockSpec((tm, tk), lambda i, k: (i, k)),
            pl.BlockSpec((tk, d), lambda i, k: (k, 0)),
            pl.BlockSpec((tm, d), lambda i, k: (i, 0)),
            pl.BlockSpec((1, d), lambda i, k: (0, 0)),
        ],
        out_specs=pl.BlockSpec((tm, d), lambda i, k: (i, 0)),
        out_shape=jax.ShapeDtypeStruct((t, d), F32),
        compiler_params=_cparams("parallel", "arbitrary"),
        name="ffn_down",
    )(act, wd, x1, nf)


def _pad_cols(a, n):
    return jnp.pad(a, ((0, 0), (0, n - a.shape[1])))


def _pad_rows(a, n):
    return jnp.pad(a, ((0, n - a.shape[0]), (0, 0)))


def _layer(x, batch, seq, norm1_g, w_in, sgu_ln_g, sgu_ln_b, sgu_w, sgu_b,
           mu_prev, mu_next, w0_f, w2_f, a0_f, a2_f, w0_b, w2_b, a0_b, a2_b, k_k, k_a, r_k,
           g2, gn_g, gn_b, w_proj_a, w_proj_b, w_out, norm2_g, ffn_w_gate, ffn_w_up,
           ffn_conv_w, ffn_conv_b, ffn_w_down, norm_out_g):
    d = x.shape[1]
    sgu_width = sgu_ln_g.shape[0]
    width = k_k.shape[0]
    dl = w2_f.shape[0]
    al = a2_f.shape[0]
    feat_w = mu_prev.shape[0]
    feat_pad = -(-feat_w // (4 * LANES)) * (4 * LANES)
    assert dl == HEAD_DIM and al == HEAD_DIM and 2 * dl == LANES
    row = lambda a: a.reshape(1, -1)

    o1 = 2 * sgu_width
    o2 = o1 + feat_w
    w_t = jnp.swapaxes(w_in, 0, 1)
    zw = jnp.zeros_like(w2_f)
    w2c = jnp.concatenate([jnp.concatenate([w2_f, zw], 1),
                           jnp.concatenate([zw, w2_b], 1)], 0).astype(BF16)
    za = jnp.zeros_like(a2_f)
    a2c = jnp.concatenate([jnp.concatenate([a2_f, za], 1),
                           jnp.concatenate([za, a2_b], 1)], 0).astype(BF16)
    g2p = _pad_rows(g2, feat_pad - 3 * width - 2 * LANES).astype(BF16)
    w0c = row(jnp.concatenate([w0_f, w0_b]))
    a0c = row(jnp.concatenate([a0_f, a0_b]))
    mup = _pad_cols(row(mu_prev), feat_pad)
    mun = _pad_cols(row(mu_next), feat_pad)
    bs_full = jnp.repeat(sgu_b.T, sgu_width // sgu_b.shape[0], axis=1)

    h1 = _rmsnorm(x, row(norm1_g))
    uv, xs, w_gates = _in_proj(h1, w_t, mup, mun, o1, feat_pad, o2, seq)

    y_a = _sgu(uv, row(sgu_ln_g), row(sgu_ln_b), sgu_w.astype(BF16), bs_full)

    sh, lw, bk, bonus, gg, gates = _rwkv_prep(xs, h1, w_gates, width, w0c, w2c, a0c, a2c, g2p,
                                              row(k_k), row(k_a), row(r_k))
    o_f, o_b, wa_bf, wb_bf, wo_bf, wd_bf = _rwkv_scan(
        sh, lw, bk, batch, seq, to_bf16=(w_proj_a, w_proj_b, w_out, ffn_w_down))

    x1, h2 = _mix_out(y_a, o_f, o_b, bonus, gg, row(gn_g), row(gn_b), wa_bf, wb_bf, gates, wo_bf,
                      x, row(norm2_g))

    act = _ffn1(h2, ffn_w_gate, ffn_w_up, ffn_conv_w, row(ffn_conv_b), seq)
    return _ffn2(act, wd_bf, x1, norm_out_g)


def kernel(x, norm1_g, w_in, sgu_ln_g, sgu_ln_b, sgu_w, sgu_b, rwkv_mu_prev, rwkv_mu_next, rwkv_w0_f, rwkv_w2_f, rwkv_a0_f, rwkv_a2_f, rwkv_w0_b, rwkv_w2_b, rwkv_a0_b, rwkv_a2_b, rwkv_k_k, rwkv_k_a, rwkv_r_k, rwkv_g2, rwkv_gn_g, rwkv_gn_b, w_proj_a, w_proj_b, w_out, norm2_g, ffn_w_gate, ffn_w_up, ffn_conv_w, ffn_conv_b, ffn_w_down, norm_f_g):
    batch, seq, d = x.shape
    depth = norm1_g.shape[0]
    assert depth == 1, "the fused final RMSNorm assumes a single layer"
    xf = x.reshape(batch * seq, d)
    per_layer = (norm1_g, w_in, sgu_ln_g, sgu_ln_b, sgu_w, sgu_b, rwkv_mu_prev, rwkv_mu_next,
                 rwkv_w0_f, rwkv_w2_f, rwkv_a0_f, rwkv_a2_f, rwkv_w0_b, rwkv_w2_b, rwkv_a0_b,
                 rwkv_a2_b, rwkv_k_k, rwkv_k_a, rwkv_r_k, rwkv_g2, rwkv_gn_g, rwkv_gn_b,
                 w_proj_a, w_proj_b, w_out, norm2_g, ffn_w_gate, ffn_w_up, ffn_conv_w,
                 ffn_conv_b, ffn_w_down)
    out = _layer(xf, batch, seq, *(p[0] for p in per_layer), norm_f_g.reshape(1, d))
    return out.reshape(batch, seq, d)
```

```python
import functools

import jax
import jax.numpy as jnp
from jax import lax
from jax.experimental import pallas as pl
from jax.experimental.pallas import tpu as pltpu

F32 = jnp.float32
BF16 = jnp.bfloat16

NORM_EPS = 1e-6
LN_EPS = 1e-5
GN_EPS = 64e-5

HEAD_DIM = 64
LANES = 128
SGU_CHUNK = 128
SCAN_CHUNK = 64
VMEM_LIMIT = 48 * 1024 * 1024


def _cparams(*sem):
    return pltpu.CompilerParams(dimension_semantics=sem, vmem_limit_bytes=VMEM_LIMIT)


def _rmsnorm_kernel(x_ref, g_ref, h_ref):
    x = x_ref[...]
    ms = jnp.mean(x * x, axis=-1, keepdims=True)
    h_ref[...] = (x * lax.rsqrt(ms + NORM_EPS) * g_ref[...]).astype(h_ref.dtype)


def _rmsnorm(x, g, tm=1024):
    t, d = x.shape
    return pl.pallas_call(
        _rmsnorm_kernel,
        grid=(t // tm,),
        in_specs=[pl.BlockSpec((tm, d), lambda i: (i, 0)), pl.BlockSpec((1, d), lambda i: (0, 0))],
        out_specs=pl.BlockSpec((tm, d), lambda i: (i, 0)),
        out_shape=jax.ShapeDtypeStruct((t, d), BF16),
        compiler_params=_cparams("parallel"),
        name="rmsnorm",
    )(x, g)


def _in_proj_kernel(h_ref, hb_ref, ha_ref, w_ref, mup_ref, mun_ref, ga_ref, gb_ref,
                    uv_ref, xs_ref, wg_ref, hx_ref, *, n_uv, seq, cast_steps):
    tm = h_ref.shape[0]
    halo = hb_ref.shape[0]
    i = pl.program_id(0)
    j = pl.program_id(1)
    tiles_per_seq = seq // tm
    first = (i % tiles_per_seq) == 0
    last = (i % tiles_per_seq) == tiles_per_seq - 1
    nt = (((1,), (1,)), ((), ()))

    @pl.when(i * pl.num_programs(1) + j < cast_steps)
    def _():
        wg_ref[0] = ga_ref[...].astype(wg_ref.dtype)
        wg_ref[1] = gb_ref[...].astype(wg_ref.dtype)

    @pl.when(j < n_uv)
    def _():
        uv_ref[...] = lax.dot_general(h_ref[...], w_ref[...].astype(BF16), nt,
                                      preferred_element_type=F32).astype(uv_ref.dtype)

    @pl.when(j == n_uv)
    def _():
        zero = jnp.zeros_like(hb_ref[...])
        hx_ref[0:halo, :] = jnp.where(first, zero, hb_ref[...])
        hx_ref[halo:halo + tm, :] = h_ref[...]
        hx_ref[halo + tm:, :] = jnp.where(last, zero, ha_ref[...])

    @pl.when(j >= n_uv)
    def _():
        f_ext = lax.dot_general(hx_ref[...], w_ref[...].astype(BF16), nt,
                                preferred_element_type=F32)
        rows = tm + 2 * halo
        f = f_ext[halo:halo + tm]
        prev = pltpu.roll(f_ext, 1, 0)[halo:halo + tm]
        nxt = pltpu.roll(f_ext, rows - 1, 0)[halo:halo + tm]
        mup = mup_ref[...]
        mun = mun_ref[...]
        xs_ref[...] = (f * (1.0 - mup - mun) + mup * prev + mun * nxt).astype(xs_ref.dtype)


def _in_proj(h, w_t, mup, mun, w_uv, w_ft, gate_row0, seq, tm=1024, tn=512, halo=16,
             cast_rows=32):
    t, d = h.shape
    n_uv, n_ft = w_uv // tn, w_ft // tn
    nbh = t // halo
    ft = lambda j: jnp.maximum(j - n_uv, 0)
    half = (w_t.shape[0] - gate_row0) // 2
    cast_steps = half // cast_rows
    steps = n_uv + n_ft
    assert gate_row0 % cast_rows == 0 and half % cast_rows == 0 and cast_steps <= (t // tm) * steps
    cast = lambda i, j: jnp.minimum(i * steps + j, cast_steps - 1)
    return pl.pallas_call(
        functools.partial(_in_proj_kernel, n_uv=n_uv, seq=seq, cast_steps=cast_steps),
        grid=(t // tm, steps),
        in_specs=[
            pl.BlockSpec((tm, d), lambda i, j: (i, 0), pipeline_mode=pl.Buffered(1)),
            pl.BlockSpec((halo, d), lambda i, j: (jnp.maximum(i * (tm // halo) - 1, 0), 0)),
            pl.BlockSpec((halo, d), lambda i, j: (jnp.minimum((i + 1) * (tm // halo), nbh - 1), 0)),
            pl.BlockSpec((tn, d), lambda i, j: (j, 0)),
            pl.BlockSpec((1, tn), lambda i, j: (0, ft(j))),
            pl.BlockSpec((1, tn), lambda i, j: (0, ft(j))),
            pl.BlockSpec((cast_rows, d), lambda i, j: (gate_row0 // cast_rows + cast(i, j), 0)),
            pl.BlockSpec((cast_rows, d),
                         lambda i, j: ((gate_row0 + half) // cast_rows + cast(i, j), 0)),
        ],
        out_specs=[
            pl.BlockSpec((tm, tn), lambda i, j: (i, jnp.minimum(j, n_uv - 1))),
            pl.BlockSpec((tm, tn), lambda i, j: (i, ft(j))),
            pl.BlockSpec((2, cast_rows, d), lambda i, j: (0, cast(i, j), 0)),
        ],
        out_shape=[
            jax.ShapeDtypeStruct((t, n_uv * tn), BF16),
            jax.ShapeDtypeStruct((t, n_ft * tn), BF16),
            jax.ShapeDtypeStruct((2, half, d), BF16),
        ],
        scratch_shapes=[pltpu.VMEM((tm + 2 * halo, d), BF16)],
        compiler_params=_cparams("arbitrary", "arbitrary"),
        name="in_proj",
    )(h, h, h, w_t, mup, mun, w_t, w_t)


def _gelu(x):
    return 0.5 * x * (1.0 + lax.erf(x * (2.0 ** -0.5)))


def _sgu_kernel(uv_ref, lng_ref, lnb_ref, ws_ref, bs_ref, o_ref, *, width, groups):
    tm = uv_ref.shape[0]
    gv = _gelu(uv_ref[:, width:].astype(F32))
    mu = jnp.mean(gv, axis=-1, keepdims=True)
    vc = gv - mu
    var = jnp.mean(vc * vc, axis=-1, keepdims=True)
    vn = (vc * lax.rsqrt(var + LN_EPS) * lng_ref[...] + lnb_ref[...]).astype(BF16)
    gd = width // groups
    for c in range(tm // SGU_CHUNK):
        rows = slice(c * SGU_CHUNK, (c + 1) * SGU_CHUNK)
        for g in range(groups):
            cols = slice(g * gd, (g + 1) * gd)
            mixed = jnp.dot(ws_ref[g], vn[rows, cols], preferred_element_type=F32)
            gu = _gelu(uv_ref[rows, cols].astype(F32))
            o_ref[rows, cols] = (gu * (mixed + bs_ref[:, cols])).astype(o_ref.dtype)


def _sgu(uv, ln_g, ln_b, ws, bs_full, tm=512):
    t, w2 = uv.shape
    width = w2 // 2
    groups = ws.shape[0]
    return pl.pallas_call(
        functools.partial(_sgu_kernel, width=width, groups=groups),
        grid=(t // tm,),
        in_specs=[
            pl.BlockSpec((tm, w2), lambda i: (i, 0)),
            pl.BlockSpec((1, width), lambda i: (0, 0)),
            pl.BlockSpec((1, width), lambda i: (0, 0)),
            pl.BlockSpec(ws.shape, lambda i: (0, 0, 0)),
            pl.BlockSpec(bs_full.shape, lambda i: (0, 0)),
        ],
        out_specs=pl.BlockSpec((tm, width), lambda i: (i, 0)),
        out_shape=jax.ShapeDtypeStruct((t, width), BF16),
        compiler_params=_cparams("parallel"),
        name="sgu_mixer",
    )(uv, ln_g, ln_b, ws, bs_full)


def _head_block_ones(scale):
    r = lax.broadcasted_iota(jnp.int32, (LANES, LANES), 0) // HEAD_DIM
    c = lax.broadcasted_iota(jnp.int32, (LANES, LANES), 1) // HEAD_DIM
    return jnp.where(r == c, scale, 0.0).astype(BF16)


def _head_sum(x, bd):
    return jnp.dot(x.astype(BF16), bd, preferred_element_type=F32)


def _prep_kernel(xs_ref, w0_ref, w2_ref, a0_ref, a2_ref, g2_ref, kk_ref, ka_ref, rk_ref,
                 h_ref, wg_ref, sh_ref, lw_ref, bk_ref, bonus_ref, gg_ref, gt_ref, *, width):
    gt_ref[...] = jax.nn.sigmoid(
        lax.dot_general(h_ref[...], wg_ref[...], (((1,), (1,)), ((), ())),
                        preferred_element_type=F32)).astype(gt_ref.dtype)

    xs = xs_ref[...].astype(F32)
    r = xs[:, 0:width]
    k = xs[:, width:2 * width]
    v = xs[:, 2 * width:3 * width]
    o = 3 * width
    lo_w = xs[:, o:o + LANES]
    lo_a = xs[:, o + LANES:o + 2 * LANES]
    lo_g = xs[:, o + 2 * LANES:]

    wpre = w0_ref[...] + jnp.dot(jnp.tanh(lo_w).astype(BF16), w2_ref[...],
                                 preferred_element_type=F32)
    lw = -(2.718281828459045 ** -0.5) * jax.nn.sigmoid(wpre)
    a = jax.nn.sigmoid(a0_ref[...] + jnp.dot(lo_a.astype(BF16), a2_ref[...],
                                             preferred_element_type=F32))
    gg_ref[...] = jnp.dot(jax.nn.sigmoid(lo_g).astype(BF16), g2_ref[...],
                          preferred_element_type=F32).astype(gg_ref.dtype)
    lw_ref[0] = lw[:, :width]
    lw_ref[1] = lw[:, width:]

    bd = _head_block_ones(1.0)
    kk = k * kk_ref[...]
    sh_ref[0] = r.astype(sh_ref.dtype)
    sh_ref[1] = v.astype(sh_ref.dtype)
    for j in range(width // LANES):
        cols = slice(j * LANES, (j + 1) * LANES)
        kkj = kk[:, cols]
        ss = _head_sum(kkj * kkj, bd)
        kkn = kkj * lax.rsqrt(jnp.maximum(ss, 1e-24))
        sh_ref[2, :, cols] = kkn.astype(sh_ref.dtype)
        ksum = jnp.zeros_like(kkj)
        for d in range(2):
            dcols = slice(d * width + j * LANES, d * width + (j + 1) * LANES)
            a_d = a[:, dcols]
            k_d = k[:, cols] * (1.0 + (a_d - 1.0) * ka_ref[:, cols])
            bk_ref[d, 0, :, cols] = (kkn * a_d).astype(bk_ref.dtype)
            bk_ref[d, 1, :, cols] = k_d.astype(bk_ref.dtype)
            ksum = ksum + k_d
        bsum = _head_sum(r[:, cols] * ksum * rk_ref[:, cols], bd)
        bonus_ref[:, cols] = (bsum * v[:, cols]).astype(bonus_ref.dtype)


def _rwkv_prep(xs, h, wg, width, w0c, w2c, a0c, a2c, g2p, k_k, k_a, r_k, tm=128):
    t, fw = xs.shape
    _, ngh, d = wg.shape
    r = lambda j, i: 2 * i + j
    row = lambda j, i: (0, 0)
    resident = lambda a: pl.BlockSpec(a.shape, row, pipeline_mode=pl.Buffered(1))
    rows = lambda w: pl.BlockSpec((tm, w), lambda j, i: (r(j, i), 0))
    return pl.pallas_call(
        functools.partial(_prep_kernel, width=width),
        grid=(2, t // (2 * tm)),
        in_specs=[
            rows(fw),
            pl.BlockSpec(w0c.shape, row),
            resident(w2c),
            pl.BlockSpec(a0c.shape, row),
            resident(a2c),
            resident(g2p),
            pl.BlockSpec((1, width), row),
            pl.BlockSpec((1, width), row),
            pl.BlockSpec((1, width), row),
            pl.BlockSpec((2 * tm, d), lambda j, i: (i, 0)),
            pl.BlockSpec((None, ngh, d), lambda j, i: (j, 0, 0), pipeline_mode=pl.Buffered(1)),
        ],
        out_specs=[
            pl.BlockSpec((3, tm, width), lambda j, i: (0, r(j, i), 0)),
            pl.BlockSpec((2, tm, width), lambda j, i: (0, r(j, i), 0)),
            pl.BlockSpec((2, 2, tm, width), lambda j, i: (0, 0, r(j, i), 0)),
            rows(width),
            rows(width),
            pl.BlockSpec((2 * tm, ngh), lambda j, i: (i, j)),
        ],
        out_shape=[
            jax.ShapeDtypeStruct((3, t, width), BF16),
            jax.ShapeDtypeStruct((2, t, width), F32),
            jax.ShapeDtypeStruct((2, 2, t, width), BF16),
            jax.ShapeDtypeStruct((t, width), BF16),
            jax.ShapeDtypeStruct((t, width), BF16),
            jax.ShapeDtypeStruct((t, 2 * ngh), BF16),
        ],
        compiler_params=_cparams("arbitrary", "arbitrary"),
        name="rwkv_prep",
    )(xs, w0c, w2c, a0c, a2c, g2p, k_k, k_a, r_k, h, wg)


def _hat(x, first_half):
    return jnp.concatenate([jnp.where(first_half, x, 0.0), jnp.where(first_half, 0.0, x)],
                           axis=0)


def _scan_kernel(shf_ref, shb_ref, lwf_ref, lwb_ref, bkf_ref, bkb_ref, *rest, group, n_cast,
                 cast_steps):
    cast_in, (of_ref, ob_ref), cast_out, s_ref = (rest[:n_cast], rest[n_cast:n_cast + 2],
                                                  rest[n_cast + 2:2 * n_cast + 2], rest[-1])
    c = SCAN_CHUNK
    batch, width = shf_ref.shape[1], shf_ref.shape[3]
    npair = width // LANES
    assert 2 * c == LANES and c == HEAD_DIM

    @pl.when(pl.program_id(0) == 0)
    def _():
        s_ref[...] = jnp.zeros_like(s_ref)

    @pl.when(pl.program_id(0) < cast_steps)
    def _():
        for src, dst in zip(cast_in, cast_out):
            dst[...] = src[...].astype(dst.dtype)

    bf = lambda t: t.astype(BF16)
    f32 = lambda t: t.astype(F32)
    dg = lambda p, q, dims: lax.dot_general(bf(p), bf(q), (dims, ((), ())),
                                            preferred_element_type=F32)
    dot = lambda p, q: dg(p, q, ((1,), (0,)))
    dot_nt = lambda p, q: dg(p, q, ((1,), (1,)))
    dot_tn = lambda p, q: dg(p, q, ((0,), (0,)))

    wt = lax.broadcasted_iota(jnp.int32, (c, LANES), 0)
    ws = lax.broadcasted_iota(jnp.int32, (c, LANES), 1) % c
    masks = ((wt > ws, wt >= ws), (wt < ws, wt <= ws))
    eye = jnp.where(wt == ws, 1.0, 0.0).astype(F32)
    half = lax.broadcasted_iota(jnp.int32, (c, LANES), 1) < c
    same_head = (lax.broadcasted_iota(jnp.int32, (LANES, LANES), 0) // HEAD_DIM
                 == lax.broadcasted_iota(jnp.int32, (LANES, LANES), 1) // HEAD_DIM)
    hat = lambda t: _hat(t, half)
    refs = ((shf_ref, lwf_ref, bkf_ref, of_ref), (shb_ref, lwb_ref, bkb_ref, ob_ref))

    units = [(d, b, p) for d in range(2) for b in range(batch) for p in range(npair)]
    for g0 in range(0, len(units), group):
        grp = units[g0:g0 + group]
        st = {}
        for (d, b, p) in grp:
            sh_ref, lw_ref, bk_ref, _ = refs[d]
            strict, incl = masks[d]
            cols = slice(p * LANES, (p + 1) * LANES)
            r = f32(sh_ref[0, b, :, cols])
            v = f32(sh_ref[1, b, :, cols])
            kk = f32(sh_ref[2, b, :, cols])
            lw = lw_ref[0, b, :, cols]
            bb = f32(bk_ref[0, 0, b, :, cols])
            kd = f32(bk_ref[0, 1, b, :, cols])
            pre = lw
            s = 1
            while s < c:
                pre = pre + jnp.where(wt >= s, pltpu.roll(pre, s, 0), 0.0)
                s *= 2
            gtot = pre[c - 1:c, :]
            gi = pre if d == 0 else gtot - pre + lw
            en = jnp.exp(-gi)
            ec = jnp.exp(gtot - gi)
            lhs = bf(jnp.concatenate([-kk * jnp.exp(gi - lw), r * jnp.exp(gi)], axis=0))
            rhs = jnp.concatenate([hat(bb * en), hat(kd * en)], axis=0)
            aa = dot_nt(lhs, rhs)
            a_ab = jnp.where(strict, aa[:c, :LANES], 0.0)
            a_ak = jnp.where(strict, aa[:c, LANES:], 0.0)
            ark = bf(jnp.concatenate([jnp.where(incl, aa[c:, :LANES], 0.0),
                                      jnp.where(incl, aa[c:, LANES:], 0.0)], axis=1))
            vh = bf(hat(v))
            sidx = (d * batch + b) * npair + p
            s0 = s_ref[sidx]
            ls = dot_nt(lhs, s0)
            st[(d, b, p)] = dict(
                sidx=sidx, cols=cols, v=v, s0=s0, ark=ark, vh=vh,
                bk=bf(jnp.concatenate([bb * ec, kd * ec], axis=0)),
                eg=jnp.exp(gtot), pw=a_ab, inv=eye + a_ab,
                rhs_u=ls[:c] + dot(a_ak, vh), rs=ls[c:])

        for u in grp:
            st[u]["pw"] = dot(st[u]["pw"], hat(st[u]["pw"]))
        n = 4
        while n < c:
            for u in grp:
                e = st[u]
                both = dot(jnp.concatenate([e["pw"], e["inv"]], axis=0), hat(e["pw"]))
                e["pw"] = both[:c]
                e["inv"] = e["inv"] + both[c:]
            n *= 2
        for u in grp:
            st[u]["inv"] = st[u]["inv"] + dot(st[u]["inv"], hat(st[u]["pw"]))

        for u in grp:
            st[u]["u"] = dot(st[u]["inv"], hat(st[u]["rhs_u"]))
        for (d, b, p) in grp:
            e = st[(d, b, p)]
            o_ref = refs[d][3]
            uvh = jnp.concatenate([bf(hat(e["u"])), e["vh"]], axis=0)
            o_ref[b, :, e["cols"]] = (e["rs"] + dot(e["ark"], uvh)).astype(o_ref.dtype)
        for u in grp:
            e = st[u]
            uv = jnp.concatenate([e["u"], e["v"]], axis=0)
            s_ref[e["sidx"]] = e["s0"] * e["eg"] + jnp.where(same_head, dot_tn(uv, e["bk"]), 0.0)


def _rwkv_scan(sh, lw, bk, batch, seq, to_bf16=(), group=16, cast_steps=32):
    _, t, width = sh.shape
    c = SCAN_CHUNK
    nc = seq // c
    sh = sh.reshape(3, batch, seq, width)
    lw = lw.reshape(2, batch, seq, width)
    bk = bk.reshape(2, 2, batch, seq, width)
    sh_spec = lambda f: pl.BlockSpec((3, batch, c, width), lambda i: (0, 0, f(i), 0))
    lw_spec = lambda d, f: pl.BlockSpec((1, batch, c, width), lambda i: (d, 0, f(i), 0))
    bk_spec = lambda d, f: pl.BlockSpec((1, 2, batch, c, width), lambda i: (d, 0, 0, f(i), 0))
    o_spec = lambda f: pl.BlockSpec((batch, c, width), lambda i: (0, f(i), 0))
    fwd = lambda i: i
    bwd = lambda i: nc - 1 - i
    cast_specs = [pl.BlockSpec((a.shape[0] // cast_steps, a.shape[1]),
                               lambda i: (jnp.minimum(i, cast_steps - 1), 0)) for a in to_bf16]
    outs = pl.pallas_call(
        functools.partial(_scan_kernel, group=group, n_cast=len(to_bf16), cast_steps=cast_steps),
        grid=(nc,),
        in_specs=[sh_spec(fwd), sh_spec(bwd), lw_spec(0, fwd), lw_spec(1, bwd),
                  bk_spec(0, fwd), bk_spec(1, bwd)] + cast_specs,
        out_specs=[o_spec(fwd), o_spec(bwd)] + cast_specs,
        out_shape=([jax.ShapeDtypeStruct((batch, seq, width), BF16)] * 2
                   + [jax.ShapeDtypeStruct(a.shape, BF16) for a in to_bf16]),
        scratch_shapes=[pltpu.VMEM((2 * batch * (width // LANES), LANES, LANES), F32)],
        compiler_params=_cparams("arbitrary"),
        name="rwkv_scan",
    )(sh, sh, lw, lw, bk, bk, *to_bf16)
    return (outs[0].reshape(t, width), outs[1].reshape(t, width)) + tuple(outs[2:])


def _rwkv_out_rows(of_ref, ob_ref, bonus_ref, gg_ref, gng_ref, gnb_ref):
    width = of_ref.shape[1]
    avg = _head_block_ones(1.0 / HEAD_DIM)
    tiles = []
    for j in range(width // LANES):
        cols = slice(j * LANES, (j + 1) * LANES)
        o = of_ref[:, cols].astype(F32) + ob_ref[:, cols].astype(F32)
        oc = o - _head_sum(o, avg)
        var = _head_sum(oc * oc, avg)
        y = oc * lax.rsqrt(var + GN_EPS) * gng_ref[:, cols] + gnb_ref[:, cols]
        tiles.append(((y + bonus_ref[:, cols].astype(F32))
                      * gg_ref[:, cols].astype(F32)).astype(BF16))
    return jnp.concatenate(tiles, axis=1)


def _mix_out_kernel(ya_ref, of_ref, ob_ref, bonus_ref, gg_ref, gng_ref, gnb_ref, wa_ref, wb_ref,
                    ga_ref, gb_ref, wo_ref, x_ref, g_ref, x1_ref, h2_ref):
    yb = _rwkv_out_rows(of_ref, ob_ref, bonus_ref, gg_ref, gng_ref, gnb_ref)
    pa = jnp.dot(ya_ref[...], wa_ref[...], preferred_element_type=F32)
    pb = jnp.dot(yb, wb_ref[...], preferred_element_type=F32)
    merged = (ga_ref[...].astype(F32) * pa + gb_ref[...].astype(F32) * pb).astype(BF16)
    x1 = x_ref[...] + jnp.dot(merged, wo_ref[...], preferred_element_type=F32)
    x1_ref[...] = x1
    ms = jnp.mean(x1 * x1, axis=-1, keepdims=True)
    h2_ref[...] = (x1 * lax.rsqrt(ms + NORM_EPS) * g_ref[...]).astype(h2_ref.dtype)


def _mix_out(ya, o_f, o_b, bonus, gg, gn_g, gn_b, wa, wb, gates, wo, x, g, tm=256):
    t, k = ya.shape
    d = x.shape[1]
    resident = lambda shape: pl.BlockSpec(shape, lambda i: (0, 0), pipeline_mode=pl.Buffered(1))
    rows = lambda w: pl.BlockSpec((tm, w), lambda i: (i, 0))
    return pl.pallas_call(
        _mix_out_kernel,
        grid=(t // tm,),
        in_specs=[
            rows(k), rows(k), rows(k), rows(k), rows(k),
            pl.BlockSpec((1, k), lambda i: (0, 0)),
            pl.BlockSpec((1, k), lambda i: (0, 0)),
            resident((k, d)),
            resident((k, d)),
            pl.BlockSpec((tm, d), lambda i: (i, 0)),
            pl.BlockSpec((tm, d), lambda i: (i, 1)),
            resident((d, d)),
            rows(d),
            pl.BlockSpec((1, d), lambda i: (0, 0)),
        ],
        out_specs=[rows(d), rows(d)],
        out_shape=[jax.ShapeDtypeStruct((t, d), F32), jax.ShapeDtypeStruct((t, d), BF16)],
        compiler_params=_cparams("parallel"),
        name="mix_out",
    )(ya, o_f, o_b, bonus, gg, gn_g, gn_b, wa, wb, gates, gates, wo, x, g)


def _ffn1_kernel(h_ref, hb_ref, ha_ref, wg_ref, wu_ref, cw_ref, cb_ref, a_ref, *, seq):
    tm = h_ref.shape[0]
    halo = hb_ref.shape[0]
    i = pl.program_id(0)
    tiles_per_seq = seq // tm
    first = (i % tiles_per_seq) == 0
    last = (i % tiles_per_seq) == tiles_per_seq - 1

    h = h_ref[...]
    zero = jnp.zeros_like(hb_ref[...])
    h_ext = jnp.concatenate([jnp.where(first, zero, hb_ref[...]), h,
                             jnp.where(last, zero, ha_ref[...])], axis=0)
    g_ext = jnp.dot(h_ext, wg_ref[...].astype(BF16), preferred_element_type=F32)
    up = jnp.dot(h, wu_ref[...].astype(BF16), preferred_element_type=F32)
    rows = tm + 2 * halo
    g = g_ext[halo:halo + tm]
    prev = pltpu.roll(g_ext, 1, 0)[halo:halo + tm]
    nxt = pltpu.roll(g_ext, rows - 1, 0)[halo:halo + tm]
    gt = prev * cw_ref[0:1, :] + g * cw_ref[1:2, :] + nxt * cw_ref[2:3, :] + cb_ref[...]
    a_ref[...] = (jax.nn.silu(gt) * up).astype(a_ref.dtype)


def _ffn1(h, wg, wu, cw, cb, seq, tm=1024, tn=512, halo=16):
    t, d = h.shape
    f = wg.shape[1]
    nbh = t // halo
    return pl.pallas_call(
        functools.partial(_ffn1_kernel, seq=seq),
        grid=(t // tm, f // tn),
        in_specs=[
            pl.BlockSpec((tm, d), lambda i, j: (i, 0)),
            pl.BlockSpec((halo, d), lambda i, j: (jnp.maximum(i * (tm // halo) - 1, 0), 0)),
            pl.BlockSpec((halo, d), lambda i, j: (jnp.minimum((i + 1) * (tm // halo), nbh - 1), 0)),
            pl.BlockSpec((d, tn), lambda i, j: (0, j)),
            pl.BlockSpec((d, tn), lambda i, j: (0, j)),
            pl.BlockSpec((cw.shape[0], tn), lambda i, j: (0, j)),
            pl.BlockSpec((1, tn), lambda i, j: (0, j)),
        ],
        out_specs=pl.BlockSpec((tm, tn), lambda i, j: (i, j)),
        out_shape=jax.ShapeDtypeStruct((t, f), BF16),
        compiler_params=_cparams("parallel", "arbitrary"),
        name="ffn_gate_up",
    )(h, h, h, wg, wu, cw, cb)


def _ffn2_kernel(a_ref, wd_ref, x1_ref, nf_ref, o_ref):
    xf = x1_ref[...] + jnp.dot(a_ref[...], wd_ref[...], preferred_element_type=F32)
    ms = jnp.mean(xf * xf, axis=-1, keepdims=True)
    o_ref[...] = xf * lax.rsqrt(ms + NORM_EPS) * nf_ref[...]


def _ffn2(act, wd, x1, nf, tm=256):
    t, f = act.shape
    d = x1.shape[1]
    return pl.pallas_call(
        _ffn2_kernel,
        grid=(t // tm,),
        in_specs=[
            pl.BlockSpec((tm, f), lambda i: (i, 0)),
            pl.BlockSpec((f, d), lambda i: (0, 0), pipeline_mode=pl.Buffered(1)),
            pl.BlockSpec((tm, d), lambda i: (i, 0)),
            pl.BlockSpec((1, d), lambda i: (0, 0)),
        ],
        out_specs=pl.BlockSpec((tm, d), lambda i: (i, 0)),
        out_shape=jax.ShapeDtypeStruct((t, d), F32),
        compiler_params=_cparams("parallel"),
        name="ffn_down",
    )(act, wd, x1, nf)


def _pad_cols(a, n):
    return jnp.pad(a, ((0, 0), (0, n - a.shape[1])))


def _pad_rows(a, n):
    return jnp.pad(a, ((0, n - a.shape[0]), (0, 0)))


def _layer(x, batch, seq, norm1_g, w_in, sgu_ln_g, sgu_ln_b, sgu_w, sgu_b,
           mu_prev, mu_next, w0_f, w2_f, a0_f, a2_f, w0_b, w2_b, a0_b, a2_b, k_k, k_a, r_k,
           g2, gn_g, gn_b, w_proj_a, w_proj_b, w_out, norm2_g, ffn_w_gate, ffn_w_up,
           ffn_conv_w, ffn_conv_b, ffn_w_down, norm_out_g):
    d = x.shape[1]
    sgu_width = sgu_ln_g.shape[0]
    width = k_k.shape[0]
    dl = w2_f.shape[0]
    al = a2_f.shape[0]
    feat_w = mu_prev.shape[0]
    feat_pad = -(-feat_w // (4 * LANES)) * (4 * LANES)
    assert dl == HEAD_DIM and al == HEAD_DIM and 2 * dl == LANES
    row = lambda a: a.reshape(1, -1)

    o1 = 2 * sgu_width
    o2 = o1 + feat_w
    w_t = jnp.swapaxes(w_in, 0, 1)
    zw = jnp.zeros_like(w2_f)
    w2c = jnp.concatenate([jnp.concatenate([w2_f, zw], 1),
                           jnp.concatenate([zw, w2_b], 1)], 0).astype(BF16)
    za = jnp.zeros_like(a2_f)
    a2c = jnp.concatenate([jnp.concatenate([a2_f, za], 1),
                           jnp.concatenate([za, a2_b], 1)], 0).astype(BF16)
    g2p = _pad_rows(g2, feat_pad - 3 * width - 2 * LANES).astype(BF16)
    w0c = row(jnp.concatenate([w0_f, w0_b]))
    a0c = row(jnp.concatenate([a0_f, a0_b]))
    mup = _pad_cols(row(mu_prev), feat_pad)
    mun = _pad_cols(row(mu_next), feat_pad)
    bs_full = jnp.repeat(sgu_b.T, sgu_width // sgu_b.shape[0], axis=1)

    h1 = _rmsnorm(x, row(norm1_g))
    uv, xs, w_gates = _in_proj(h1, w_t, mup, mun, o1, feat_pad, o2, seq)

    y_a = _sgu(uv, row(sgu_ln_g), row(sgu_ln_b), sgu_w.astype(BF16), bs_full)

    sh, lw, bk, bonus, gg, gates = _rwkv_prep(xs, h1, w_gates, width, w0c, w2c, a0c, a2c, g2p,
                                              row(k_k), row(k_a), row(r_k))
    o_f, o_b, wa_bf, wb_bf, wo_bf, wd_bf = _rwkv_scan(
        sh, lw, bk, batch, seq, to_bf16=(w_proj_a, w_proj_b, w_out, ffn_w_down))

    x1, h2 = _mix_out(y_a, o_f, o_b, bonus, gg, row(gn_g), row(gn_b), wa_bf, wb_bf, gates, wo_bf,
                      x, row(norm2_g))

    act = _ffn1(h2, ffn_w_gate, ffn_w_up, ffn_conv_w, row(ffn_conv_b), seq)
    return _ffn2(act, wd_bf, x1, norm_out_g)


def kernel(x, norm1_g, w_in, sgu_ln_g, sgu_ln_b, sgu_w, sgu_b, rwkv_mu_prev, rwkv_mu_next, rwkv_w0_f, rwkv_w2_f, rwkv_a0_f, rwkv_a2_f, rwkv_w0_b, rwkv_w2_b, rwkv_a0_b, rwkv_a2_b, rwkv_k_k, rwkv_k_a, rwkv_r_k, rwkv_g2, rwkv_gn_g, rwkv_gn_b, w_proj_a, w_proj_b, w_out, norm2_g, ffn_w_gate, ffn_w_up, ffn_conv_w, ffn_conv_b, ffn_w_down, norm_f_g):
    batch, seq, d = x.shape
    depth = norm1_g.shape[0]
    assert depth == 1, "the fused final RMSNorm assumes a single layer"
    xf = x.reshape(batch * seq, d)
    per_layer = (norm1_g, w_in, sgu_ln_g, sgu_ln_b, sgu_w, sgu_b, rwkv_mu_prev, rwkv_mu_next,
                 rwkv_w0_f, rwkv_w2_f, rwkv_a0_f, rwkv_a2_f, rwkv_w0_b, rwkv_w2_b, rwkv_a0_b,
                 rwkv_a2_b, rwkv_k_k, rwkv_k_a, rwkv_r_k, rwkv_g2, rwkv_gn_g, rwkv_gn_b,
                 w_proj_a, w_proj_b, w_out, norm2_g, ffn_w_gate, ffn_w_up, ffn_conv_w,
                 ffn_conv_b, ffn_w_down)
    out = _layer(xf, batch, seq, *(p[0] for p in per_layer), norm_f_g.reshape(1, d))
    return out.reshape(batch, seq, d)
```

```python
import functools

import jax
import jax.numpy as jnp
from jax import lax
from jax.experimental import pallas as pl
from jax.experimental.pallas import tpu as pltpu

F32 = jnp.float32
BF16 = jnp.bfloat16

NORM_EPS = 1e-6
LN_EPS = 1e-5
GN_EPS = 64e-5

HEAD_DIM = 64
LANES = 128
SGU_CHUNK = 128
SCAN_CHUNK = 64
VMEM_LIMIT = 48 * 1024 * 1024


def _cparams(*sem):
    return pltpu.CompilerParams(dimension_semantics=sem, vmem_limit_bytes=VMEM_LIMIT)


def _rmsnorm_kernel(x_ref, g_ref, h_ref):
    x = x_ref[...]
    ms = jnp.mean(x * x, axis=-1, keepdims=True)
    h_ref[...] = (x * lax.rsqrt(ms + NORM_EPS) * g_ref[...]).astype(h_ref.dtype)


def _rmsnorm(x, g, tm=1024):
    t, d = x.shape
    return pl.pallas_call(
        _rmsnorm_kernel,
        grid=(t // tm,),
        in_specs=[pl.BlockSpec((tm, d), lambda i: (i, 0)), pl.BlockSpec((1, d), lambda i: (0, 0))],
        out_specs=pl.BlockSpec((tm, d), lambda i: (i, 0)),
        out_shape=jax.ShapeDtypeStruct((t, d), BF16),
        compiler_params=_cparams("parallel"),
        name="rmsnorm",
    )(x, g)


def _in_proj_kernel(h_ref, hb_ref, ha_ref, w_ref, mup_ref, mun_ref, ga0_ref, ga1_ref, gb0_ref,
                    gb1_ref, xs_ref, wg_ref, hx_ref, *, seq, cast_steps):
    tm = h_ref.shape[0]
    halo = hb_ref.shape[0]
    cr = ga0_ref.shape[0]
    i = pl.program_id(0)
    j = pl.program_id(1)
    tiles_per_seq = seq // tm
    first = (i % tiles_per_seq) == 0
    last = (i % tiles_per_seq) == tiles_per_seq - 1

    @pl.when(i * pl.num_programs(1) + j < cast_steps)
    def _():
        wg_ref[0, 0:cr, :] = ga0_ref[...].astype(wg_ref.dtype)
        wg_ref[0, cr:, :] = ga1_ref[...].astype(wg_ref.dtype)
        wg_ref[1, 0:cr, :] = gb0_ref[...].astype(wg_ref.dtype)
        wg_ref[1, cr:, :] = gb1_ref[...].astype(wg_ref.dtype)

    @pl.when(j == 0)
    def _():
        zero = jnp.zeros_like(hb_ref[...])
        hx_ref[0:halo, :] = jnp.where(first, zero, hb_ref[...])
        hx_ref[halo:halo + tm, :] = h_ref[...]
        hx_ref[halo + tm:, :] = jnp.where(last, zero, ha_ref[...])

    f_ext = lax.dot_general(hx_ref[...], w_ref[...].astype(BF16), (((1,), (1,)), ((), ())),
                            preferred_element_type=F32)
    rows = tm + 2 * halo
    f = f_ext[halo:halo + tm]
    prev = pltpu.roll(f_ext, 1, 0)[halo:halo + tm]
    nxt = pltpu.roll(f_ext, rows - 1, 0)[halo:halo + tm]
    mup = mup_ref[...]
    mun = mun_ref[...]
    xs_ref[...] = (f * (1.0 - mup - mun) + mup * prev + mun * nxt).astype(xs_ref.dtype)


def _in_proj(h, w_t, mup, mun, row0, w_ft, gate_row0, seq, tm=1024, tn=512, halo=16,
             cast_rows=32):
    t, d = h.shape
    n_ft = w_ft // tn
    nbh = t // halo
    half = (w_t.shape[0] - gate_row0) // 2
    cast_steps = half // (2 * cast_rows)
    assert row0 % tn == 0 and gate_row0 % cast_rows == 0 and half % (2 * cast_rows) == 0
    assert cast_steps <= (t // tm) * n_ft
    cast = lambda i, j: jnp.minimum(i * n_ft + j, cast_steps - 1)
    gate_rows = lambda r0, k: pl.BlockSpec(
        (cast_rows, d), lambda i, j: (r0 // cast_rows + 2 * cast(i, j) + k, 0))
    return pl.pallas_call(
        functools.partial(_in_proj_kernel, seq=seq, cast_steps=cast_steps),
        grid=(t // tm, n_ft),
        in_specs=[
            pl.BlockSpec((tm, d), lambda i, j: (i, 0), pipeline_mode=pl.Buffered(1)),
            pl.BlockSpec((halo, d), lambda i, j: (jnp.maximum(i * (tm // halo) - 1, 0), 0)),
            pl.BlockSpec((halo, d), lambda i, j: (jnp.minimum((i + 1) * (tm // halo), nbh - 1), 0)),
            pl.BlockSpec((tn, d), lambda i, j: (row0 // tn + j, 0)),
            pl.BlockSpec((1, tn), lambda i, j: (0, j)),
            pl.BlockSpec((1, tn), lambda i, j: (0, j)),
            gate_rows(gate_row0, 0), gate_rows(gate_row0, 1),
            gate_rows(gate_row0 + half, 0), gate_rows(gate_row0 + half, 1),
        ],
        out_specs=[
            pl.BlockSpec((tm, tn), lambda i, j: (i, j)),
            pl.BlockSpec((2, 2 * cast_rows, d), lambda i, j: (0, cast(i, j), 0)),
        ],
        out_shape=[
            jax.ShapeDtypeStruct((t, n_ft * tn), BF16),
            jax.ShapeDtypeStruct((2, half, d), BF16),
        ],
        scratch_shapes=[pltpu.VMEM((tm + 2 * halo, d), BF16)],
        compiler_params=_cparams("arbitrary", "arbitrary"),
        name="in_proj",
    )(h, h, h, w_t, mup, mun, w_t, w_t, w_t, w_t)


def _gelu(x):
    return 0.5 * x * (1.0 + lax.erf(x * (2.0 ** -0.5)))


def _sgu_kernel(h_ref, w_ref, lng_ref, lnb_ref, ws_ref, bs_ref, o_ref, wb_ref, *, width, groups):
    @pl.when(pl.program_id(0) == 0)
    def _():
        wb_ref[...] = w_ref[...].astype(wb_ref.dtype)

    tm = h_ref.shape[0]
    uv = lax.dot_general(h_ref[...], wb_ref[...], (((1,), (1,)), ((), ())),
                         preferred_element_type=F32)
    gv = _gelu(uv[:, width:])
    mu = jnp.mean(gv, axis=-1, keepdims=True)
    vc = gv - mu
    var = jnp.mean(vc * vc, axis=-1, keepdims=True)
    vn = (vc * lax.rsqrt(var + LN_EPS) * lng_ref[...] + lnb_ref[...]).astype(BF16)
    gd = width // groups
    for c in range(tm // SGU_CHUNK):
        rows = slice(c * SGU_CHUNK, (c + 1) * SGU_CHUNK)
        for g in range(groups):
            cols = slice(g * gd, (g + 1) * gd)
            mixed = jnp.dot(ws_ref[g], vn[rows, cols], preferred_element_type=F32)
            gu = _gelu(uv[rows, cols])
            o_ref[rows, cols] = (gu * (mixed + bs_ref[:, cols])).astype(o_ref.dtype)


def _sgu(h, w_t, ln_g, ln_b, ws, bs_full, tm=512):
    t, d = h.shape
    width = ln_g.shape[1]
    groups = ws.shape[0]
    return pl.pallas_call(
        functools.partial(_sgu_kernel, width=width, groups=groups),
        grid=(t // tm,),
        in_specs=[
            pl.BlockSpec((tm, d), lambda i: (i, 0)),
            pl.BlockSpec((2 * width, d), lambda i: (0, 0), pipeline_mode=pl.Buffered(1)),
            pl.BlockSpec((1, width), lambda i: (0, 0)),
            pl.BlockSpec((1, width), lambda i: (0, 0)),
            pl.BlockSpec(ws.shape, lambda i: (0, 0, 0)),
            pl.BlockSpec(bs_full.shape, lambda i: (0, 0)),
        ],
        out_specs=pl.BlockSpec((tm, width), lambda i: (i, 0)),
        out_shape=jax.ShapeDtypeStruct((t, width), BF16),
        scratch_shapes=[pltpu.VMEM((2 * width, d), BF16)],
        compiler_params=_cparams("arbitrary"),
        name="sgu_mixer",
    )(h, w_t, ln_g, ln_b, ws, bs_full)


def _head_block_ones(scale):
    r = lax.broadcasted_iota(jnp.int32, (LANES, LANES), 0) // HEAD_DIM
    c = lax.broadcasted_iota(jnp.int32, (LANES, LANES), 1) // HEAD_DIM
    return jnp.where(r == c, scale, 0.0).astype(BF16)


def _head_sum(x, bd):
    return jnp.dot(x.astype(BF16), bd, preferred_element_type=F32)


def _prep_kernel(xs_ref, w0_ref, w2_ref, a0_ref, a2_ref, g2_ref, kk_ref, ka_ref, rk_ref,
                 h_ref, wg_ref, sh_ref, lw_ref, bk_ref, bonus_ref, gg_ref, gt_ref, *, width):
    gt_ref[...] = jax.nn.sigmoid(
        lax.dot_general(h_ref[...], wg_ref[...], (((1,), (1,)), ((), ())),
                        preferred_element_type=F32)).astype(gt_ref.dtype)

    xs = xs_ref[...].astype(F32)
    r = xs[:, 0:width]
    k = xs[:, width:2 * width]
    v = xs[:, 2 * width:3 * width]
    o = 3 * width
    lo_w = xs[:, o:o + LANES]
    lo_a = xs[:, o + LANES:o + 2 * LANES]
    lo_g = xs[:, o + 2 * LANES:]

    wpre = w0_ref[...] + jnp.dot(jnp.tanh(lo_w).astype(BF16), w2_ref[...],
                                 preferred_element_type=F32)
    lw = -(2.718281828459045 ** -0.5) * jax.nn.sigmoid(wpre)
    a = jax.nn.sigmoid(a0_ref[...] + jnp.dot(lo_a.astype(BF16), a2_ref[...],
                                             preferred_element_type=F32))
    gg_ref[...] = jnp.dot(jax.nn.sigmoid(lo_g).astype(BF16), g2_ref[...],
                          preferred_element_type=F32).astype(gg_ref.dtype)
    lw_ref[0] = lw[:, :width]
    lw_ref[1] = lw[:, width:]

    bd = _head_block_ones(1.0)
    kk = k * kk_ref[...]
    sh_ref[0] = r.astype(sh_ref.dtype)
    sh_ref[1] = v.astype(sh_ref.dtype)
    for j in range(width // LANES):
        cols = slice(j * LANES, (j + 1) * LANES)
        kkj = kk[:, cols]
        ss = _head_sum(kkj * kkj, bd)
        kkn = kkj * lax.rsqrt(jnp.maximum(ss, 1e-24))
        sh_ref[2, :, cols] = kkn.astype(sh_ref.dtype)
        ksum = jnp.zeros_like(kkj)
        for d in range(2):
            dcols = slice(d * width + j * LANES, d * width + (j + 1) * LANES)
            a_d = a[:, dcols]
            k_d = k[:, cols] * (1.0 + (a_d - 1.0) * ka_ref[:, cols])
            bk_ref[d, 0, :, cols] = (kkn * a_d).astype(bk_ref.dtype)
            bk_ref[d, 1, :, cols] = k_d.astype(bk_ref.dtype)
            ksum = ksum + k_d
        bsum = _head_sum(r[:, cols] * ksum * rk_ref[:, cols], bd)
        bonus_ref[:, cols] = (bsum * v[:, cols]).astype(bonus_ref.dtype)


def _rwkv_prep(xs, h, wg, width, w0c, w2c, a0c, a2c, g2p, k_k, k_a, r_k, tm=128):
    t, fw = xs.shape
    _, ngh, d = wg.shape
    r = lambda j, i: 2 * i + j
    row = lambda j, i: (0, 0)
    resident = lambda a: pl.BlockSpec(a.shape, row, pipeline_mode=pl.Buffered(1))
    rows = lambda w: pl.BlockSpec((tm, w), lambda j, i: (r(j, i), 0))
    return pl.pallas_call(
        functools.partial(_prep_kernel, width=width),
        grid=(2, t // (2 * tm)),
        in_specs=[
            rows(fw),
            pl.BlockSpec(w0c.shape, row),
            resident(w2c),
            pl.BlockSpec(a0c.shape, row),
            resident(a2c),
            resident(g2p),
            pl.BlockSpec((1, width), row),
            pl.BlockSpec((1, width), row),
            pl.BlockSpec((1, width), row),
            pl.BlockSpec((2 * tm, d), lambda j, i: (i, 0)),
            pl.BlockSpec((None, ngh, d), lambda j, i: (j, 0, 0), pipeline_mode=pl.Buffered(1)),
        ],
        out_specs=[
            pl.BlockSpec((3, tm, width), lambda j, i: (0, r(j, i), 0)),
            pl.BlockSpec((2, tm, width), lambda j, i: (0, r(j, i), 0)),
            pl.BlockSpec((2, 2, tm, width), lambda j, i: (0, 0, r(j, i), 0)),
            rows(width),
            rows(width),
            pl.BlockSpec((2 * tm, ngh), lambda j, i: (i, j)),
        ],
        out_shape=[
            jax.ShapeDtypeStruct((3, t, width), BF16),
            jax.ShapeDtypeStruct((2, t, width), F32),
            jax.ShapeDtypeStruct((2, 2, t, width), BF16),
            jax.ShapeDtypeStruct((t, width), BF16),
            jax.ShapeDtypeStruct((t, width), BF16),
            jax.ShapeDtypeStruct((t, 2 * ngh), BF16),
        ],
        compiler_params=_cparams("arbitrary", "arbitrary"),
        name="rwkv_prep",
    )(xs, w0c, w2c, a0c, a2c, g2p, k_k, k_a, r_k, h, wg)


def _hat(x, first_half):
    return jnp.concatenate([jnp.where(first_half, x, 0.0), jnp.where(first_half, 0.0, x)],
                           axis=0)


def _scan_kernel(shf_ref, shb_ref, lwf_ref, lwb_ref, bkf_ref, bkb_ref, *rest, group, n_cast,
                 cast_steps):
    cast_in, (of_ref, ob_ref), cast_out, s_ref = (rest[:n_cast], rest[n_cast:n_cast + 2],
                                                  rest[n_cast + 2:2 * n_cast + 2], rest[-1])
    c = SCAN_CHUNK
    batch, width = shf_ref.shape[1], shf_ref.shape[3]
    npair = width // LANES
    assert 2 * c == LANES and c == HEAD_DIM

    @pl.when(pl.program_id(0) == 0)
    def _():
        s_ref[...] = jnp.zeros_like(s_ref)

    @pl.when(pl.program_id(0) < cast_steps)
    def _():
        for src, dst in zip(cast_in, cast_out):
            dst[...] = src[...].astype(dst.dtype)

    bf = lambda t: t.astype(BF16)
    f32 = lambda t: t.astype(F32)
    dg = lambda p, q, dims: lax.dot_general(bf(p), bf(q), (dims, ((), ())),
                                            preferred_element_type=F32)
    dot = lambda p, q: dg(p, q, ((1,), (0,)))
    dot_nt = lambda p, q: dg(p, q, ((1,), (1,)))
    dot_tn = lambda p, q: dg(p, q, ((0,), (0,)))

    wt = lax.broadcasted_iota(jnp.int32, (c, LANES), 0)
    ws = lax.broadcasted_iota(jnp.int32, (c, LANES), 1) % c
    masks = ((wt > ws, wt >= ws), (wt < ws, wt <= ws))
    eye = jnp.where(wt == ws, 1.0, 0.0).astype(F32)
    half = lax.broadcasted_iota(jnp.int32, (c, LANES), 1) < c
    same_head = (lax.broadcasted_iota(jnp.int32, (LANES, LANES), 0) // HEAD_DIM
                 == lax.broadcasted_iota(jnp.int32, (LANES, LANES), 1) // HEAD_DIM)
    hat = lambda t: _hat(t, half)
    refs = ((shf_ref, lwf_ref, bkf_ref, of_ref), (shb_ref, lwb_ref, bkb_ref, ob_ref))

    units = [(d, b, p) for d in range(2) for b in range(batch) for p in range(npair)]
    for g0 in range(0, len(units), group):
        grp = units[g0:g0 + group]
        st = {}
        for (d, b, p) in grp:
            sh_ref, lw_ref, bk_ref, _ = refs[d]
            strict, incl = masks[d]
            cols = slice(p * LANES, (p + 1) * LANES)
            r = f32(sh_ref[0, b, :, cols])
            v = f32(sh_ref[1, b, :, cols])
            kk = f32(sh_ref[2, b, :, cols])
            lw = lw_ref[0, b, :, cols]
            bb = f32(bk_ref[0, 0, b, :, cols])
            kd = f32(bk_ref[0, 1, b, :, cols])
            pre = lw
            s = 1
            while s < c:
                pre = pre + jnp.where(wt >= s, pltpu.roll(pre, s, 0), 0.0)
                s *= 2
            gtot = pre[c - 1:c, :]
            gi = pre if d == 0 else gtot - pre + lw
            en = jnp.exp(-gi)
            ec = jnp.exp(gtot - gi)
            lhs = bf(jnp.concatenate([-kk * jnp.exp(gi - lw), r * jnp.exp(gi)], axis=0))
            rhs = jnp.concatenate([hat(bb * en), hat(kd * en)], axis=0)
            aa = dot_nt(lhs, rhs)
            a_ab = jnp.where(strict, aa[:c, :LANES], 0.0)
            a_ak = jnp.where(strict, aa[:c, LANES:], 0.0)
            ark = bf(jnp.concatenate([jnp.where(incl, aa[c:, :LANES], 0.0),
                                      jnp.where(incl, aa[c:, LANES:], 0.0)], axis=1))
            vh = bf(hat(v))
            sidx = (d * batch + b) * npair + p
            s0 = s_ref[sidx]
            ls = dot_nt(lhs, s0)
            st[(d, b, p)] = dict(
                sidx=sidx, cols=cols, v=v, s0=s0, ark=ark, vh=vh,
                bk=bf(jnp.concatenate([bb * ec, kd * ec], axis=0)),
                eg=jnp.exp(gtot), pw=a_ab, inv=eye + a_ab,
                rhs_u=ls[:c] + dot(a_ak, vh), rs=ls[c:])

        for u in grp:
            st[u]["pw"] = dot(st[u]["pw"], hat(st[u]["pw"]))
        n = 4
        while n < c:
            for u in grp:
                e = st[u]
                both = dot(jnp.concatenate([e["pw"], e["inv"]], axis=0), hat(e["pw"]))
                e["pw"] = both[:c]
                e["inv"] = e["inv"] + both[c:]
            n *= 2
        for u in grp:
            st[u]["inv"] = st[u]["inv"] + dot(st[u]["inv"], hat(st[u]["pw"]))

        for u in grp:
            st[u]["u"] = dot(st[u]["inv"], hat(st[u]["rhs_u"]))
        for (d, b, p) in grp:
            e = st[(d, b, p)]
            o_ref = refs[d][3]
            uvh = jnp.concatenate([bf(hat(e["u"])), e["vh"]], axis=0)
            o_ref[b, :, e["cols"]] = (e["rs"] + dot(e["ark"], uvh)).astype(o_ref.dtype)
        for u in grp:
            e = st[u]
            uv = jnp.concatenate([e["u"], e["v"]], axis=0)
            s_ref[e["sidx"]] = e["s0"] * e["eg"] + jnp.where(same_head, dot_tn(uv, e["bk"]), 0.0)


def _rwkv_scan(sh, lw, bk, batch, seq, to_bf16=(), group=16, cast_steps=32):
    _, t, width = sh.shape
    c = SCAN_CHUNK
    nc = seq // c
    sh = sh.reshape(3, batch, seq, width)
    lw = lw.reshape(2, batch, seq, width)
    bk = bk.reshape(2, 2, batch, seq, width)
    sh_spec = lambda f: pl.BlockSpec((3, batch, c, width), lambda i: (0, 0, f(i), 0))
    lw_spec = lambda d, f: pl.BlockSpec((1, batch, c, width), lambda i: (d, 0, f(i), 0))
    bk_spec = lambda d, f: pl.BlockSpec((1, 2, batch, c, width), lambda i: (d, 0, 0, f(i), 0))
    o_spec = lambda f: pl.BlockSpec((batch, c, width), lambda i: (0, f(i), 0))
    fwd = lambda i: i
    bwd = lambda i: nc - 1 - i
    cast_specs = [pl.BlockSpec((a.shape[0] // cast_steps, a.shape[1]),
                               lambda i: (jnp.minimum(i, cast_steps - 1), 0)) for a in to_bf16]
    outs = pl.pallas_call(
        functools.partial(_scan_kernel, group=group, n_cast=len(to_bf16), cast_steps=cast_steps),
        grid=(nc,),
        in_specs=[sh_spec(fwd), sh_spec(bwd), lw_spec(0, fwd), lw_spec(1, bwd),
                  bk_spec(0, fwd), bk_spec(1, bwd)] + cast_specs,
        out_specs=[o_spec(fwd), o_spec(bwd)] + cast_specs,
        out_shape=([jax.ShapeDtypeStruct((batch, seq, width), BF16)] * 2
                   + [jax.ShapeDtypeStruct(a.shape, BF16) for a in to_bf16]),
        scratch_shapes=[pltpu.VMEM((2 * batch * (width // LANES), LANES, LANES), F32)],
        compiler_params=_cparams("arbitrary"),
        name="rwkv_scan",
    )(sh, sh, lw, lw, bk, bk, *to_bf16)
    return (outs[0].reshape(t, width), outs[1].reshape(t, width)) + tuple(outs[2:])


def _rwkv_out_rows(of_ref, ob_ref, bonus_ref, gg_ref, gng_ref, gnb_ref):
    width = of_ref.shape[1]
    avg = _head_block_ones(1.0 / HEAD_DIM)
    tiles = []
    for j in range(width // LANES):
        cols = slice(j * LANES, (j + 1) * LANES)
        o = of_ref[:, cols].astype(F32) + ob_ref[:, cols].astype(F32)
        oc = o - _head_sum(o, avg)
        var = _head_sum(oc * oc, avg)
        y = oc * lax.rsqrt(var + GN_EPS) * gng_ref[:, cols] + gnb_ref[:, cols]
        tiles.append(((y + bonus_ref[:, cols].astype(F32))
                      * gg_ref[:, cols].astype(F32)).astype(BF16))
    return jnp.concatenate(tiles, axis=1)


def _mix_out_kernel(ya_ref, of_ref, ob_ref, bonus_ref, gg_ref, gng_ref, gnb_ref, wa_ref, wb_ref,
                    ga_ref, gb_ref, wo_ref, x_ref, g_ref, x1_ref, h2_ref):
    yb = _rwkv_out_rows(of_ref, ob_ref, bonus_ref, gg_ref, gng_ref, gnb_ref)
    pa = jnp.dot(ya_ref[...], wa_ref[...], preferred_element_type=F32)
    pb = jnp.dot(yb, wb_ref[...], preferred_element_type=F32)
    merged = (ga_ref[...].astype(F32) * pa + gb_ref[...].astype(F32) * pb).astype(BF16)
    x1 = x_ref[...] + jnp.dot(merged, wo_ref[...], preferred_element_type=F32)
    x1_ref[...] = x1
    ms = jnp.mean(x1 * x1, axis=-1, keepdims=True)
    h2_ref[...] = (x1 * lax.rsqrt(ms + NORM_EPS) * g_ref[...]).astype(h2_ref.dtype)


def _mix_out(ya, o_f, o_b, bonus, gg, gn_g, gn_b, wa, wb, gates, wo, x, g, tm=256):
    t, k = ya.shape
    d = x.shape[1]
    resident = lambda shape: pl.BlockSpec(shape, lambda i: (0, 0), pipeline_mode=pl.Buffered(1))
    rows = lambda w: pl.BlockSpec((tm, w), lambda i: (i, 0))
    return pl.pallas_call(
        _mix_out_kernel,
        grid=(t // tm,),
        in_specs=[
            rows(k), rows(k), rows(k), rows(k), rows(k),
            pl.BlockSpec((1, k), lambda i: (0, 0)),
            pl.BlockSpec((1, k), lambda i: (0, 0)),
            resident((k, d)),
            resident((k, d)),
            pl.BlockSpec((tm, d), lambda i: (i, 0)),
            pl.BlockSpec((tm, d), lambda i: (i, 1)),
            resident((d, d)),
            rows(d),
            pl.BlockSpec((1, d), lambda i: (0, 0)),
        ],
        out_specs=[rows(d), rows(d)],
        out_shape=[jax.ShapeDtypeStruct((t, d), F32), jax.ShapeDtypeStruct((t, d), BF16)],
        compiler_params=_cparams("parallel"),
        name="mix_out",
    )(ya, o_f, o_b, bonus, gg, gn_g, gn_b, wa, wb, gates, gates, wo, x, g)


def _ffn1_kernel(h_ref, hb_ref, ha_ref, wg_ref, wu_ref, cw_ref, cb_ref, a_ref, *, seq):
    tm = h_ref.shape[0]
    halo = hb_ref.shape[0]
    i = pl.program_id(0)
    tiles_per_seq = seq // tm
    first = (i % tiles_per_seq) == 0
    last = (i % tiles_per_seq) == tiles_per_seq - 1

    h = h_ref[...]
    zero = jnp.zeros_like(hb_ref[...])
    h_ext = jnp.concatenate([jnp.where(first, zero, hb_ref[...]), h,
                             jnp.where(last, zero, ha_ref[...])], axis=0)
    g_ext = jnp.dot(h_ext, wg_ref[...].astype(BF16), preferred_element_type=F32)
    up = jnp.dot(h, wu_ref[...].astype(BF16), preferred_element_type=F32)
    rows = tm + 2 * halo
    g = g_ext[halo:halo + tm]
    prev = pltpu.roll(g_ext, 1, 0)[halo:halo + tm]
    nxt = pltpu.roll(g_ext, rows - 1, 0)[halo:halo + tm]
    gt = prev * cw_ref[0:1, :] + g * cw_ref[1:2, :] + nxt * cw_ref[2:3, :] + cb_ref[...]
    a_ref[...] = (jax.nn.silu(gt) * up).astype(a_ref.dtype)


def _ffn1(h, wg, wu, cw, cb, seq, tm=1024, tn=512, halo=16):
    t, d = h.shape
    f = wg.shape[1]
    nbh = t // halo
    return pl.pallas_call(
        functools.partial(_ffn1_kernel, seq=seq),
        grid=(t // tm, f // tn),
        in_specs=[
            pl.BlockSpec((tm, d), lambda i, j: (i, 0)),
            pl.BlockSpec((halo, d), lambda i, j: (jnp.maximum(i * (tm // halo) - 1, 0), 0)),
            pl.BlockSpec((halo, d), lambda i, j: (jnp.minimum((i + 1) * (tm // halo), nbh - 1), 0)),
            pl.BlockSpec((d, tn), lambda i, j: (0, j)),
            pl.BlockSpec((d, tn), lambda i, j: (0, j)),
            pl.BlockSpec((cw.shape[0], tn), lambda i, j: (0, j)),
            pl.BlockSpec((1, tn), lambda i, j: (0, j)),
        ],
        out_specs=pl.BlockSpec((tm, tn), lambda i, j: (i, j)),
        out_shape=jax.ShapeDtypeStruct((t, f), BF16),
        compiler_params=_cparams("parallel", "arbitrary"),
        name="ffn_gate_up",
    )(h, h, h, wg, wu, cw, cb)


def _ffn2_kernel(a_ref, wd_ref, x1_ref, nf_ref, o_ref):
    xf = x1_ref[...] + jnp.dot(a_ref[...], wd_ref[...], preferred_element_type=F32)
    ms = jnp.mean(xf * xf, axis=-1, keepdims=True)
    o_ref[...] = xf * lax.rsqrt(ms + NORM_EPS) * nf_ref[...]


def _ffn2(act, wd, x1, nf, tm=256):
    t, f = act.shape
    d = x1.shape[1]
    return pl.pallas_call(
        _ffn2_kernel,
        grid=(t // tm,),
        in_specs=[
            pl.BlockSpec((tm, f), lambda i: (i, 0)),
            pl.BlockSpec((f, d), lambda i: (0, 0), pipeline_mode=pl.Buffered(1)),
            pl.BlockSpec((tm, d), lambda i: (i, 0)),
            pl.BlockSpec((1, d), lambda i: (0, 0)),
        ],
        out_specs=pl.BlockSpec((tm, d), lambda i: (i, 0)),
        out_shape=jax.ShapeDtypeStruct((t, d), F32),
        compiler_params=_cparams("parallel"),
        name="ffn_down",
    )(act, wd, x1, nf)


def _pad_cols(a, n):
    return jnp.pad(a, ((0, 0), (0, n - a.shape[1])))


def _pad_rows(a, n):
    return jnp.pad(a, ((0, n - a.shape[0]), (0, 0)))


def _layer(x, batch, seq, norm1_g, w_in, sgu_ln_g, sgu_ln_b, sgu_w, sgu_b,
           mu_prev, mu_next, w0_f, w2_f, a0_f, a2_f, w0_b, w2_b, a0_b, a2_b, k_k, k_a, r_k,
           g2, gn_g, gn_b, w_proj_a, w_proj_b, w_out, norm2_g, ffn_w_gate, ffn_w_up,
           ffn_conv_w, ffn_conv_b, ffn_w_down, norm_out_g):
    d = x.shape[1]
    sgu_width = sgu_ln_g.shape[0]
    width = k_k.shape[0]
    dl = w2_f.shape[0]
    al = a2_f.shape[0]
    feat_w = mu_prev.shape[0]
    feat_pad = -(-feat_w // (4 * LANES)) * (4 * LANES)
    assert dl == HEAD_DIM and al == HEAD_DIM and 2 * dl == LANES
    row = lambda a: a.reshape(1, -1)

    o1 = 2 * sgu_width
    o2 = o1 + feat_w
    w_t = jnp.swapaxes(w_in, 0, 1)
    zw = jnp.zeros_like(w2_f)
    w2c = jnp.concatenate([jnp.concatenate([w2_f, zw], 1),
                           jnp.concatenate([zw, w2_b], 1)], 0).astype(BF16)
    za = jnp.zeros_like(a2_f)
    a2c = jnp.concatenate([jnp.concatenate([a2_f, za], 1),
                           jnp.concatenate([za, a2_b], 1)], 0).astype(BF16)
    g2p = _pad_rows(g2, feat_pad - 3 * width - 2 * LANES).astype(BF16)
    w0c = row(jnp.concatenate([w0_f, w0_b]))
    a0c = row(jnp.concatenate([a0_f, a0_b]))
    mup = _pad_cols(row(mu_prev), feat_pad)
    mun = _pad_cols(row(mu_next), feat_pad)
    bs_full = jnp.repeat(sgu_b.T, sgu_width // sgu_b.shape[0], axis=1)

    h1 = _rmsnorm(x, row(norm1_g))
    xs, w_gates = _in_proj(h1, w_t, mup, mun, o1, feat_pad, o2, seq)

    y_a = _sgu(h1, w_t, row(sgu_ln_g), row(sgu_ln_b), sgu_w.astype(BF16), bs_full)

    sh, lw, bk, bonus, gg, gates = _rwkv_prep(xs, h1, w_gates, width, w0c, w2c, a0c, a2c, g2p,
                                              row(k_k), row(k_a), row(r_k))
    o_f, o_b, wa_bf, wb_bf, wo_bf, wd_bf = _rwkv_scan(
        sh, lw, bk, batch, seq, to_bf16=(w_proj_a, w_proj_b, w_out, ffn_w_down))

    x1, h2 = _mix_out(y_a, o_f, o_b, bonus, gg, row(gn_g), row(gn_b), wa_bf, wb_bf, gates, wo_bf,
                      x, row(norm2_g))

    act = _ffn1(h2, ffn_w_gate, ffn_w_up, ffn_conv_w, row(ffn_conv_b), seq)
    return _ffn2(act, wd_bf, x1, norm_out_g)


def kernel(x, norm1_g, w_in, sgu_ln_g, sgu_ln_b, sgu_w, sgu_b, rwkv_mu_prev, rwkv_mu_next, rwkv_w0_f, rwkv_w2_f, rwkv_a0_f, rwkv_a2_f, rwkv_w0_b, rwkv_w2_b, rwkv_a0_b, rwkv_a2_b, rwkv_k_k, rwkv_k_a, rwkv_r_k, rwkv_g2, rwkv_gn_g, rwkv_gn_b, w_proj_a, w_proj_b, w_out, norm2_g, ffn_w_gate, ffn_w_up, ffn_conv_w, ffn_conv_b, ffn_w_down, norm_f_g):
    batch, seq, d = x.shape
    depth = norm1_g.shape[0]
    assert depth == 1, "the fused final RMSNorm assumes a single layer"
    xf = x.reshape(batch * seq, d)
    per_layer = (norm1_g, w_in, sgu_ln_g, sgu_ln_b, sgu_w, sgu_b, rwkv_mu_prev, rwkv_mu_next,
                 rwkv_w0_f, rwkv_w2_f, rwkv_a0_f, rwkv_a2_f, rwkv_w0_b, rwkv_w2_b, rwkv_a0_b,
                 rwkv_a2_b, rwkv_k_k, rwkv_k_a, rwkv_r_k, rwkv_g2, rwkv_gn_g, rwkv_gn_b,
                 w_proj_a, w_proj_b, w_out, norm2_g, ffn_w_gate, ffn_w_up, ffn_conv_w,
                 ffn_conv_b, ffn_w_down)
    out = _layer(xf, batch, seq, *(p[0] for p in per_layer), norm_f_g.reshape(1, d))
    return out.reshape(batch, seq, d)
```

```python
import functools

import jax
import jax.numpy as jnp
from jax import lax
from jax.experimental import pallas as pl
from jax.experimental.pallas import tpu as pltpu

F32 = jnp.float32
BF16 = jnp.bfloat16

NORM_EPS = 1e-6
LN_EPS = 1e-5
GN_EPS = 64e-5

HEAD_DIM = 64
LANES = 128
SGU_CHUNK = 128
SCAN_CHUNK = 64
VMEM_LIMIT = 48 * 1024 * 1024


def _cparams(*sem):
    return pltpu.CompilerParams(dimension_semantics=sem, vmem_limit_bytes=VMEM_LIMIT)


def _rmsnorm_kernel(x_ref, g_ref, h_ref):
    x = x_ref[...]
    ms = jnp.mean(x * x, axis=-1, keepdims=True)
    h_ref[...] = (x * lax.rsqrt(ms + NORM_EPS) * g_ref[...]).astype(h_ref.dtype)


def _rmsnorm(x, g, tm=1024):
    t, d = x.shape
    return pl.pallas_call(
        _rmsnorm_kernel,
        grid=(t // tm,),
        in_specs=[pl.BlockSpec((tm, d), lambda i: (i, 0)), pl.BlockSpec((1, d), lambda i: (0, 0))],
        out_specs=pl.BlockSpec((tm, d), lambda i: (i, 0)),
        out_shape=jax.ShapeDtypeStruct((t, d), BF16),
        compiler_params=_cparams("parallel"),
        name="rmsnorm",
    )(x, g)


def _in_proj_kernel(h_ref, hb_ref, ha_ref, w_ref, mup_ref, mun_ref, ga0_ref, ga1_ref, gb0_ref,
                    gb1_ref, xs_ref, wg_ref, hx_ref, *, seq, cast_steps):
    tm = h_ref.shape[0]
    halo = hb_ref.shape[0]
    cr = ga0_ref.shape[0]
    i = pl.program_id(0)
    j = pl.program_id(1)
    tiles_per_seq = seq // tm
    first = (i % tiles_per_seq) == 0
    last = (i % tiles_per_seq) == tiles_per_seq - 1

    @pl.when(i * pl.num_programs(1) + j < cast_steps)
    def _():
        wg_ref[0, 0:cr, :] = ga0_ref[...].astype(wg_ref.dtype)
        wg_ref[0, cr:, :] = ga1_ref[...].astype(wg_ref.dtype)
        wg_ref[1, 0:cr, :] = gb0_ref[...].astype(wg_ref.dtype)
        wg_ref[1, cr:, :] = gb1_ref[...].astype(wg_ref.dtype)

    @pl.when(j == 0)
    def _():
        zero = jnp.zeros_like(hb_ref[...])
        hx_ref[0:halo, :] = jnp.where(first, zero, hb_ref[...])
        hx_ref[halo:halo + tm, :] = h_ref[...]
        hx_ref[halo + tm:, :] = jnp.where(last, zero, ha_ref[...])

    f_ext = lax.dot_general(hx_ref[...], w_ref[...].astype(BF16), (((1,), (1,)), ((), ())),
                            preferred_element_type=F32)
    rows = tm + 2 * halo
    f = f_ext[halo:halo + tm]
    prev = pltpu.roll(f_ext, 1, 0)[halo:halo + tm]
    nxt = pltpu.roll(f_ext, rows - 1, 0)[halo:halo + tm]
    mup = mup_ref[...]
    mun = mun_ref[...]
    xs_ref[...] = (f * (1.0 - mup - mun) + mup * prev + mun * nxt).astype(xs_ref.dtype)


def _in_proj(h, w_t, mup, mun, row0, w_ft, gate_row0, seq, tm=1024, tn=512, halo=16,
             cast_rows=32):
    t, d = h.shape
    n_ft = w_ft // tn
    nbh = t // halo
    half = (w_t.shape[0] - gate_row0) // 2
    cast_steps = half // (2 * cast_rows)
    assert row0 % tn == 0 and gate_row0 % cast_rows == 0 and half % (2 * cast_rows) == 0
    assert cast_steps <= (t // tm) * n_ft
    cast = lambda i, j: jnp.minimum(i * n_ft + j, cast_steps - 1)
    gate_rows = lambda r0, k: pl.BlockSpec(
        (cast_rows, d), lambda i, j: (r0 // cast_rows + 2 * cast(i, j) + k, 0))
    return pl.pallas_call(
        functools.partial(_in_proj_kernel, seq=seq, cast_steps=cast_steps),
        grid=(t // tm, n_ft),
        in_specs=[
            pl.BlockSpec((tm, d), lambda i, j: (i, 0), pipeline_mode=pl.Buffered(1)),
            pl.BlockSpec((halo, d), lambda i, j: (jnp.maximum(i * (tm // halo) - 1, 0), 0)),
            pl.BlockSpec((halo, d), lambda i, j: (jnp.minimum((i + 1) * (tm // halo), nbh - 1), 0)),
            pl.BlockSpec((tn, d), lambda i, j: (row0 // tn + j, 0)),
            pl.BlockSpec((1, tn), lambda i, j: (0, j)),
            pl.BlockSpec((1, tn), lambda i, j: (0, j)),
            gate_rows(gate_row0, 0), gate_rows(gate_row0, 1),
            gate_rows(gate_row0 + half, 0), gate_rows(gate_row0 + half, 1),
        ],
        out_specs=[
            pl.BlockSpec((tm, tn), lambda i, j: (i, j)),
            pl.BlockSpec((2, 2 * cast_rows, d), lambda i, j: (0, cast(i, j), 0)),
        ],
        out_shape=[
            jax.ShapeDtypeStruct((t, n_ft * tn), BF16),
            jax.ShapeDtypeStruct((2, half, d), BF16),
        ],
        scratch_shapes=[pltpu.VMEM((tm + 2 * halo, d), BF16)],
        compiler_params=_cparams("arbitrary", "arbitrary"),
        name="in_proj",
    )(h, h, h, w_t, mup, mun, w_t, w_t, w_t, w_t)


def _gelu(x):
    return 0.5 * x * (1.0 + lax.erf(x * (2.0 ** -0.5)))


def _sgu_kernel(h_ref, w_ref, lng_ref, lnb_ref, ws_ref, bs_ref, o_ref, wb_ref, *, width, groups):
    @pl.when(pl.program_id(0) == 0)
    def _():
        wb_ref[...] = w_ref[...].astype(wb_ref.dtype)

    tm = h_ref.shape[0]
    uv = lax.dot_general(h_ref[...], wb_ref[...], (((1,), (1,)), ((), ())),
                         preferred_element_type=F32)
    gv = _gelu(uv[:, width:])
    mu = jnp.mean(gv, axis=-1, keepdims=True)
    vc = gv - mu
    var = jnp.mean(vc * vc, axis=-1, keepdims=True)
    vn = (vc * lax.rsqrt(var + LN_EPS) * lng_ref[...] + lnb_ref[...]).astype(BF16)
    gd = width // groups
    for c in range(tm // SGU_CHUNK):
        rows = slice(c * SGU_CHUNK, (c + 1) * SGU_CHUNK)
        for g in range(groups):
            cols = slice(g * gd, (g + 1) * gd)
            mixed = jnp.dot(ws_ref[g], vn[rows, cols], preferred_element_type=F32)
            gu = _gelu(uv[rows, cols])
            o_ref[rows, cols] = (gu * (mixed + bs_ref[:, cols])).astype(o_ref.dtype)


def _sgu(h, w_t, ln_g, ln_b, ws, bs_full, tm=512):
    t, d = h.shape
    width = ln_g.shape[1]
    groups = ws.shape[0]
    return pl.pallas_call(
        functools.partial(_sgu_kernel, width=width, groups=groups),
        grid=(t // tm,),
        in_specs=[
            pl.BlockSpec((tm, d), lambda i: (i, 0)),
            pl.BlockSpec((2 * width, d), lambda i: (0, 0), pipeline_mode=pl.Buffered(1)),
            pl.BlockSpec((1, width), lambda i: (0, 0)),
            pl.BlockSpec((1, width), lambda i: (0, 0)),
            pl.BlockSpec(ws.shape, lambda i: (0, 0, 0)),
            pl.BlockSpec(bs_full.shape, lambda i: (0, 0)),
        ],
        out_specs=pl.BlockSpec((tm, width), lambda i: (i, 0)),
        out_shape=jax.ShapeDtypeStruct((t, width), BF16),
        scratch_shapes=[pltpu.VMEM((2 * width, d), BF16)],
        compiler_params=_cparams("arbitrary"),
        name="sgu_mixer",
    )(h, w_t, ln_g, ln_b, ws, bs_full)


def _head_block_ones(scale):
    r = lax.broadcasted_iota(jnp.int32, (LANES, LANES), 0) // HEAD_DIM
    c = lax.broadcasted_iota(jnp.int32, (LANES, LANES), 1) // HEAD_DIM
    return jnp.where(r == c, scale, 0.0).astype(BF16)


def _head_sum(x, bd):
    return jnp.dot(x.astype(BF16), bd, preferred_element_type=F32)


def _prep_kernel(xs_ref, w0_ref, w2_ref, a0_ref, a2_ref, g2_ref, kk_ref, ka_ref, rk_ref,
                 h_ref, wg_ref, sh_ref, lw_ref, bk_ref, bonus_ref, gg_ref, gt_ref, *, width):
    gt_ref[...] = jax.nn.sigmoid(
        lax.dot_general(h_ref[...], wg_ref[...], (((1,), (1,)), ((), ())),
                        preferred_element_type=F32)).astype(gt_ref.dtype)

    xs = xs_ref[...].astype(F32)
    r = xs[:, 0:width]
    k = xs[:, width:2 * width]
    v = xs[:, 2 * width:3 * width]
    o = 3 * width
    lo_w = xs[:, o:o + LANES]
    lo_a = xs[:, o + LANES:o + 2 * LANES]
    lo_g = xs[:, o + 2 * LANES:]

    wpre = w0_ref[...] + jnp.dot(jnp.tanh(lo_w).astype(BF16), w2_ref[...],
                                 preferred_element_type=F32)
    lw = -(2.718281828459045 ** -0.5) * jax.nn.sigmoid(wpre)
    a = jax.nn.sigmoid(a0_ref[...] + jnp.dot(lo_a.astype(BF16), a2_ref[...],
                                             preferred_element_type=F32))
    gg_ref[...] = jnp.dot(jax.nn.sigmoid(lo_g).astype(BF16), g2_ref[...],
                          preferred_element_type=F32).astype(gg_ref.dtype)
    lw_ref[0] = lw[:, :width]
    lw_ref[1] = lw[:, width:]

    bd = _head_block_ones(1.0)
    kk = k * kk_ref[...]
    sh_ref[0] = xs_ref[:, 0:width]
    sh_ref[1] = xs_ref[:, 2 * width:3 * width]
    for j in range(width // LANES):
        cols = slice(j * LANES, (j + 1) * LANES)
        kkj = kk[:, cols]
        ss = _head_sum(kkj * kkj, bd)
        kkn = kkj * lax.rsqrt(jnp.maximum(ss, 1e-24))
        sh_ref[2, :, cols] = kkn.astype(sh_ref.dtype)
        ksum = jnp.zeros_like(kkj)
        for d in range(2):
            dcols = slice(d * width + j * LANES, d * width + (j + 1) * LANES)
            a_d = a[:, dcols]
            k_d = k[:, cols] * (1.0 + (a_d - 1.0) * ka_ref[:, cols])
            bk_ref[d, 0, :, cols] = (kkn * a_d).astype(bk_ref.dtype)
            bk_ref[d, 1, :, cols] = k_d.astype(bk_ref.dtype)
            ksum = ksum + k_d
        bsum = _head_sum(r[:, cols] * ksum * rk_ref[:, cols], bd)
        bonus_ref[:, cols] = (bsum * v[:, cols]).astype(bonus_ref.dtype)


def _rwkv_prep(xs, h, wg, width, w0c, w2c, a0c, a2c, g2p, k_k, k_a, r_k, tm=128):
    t, fw = xs.shape
    _, ngh, d = wg.shape
    r = lambda j, i: 2 * i + j
    row = lambda j, i: (0, 0)
    resident = lambda a: pl.BlockSpec(a.shape, row, pipeline_mode=pl.Buffered(1))
    rows = lambda w: pl.BlockSpec((tm, w), lambda j, i: (r(j, i), 0))
    return pl.pallas_call(
        functools.partial(_prep_kernel, width=width),
        grid=(2, t // (2 * tm)),
        in_specs=[
            rows(fw),
            pl.BlockSpec(w0c.shape, row),
            resident(w2c),
            pl.BlockSpec(a0c.shape, row),
            resident(a2c),
            resident(g2p),
            pl.BlockSpec((1, width), row),
            pl.BlockSpec((1, width), row),
            pl.BlockSpec((1, width), row),
            pl.BlockSpec((2 * tm, d), lambda j, i: (i, 0)),
            pl.BlockSpec((None, ngh, d), lambda j, i: (j, 0, 0), pipeline_mode=pl.Buffered(1)),
        ],
        out_specs=[
            pl.BlockSpec((3, tm, width), lambda j, i: (0, r(j, i), 0)),
            pl.BlockSpec((2, tm, width), lambda j, i: (0, r(j, i), 0)),
            pl.BlockSpec((2, 2, tm, width), lambda j, i: (0, 0, r(j, i), 0)),
            rows(width),
            rows(width),
            pl.BlockSpec((2 * tm, ngh), lambda j, i: (i, j)),
        ],
        out_shape=[
            jax.ShapeDtypeStruct((3, t, width), BF16),
            jax.ShapeDtypeStruct((2, t, width), F32),
            jax.ShapeDtypeStruct((2, 2, t, width), BF16),
            jax.ShapeDtypeStruct((t, width), BF16),
            jax.ShapeDtypeStruct((t, width), BF16),
            jax.ShapeDtypeStruct((t, 2 * ngh), BF16),
        ],
        compiler_params=_cparams("arbitrary", "arbitrary"),
        name="rwkv_prep",
    )(xs, w0c, w2c, a0c, a2c, g2p, k_k, k_a, r_k, h, wg)


def _hat(x, first_half):
    return jnp.concatenate([jnp.where(first_half, x, 0.0), jnp.where(first_half, 0.0, x)],
                           axis=0)


def _scan_kernel(shf_ref, shb_ref, lwf_ref, lwb_ref, bkf_ref, bkb_ref, *rest, group, n_cast,
                 cast_steps):
    cast_in, (of_ref, ob_ref), cast_out, s_ref = (rest[:n_cast], rest[n_cast:n_cast + 2],
                                                  rest[n_cast + 2:2 * n_cast + 2], rest[-1])
    c = SCAN_CHUNK
    batch, width = shf_ref.shape[1], shf_ref.shape[3]
    npair = width // LANES
    assert 2 * c == LANES and c == HEAD_DIM

    @pl.when(pl.program_id(0) == 0)
    def _():
        s_ref[...] = jnp.zeros_like(s_ref)

    @pl.when(pl.program_id(0) < cast_steps)
    def _():
        for src, dst in zip(cast_in, cast_out):
            dst[...] = src[...].astype(dst.dtype)

    bf = lambda t: t.astype(BF16)
    f32 = lambda t: t.astype(F32)
    dg = lambda p, q, dims: lax.dot_general(bf(p), bf(q), (dims, ((), ())),
                                            preferred_element_type=F32)
    dot = lambda p, q: dg(p, q, ((1,), (0,)))
    dot_nt = lambda p, q: dg(p, q, ((1,), (1,)))
    dot_tn = lambda p, q: dg(p, q, ((0,), (0,)))

    wt = lax.broadcasted_iota(jnp.int32, (c, LANES), 0)
    ws = lax.broadcasted_iota(jnp.int32, (c, LANES), 1) % c
    masks = ((wt > ws, wt >= ws), (wt < ws, wt <= ws))
    eye = jnp.where(wt == ws, 1.0, 0.0).astype(F32)
    half = lax.broadcasted_iota(jnp.int32, (c, LANES), 1) < c
    same_head = (lax.broadcasted_iota(jnp.int32, (LANES, LANES), 0) // HEAD_DIM
                 == lax.broadcasted_iota(jnp.int32, (LANES, LANES), 1) // HEAD_DIM)
    hat = lambda t: _hat(t, half)
    refs = ((shf_ref, lwf_ref, bkf_ref, of_ref), (shb_ref, lwb_ref, bkb_ref, ob_ref))

    st = {}

    def prepare(unit):
        d, b, p = unit
        sh_ref, lw_ref, bk_ref, _ = refs[d]
        strict, incl = masks[d]
        cols = slice(p * LANES, (p + 1) * LANES)
        r = f32(sh_ref[0, b, :, cols])
        v = f32(sh_ref[1, b, :, cols])
        kk = f32(sh_ref[2, b, :, cols])
        lw = lw_ref[0, b, :, cols]
        bb = f32(bk_ref[0, 0, b, :, cols])
        kd = f32(bk_ref[0, 1, b, :, cols])
        pre = lw
        s = 1
        while s < c:
            pre = pre + jnp.where(wt >= s, pltpu.roll(pre, s, 0), 0.0)
            s *= 2
        gtot = pre[c - 1:c, :]
        gi = pre if d == 0 else gtot - pre + lw
        en = jnp.exp(-gi)
        ec = jnp.exp(gtot - gi)
        lhs = bf(jnp.concatenate([-kk * jnp.exp(gi - lw), r * jnp.exp(gi)], axis=0))
        rhs = jnp.concatenate([hat(bb * en), hat(kd * en)], axis=0)
        aa = dot_nt(lhs, rhs)
        a_ab = jnp.where(strict, aa[:c, :LANES], 0.0)
        a_ak = jnp.where(strict, aa[:c, LANES:], 0.0)
        ark = bf(jnp.concatenate([jnp.where(incl, aa[c:, :LANES], 0.0),
                                  jnp.where(incl, aa[c:, LANES:], 0.0)], axis=1))
        vh = bf(hat(v))
        sidx = (d * batch + b) * npair + p
        s0 = s_ref[sidx]
        ls = dot_nt(lhs, s0)
        st[unit] = dict(
            sidx=sidx, cols=cols, v=v, s0=s0, ark=ark, vh=vh,
            bk=bf(jnp.concatenate([bb * ec, kd * ec], axis=0)),
            eg=jnp.exp(gtot), pw=a_ab, inv=eye + a_ab,
            rhs_u=ls[:c] + dot(a_ak, vh), rs=ls[c:])

    def square(grp):
        for u in grp:
            st[u]["pw"] = dot(st[u]["pw"], hat(st[u]["pw"]))

    def inverse_round(grp):
        for u in grp:
            e = st[u]
            both = dot(jnp.concatenate([e["pw"], e["inv"]], axis=0), hat(e["pw"]))
            e["pw"] = both[:c]
            e["inv"] = e["inv"] + both[c:]

    def finish_stages(grp):
        def last_product():
            for u in grp:
                st[u]["inv"] = st[u]["inv"] + dot(st[u]["inv"], hat(st[u]["pw"]))

        def solve():
            for u in grp:
                st[u]["u"] = dot(st[u]["inv"], hat(st[u]["rhs_u"]))

        def outputs():
            for (d, b, p) in grp:
                e = st[(d, b, p)]
                o_ref = refs[d][3]
                uvh = jnp.concatenate([bf(hat(e["u"])), e["vh"]], axis=0)
                o_ref[b, :, e["cols"]] = (e["rs"] + dot(e["ark"], uvh)).astype(o_ref.dtype)

        def states():
            for u in grp:
                e = st[u]
                uv = jnp.concatenate([e["u"], e["v"]], axis=0)
                s_ref[e["sidx"]] = (e["s0"] * e["eg"]
                                    + jnp.where(same_head, dot_tn(uv, e["bk"]), 0.0))

        return [last_product, solve, outputs, states]

    units = [(d, b, p) for d in range(2) for b in range(batch) for p in range(npair)]
    groups = [units[g0:g0 + group] for g0 in range(0, len(units), group)]
    rounds = 0
    n = 4
    while n < c:
        rounds += 1
        n *= 2
    for u in groups[0]:
        prepare(u)
    pending = []
    for gi_, grp in enumerate(groups):
        nxt = groups[gi_ + 1] if gi_ + 1 < len(groups) else []
        share = -(-len(nxt) // (rounds + 1))
        square(grp)
        for u in nxt[:share]:
            prepare(u)
        for rd in range(rounds):
            if pending:
                pending.pop(0)()
            inverse_round(grp)
            for u in nxt[(rd + 1) * share:(rd + 2) * share]:
                prepare(u)
        for stage in pending:
            stage()
        pending = finish_stages(grp)
    for stage in pending:
        stage()


def _rwkv_scan(sh, lw, bk, batch, seq, to_bf16=(), group=16, cast_steps=32):
    _, t, width = sh.shape
    c = SCAN_CHUNK
    nc = seq // c
    sh = sh.reshape(3, batch, seq, width)
    lw = lw.reshape(2, batch, seq, width)
    bk = bk.reshape(2, 2, batch, seq, width)
    sh_spec = lambda f: pl.BlockSpec((3, batch, c, width), lambda i: (0, 0, f(i), 0))
    lw_spec = lambda d, f: pl.BlockSpec((1, batch, c, width), lambda i: (d, 0, f(i), 0))
    bk_spec = lambda d, f: pl.BlockSpec((1, 2, batch, c, width), lambda i: (d, 0, 0, f(i), 0))
    o_spec = lambda f: pl.BlockSpec((batch, c, width), lambda i: (0, f(i), 0))
    fwd = lambda i: i
    bwd = lambda i: nc - 1 - i
    cast_specs = [pl.BlockSpec((a.shape[0] // cast_steps, a.shape[1]),
                               lambda i: (jnp.minimum(i, cast_steps - 1), 0)) for a in to_bf16]
    outs = pl.pallas_call(
        functools.partial(_scan_kernel, group=group, n_cast=len(to_bf16), cast_steps=cast_steps),
        grid=(nc,),
        in_specs=[sh_spec(fwd), sh_spec(bwd), lw_spec(0, fwd), lw_spec(1, bwd),
                  bk_spec(0, fwd), bk_spec(1, bwd)] + cast_specs,
        out_specs=[o_spec(fwd), o_spec(bwd)] + cast_specs,
        out_shape=([jax.ShapeDtypeStruct((batch, seq, width), BF16)] * 2
                   + [jax.ShapeDtypeStruct(a.shape, BF16) for a in to_bf16]),
        scratch_shapes=[pltpu.VMEM((2 * batch * (width // LANES), LANES, LANES), F32)],
        compiler_params=_cparams("arbitrary"),
        name="rwkv_scan",
    )(sh, sh, lw, lw, bk, bk, *to_bf16)
    return (outs[0].reshape(t, width), outs[1].reshape(t, width)) + tuple(outs[2:])


def _rwkv_out_rows(of_ref, ob_ref, bonus_ref, gg_ref, gng_ref, gnb_ref):
    width = of_ref.shape[1]
    avg = _head_block_ones(1.0 / HEAD_DIM)
    tiles = []
    for j in range(width // LANES):
        cols = slice(j * LANES, (j + 1) * LANES)
        o = of_ref[:, cols].astype(F32) + ob_ref[:, cols].astype(F32)
        oc = o - _head_sum(o, avg)
        var = _head_sum(oc * oc, avg)
        y = oc * lax.rsqrt(var + GN_EPS) * gng_ref[:, cols] + gnb_ref[:, cols]
        tiles.append(((y + bonus_ref[:, cols].astype(F32))
                      * gg_ref[:, cols].astype(F32)).astype(BF16))
    return jnp.concatenate(tiles, axis=1)


def _mix_out_kernel(ya_ref, of_ref, ob_ref, bonus_ref, gg_ref, gng_ref, gnb_ref, wa_ref, wb_ref,
                    ga_ref, gb_ref, wo_ref, x_ref, g_ref, x1_ref, h2_ref):
    yb = _rwkv_out_rows(of_ref, ob_ref, bonus_ref, gg_ref, gng_ref, gnb_ref)
    pa = jnp.dot(ya_ref[...], wa_ref[...], preferred_element_type=F32)
    pb = jnp.dot(yb, wb_ref[...], preferred_element_type=F32)
    merged = (ga_ref[...].astype(F32) * pa + gb_ref[...].astype(F32) * pb).astype(BF16)
    x1 = x_ref[...] + jnp.dot(merged, wo_ref[...], preferred_element_type=F32)
    x1_ref[...] = x1
    ms = jnp.mean(x1 * x1, axis=-1, keepdims=True)
    h2_ref[...] = (x1 * lax.rsqrt(ms + NORM_EPS) * g_ref[...]).astype(h2_ref.dtype)


def _mix_out(ya, o_f, o_b, bonus, gg, gn_g, gn_b, wa, wb, gates, wo, x, g, tm=256):
    t, k = ya.shape
    d = x.shape[1]
    resident = lambda shape: pl.BlockSpec(shape, lambda i: (0, 0), pipeline_mode=pl.Buffered(1))
    rows = lambda w: pl.BlockSpec((tm, w), lambda i: (i, 0))
    return pl.pallas_call(
        _mix_out_kernel,
        grid=(t // tm,),
        in_specs=[
            rows(k), rows(k), rows(k), rows(k), rows(k),
            pl.BlockSpec((1, k), lambda i: (0, 0)),
            pl.BlockSpec((1, k), lambda i: (0, 0)),
            resident((k, d)),
            resident((k, d)),
            pl.BlockSpec((tm, d), lambda i: (i, 0)),
            pl.BlockSpec((tm, d), lambda i: (i, 1)),
            resident((d, d)),
            rows(d),
            pl.BlockSpec((1, d), lambda i: (0, 0)),
        ],
        out_specs=[rows(d), rows(d)],
        out_shape=[jax.ShapeDtypeStruct((t, d), F32), jax.ShapeDtypeStruct((t, d), BF16)],
        compiler_params=_cparams("parallel"),
        name="mix_out",
    )(ya, o_f, o_b, bonus, gg, gn_g, gn_b, wa, wb, gates, gates, wo, x, g)


def _ffn1_kernel(h_ref, hb_ref, ha_ref, wg_ref, wu_ref, cw_ref, cb_ref, a_ref, *, seq):
    tm = h_ref.shape[0]
    halo = hb_ref.shape[0]
    i = pl.program_id(0)
    tiles_per_seq = seq // tm
    first = (i % tiles_per_seq) == 0
    last = (i % tiles_per_seq) == tiles_per_seq - 1

    h = h_ref[...]
    zero = jnp.zeros_like(hb_ref[...])
    h_ext = jnp.concatenate([jnp.where(first, zero, hb_ref[...]), h,
                             jnp.where(last, zero, ha_ref[...])], axis=0)
    g_ext = jnp.dot(h_ext, wg_ref[...].astype(BF16), preferred_element_type=F32)
    up = jnp.dot(h, wu_ref[...].astype(BF16), preferred_element_type=F32)
    rows = tm + 2 * halo
    g = g_ext[halo:halo + tm]
    prev = pltpu.roll(g_ext, 1, 0)[halo:halo + tm]
    nxt = pltpu.roll(g_ext, rows - 1, 0)[halo:halo + tm]
    gt = prev * cw_ref[0:1, :] + g * cw_ref[1:2, :] + nxt * cw_ref[2:3, :] + cb_ref[...]
    a_ref[...] = (jax.nn.silu(gt) * up).astype(a_ref.dtype)


def _ffn1(h, wg, wu, cw, cb, seq, tm=1024, tn=512, halo=16):
    t, d = h.shape
    f = wg.shape[1]
    nbh = t // halo
    return pl.pallas_call(
        functools.partial(_ffn1_kernel, seq=seq),
        grid=(t // tm, f // tn),
        in_specs=[
            pl.BlockSpec((tm, d), lambda i, j: (i, 0)),
            pl.BlockSpec((halo, d), lambda i, j: (jnp.maximum(i * (tm // halo) - 1, 0), 0)),
            pl.BlockSpec((halo, d), lambda i, j: (jnp.minimum((i + 1) * (tm // halo), nbh - 1), 0)),
            pl.BlockSpec((d, tn), lambda i, j: (0, j)),
            pl.BlockSpec((d, tn), lambda i, j: (0, j)),
            pl.BlockSpec((cw.shape[0], tn), lambda i, j: (0, j)),
            pl.BlockSpec((1, tn), lambda i, j: (0, j)),
        ],
        out_specs=pl.BlockSpec((tm, tn), lambda i, j: (i, j)),
        out_shape=jax.ShapeDtypeStruct((t, f), BF16),
        compiler_params=_cparams("parallel", "arbitrary"),
        name="ffn_gate_up",
    )(h, h, h, wg, wu, cw, cb)


def _ffn2_kernel(a_ref, wd_ref, x1_ref, nf_ref, o_ref):
    xf = x1_ref[...] + jnp.dot(a_ref[...], wd_ref[...], preferred_element_type=F32)
    ms = jnp.mean(xf * xf, axis=-1, keepdims=True)
    o_ref[...] = xf * lax.rsqrt(ms + NORM_EPS) * nf_ref[...]


def _ffn2(act, wd, x1, nf, tm=256):
    t, f = act.shape
    d = x1.shape[1]
    return pl.pallas_call(
        _ffn2_kernel,
        grid=(t // tm,),
        in_specs=[
            pl.BlockSpec((tm, f), lambda i: (i, 0)),
            pl.BlockSpec((f, d), lambda i: (0, 0), pipeline_mode=pl.Buffered(1)),
            pl.BlockSpec((tm, d), lambda i: (i, 0)),
            pl.BlockSpec((1, d), lambda i: (0, 0)),
        ],
        out_specs=pl.BlockSpec((tm, d), lambda i: (i, 0)),
        out_shape=jax.ShapeDtypeStruct((t, d), F32),
        compiler_params=_cparams("parallel"),
        name="ffn_down",
    )(act, wd, x1, nf)


def _pad_cols(a, n):
    return jnp.pad(a, ((0, 0), (0, n - a.shape[1])))


def _pad_rows(a, n):
    return jnp.pad(a, ((0, n - a.shape[0]), (0, 0)))


def _layer(x, batch, seq, norm1_g, w_in, sgu_ln_g, sgu_ln_b, sgu_w, sgu_b,
           mu_prev, mu_next, w0_f, w2_f, a0_f, a2_f, w0_b, w2_b, a0_b, a2_b, k_k, k_a, r_k,
           g2, gn_g, gn_b, w_proj_a, w_proj_b, w_out, norm2_g, ffn_w_gate, ffn_w_up,
           ffn_conv_w, ffn_conv_b, ffn_w_down, norm_out_g):
    d = x.shape[1]
    sgu_width = sgu_ln_g.shape[0]
    width = k_k.shape[0]
    dl = w2_f.shape[0]
    al = a2_f.shape[0]
    feat_w = mu_prev.shape[0]
    feat_pad = -(-feat_w // (4 * LANES)) * (4 * LANES)
    assert dl == HEAD_DIM and al == HEAD_DIM and 2 * dl == LANES
    row = lambda a: a.reshape(1, -1)

    o1 = 2 * sgu_width
    o2 = o1 + feat_w
    w_t = jnp.swapaxes(w_in, 0, 1)
    zw = jnp.zeros_like(w2_f)
    w2c = jnp.concatenate([jnp.concatenate([w2_f, zw], 1),
                           jnp.concatenate([zw, w2_b], 1)], 0).astype(BF16)
    za = jnp.zeros_like(a2_f)
    a2c = jnp.concatenate([jnp.concatenate([a2_f, za], 1),
                           jnp.concatenate([za, a2_b], 1)], 0).astype(BF16)
    g2p = _pad_rows(g2, feat_pad - 3 * width - 2 * LANES).astype(BF16)
    w0c = row(jnp.concatenate([w0_f, w0_b]))
    a0c = row(jnp.concatenate([a0_f, a0_b]))
    mup = _pad_cols(row(mu_prev), feat_pad)
    mun = _pad_cols(row(mu_next), feat_pad)
    bs_full = jnp.repeat(sgu_b.T, sgu_width // sgu_b.shape[0], axis=1)

    h1 = _rmsnorm(x, row(norm1_g))
    xs, w_gates = _in_proj(h1, w_t, mup, mun, o1, feat_pad, o2, seq)

    y_a = _sgu(h1, w_t, row(sgu_ln_g), row(sgu_ln_b), sgu_w.astype(BF16), bs_full)

    sh, lw, bk, bonus, gg, gates = _rwkv_prep(xs, h1, w_gates, width, w0c, w2c, a0c, a2c, g2p,
                                              row(k_k), row(k_a), row(r_k))
    o_f, o_b, wa_bf, wb_bf, wo_bf, wd_bf = _rwkv_scan(
        sh, lw, bk, batch, seq, to_bf16=(w_proj_a, w_proj_b, w_out, ffn_w_down))

    x1, h2 = _mix_out(y_a, o_f, o_b, bonus, gg, row(gn_g), row(gn_b), wa_bf, wb_bf, gates, wo_bf,
                      x, row(norm2_g))

    act = _ffn1(h2, ffn_w_gate, ffn_w_up, ffn_conv_w, row(ffn_conv_b), seq)
    return _ffn2(act, wd_bf, x1, norm_out_g)


def kernel(x, norm1_g, w_in, sgu_ln_g, sgu_ln_b, sgu_w, sgu_b, rwkv_mu_prev, rwkv_mu_next, rwkv_w0_f, rwkv_w2_f, rwkv_a0_f, rwkv_a2_f, rwkv_w0_b, rwkv_w2_b, rwkv_a0_b, rwkv_a2_b, rwkv_k_k, rwkv_k_a, rwkv_r_k, rwkv_g2, rwkv_gn_g, rwkv_gn_b, w_proj_a, w_proj_b, w_out, norm2_g, ffn_w_gate, ffn_w_up, ffn_conv_w, ffn_conv_b, ffn_w_down, norm_f_g):
    batch, seq, d = x.shape
    depth = norm1_g.shape[0]
    assert depth == 1, "the fused final RMSNorm assumes a single layer"
    xf = x.reshape(batch * seq, d)
    per_layer = (norm1_g, w_in, sgu_ln_g, sgu_ln_b, sgu_w, sgu_b, rwkv_mu_prev, rwkv_mu_next,
                 rwkv_w0_f, rwkv_w2_f, rwkv_a0_f, rwkv_a2_f, rwkv_w0_b, rwkv_w2_b, rwkv_a0_b,
                 rwkv_a2_b, rwkv_k_k, rwkv_k_a, rwkv_r_k, rwkv_g2, rwkv_gn_g, rwkv_gn_b,
                 w_proj_a, w_proj_b, w_out, norm2_g, ffn_w_gate, ffn_w_up, ffn_conv_w,
                 ffn_conv_b, ffn_w_down)
    out = _layer(xf, batch, seq, *(p[0] for p in per_layer), norm_f_g.reshape(1, d))
    return out.reshape(batch, seq, d)
```

```python
import functools

import jax
import jax.numpy as jnp
from jax import lax
from jax.experimental import pallas as pl
from jax.experimental.pallas import tpu as pltpu

F32 = jnp.float32
BF16 = jnp.bfloat16

NORM_EPS = 1e-6
LN_EPS = 1e-5
GN_EPS = 64e-5

HEAD_DIM = 64
LANES = 128
SGU_CHUNK = 128
SCAN_CHUNK = 64
VMEM_LIMIT = 48 * 1024 * 1024


def _cparams(*sem):
    return pltpu.CompilerParams(dimension_semantics=sem, vmem_limit_bytes=VMEM_LIMIT)


def _in_proj_kernel(h_ref, hb_ref, ha_ref, w_ref, mup_ref, mun_ref, ga0_ref, ga1_ref, gb0_ref,
                    gb1_ref, xs_ref, wg_ref, hx_ref, *, seq, cast_steps):
    tm = h_ref.shape[0]
    halo = hb_ref.shape[0]
    cr = ga0_ref.shape[0]
    i = pl.program_id(0)
    j = pl.program_id(1)
    tiles_per_seq = seq // tm
    first = (i % tiles_per_seq) == 0
    last = (i % tiles_per_seq) == tiles_per_seq - 1

    @pl.when(i * pl.num_programs(1) + j < cast_steps)
    def _():
        wg_ref[0, 0:cr, :] = ga0_ref[...].astype(wg_ref.dtype)
        wg_ref[0, cr:, :] = ga1_ref[...].astype(wg_ref.dtype)
        wg_ref[1, 0:cr, :] = gb0_ref[...].astype(wg_ref.dtype)
        wg_ref[1, cr:, :] = gb1_ref[...].astype(wg_ref.dtype)

    @pl.when(j == 0)
    def _():
        zero = jnp.zeros_like(hb_ref[...])
        hx_ref[0:halo, :] = jnp.where(first, zero, hb_ref[...])
        hx_ref[halo:halo + tm, :] = h_ref[...]
        hx_ref[halo + tm:, :] = jnp.where(last, zero, ha_ref[...])

    f_ext = lax.dot_general(hx_ref[...], w_ref[...].astype(BF16), (((1,), (1,)), ((), ())),
                            preferred_element_type=F32)
    rows = tm + 2 * halo
    f = f_ext[halo:halo + tm]
    prev = pltpu.roll(f_ext, 1, 0)[halo:halo + tm]
    nxt = pltpu.roll(f_ext, rows - 1, 0)[halo:halo + tm]
    mup = mup_ref[...]
    mun = mun_ref[...]
    xs_ref[...] = (f * (1.0 - mup - mun) + mup * prev + mun * nxt).astype(xs_ref.dtype)


def _in_proj(h, w_t, mup, mun, row0, w_ft, gate_row0, seq, tm=1024, tn=512, halo=16,
             cast_rows=32):
    t, d = h.shape
    n_ft = w_ft // tn
    nbh = t // halo
    half = (w_t.shape[0] - gate_row0) // 2
    cast_steps = half // (2 * cast_rows)
    assert row0 % tn == 0 and gate_row0 % cast_rows == 0 and half % (2 * cast_rows) == 0
    assert cast_steps <= (t // tm) * n_ft
    cast = lambda i, j: jnp.minimum(i * n_ft + j, cast_steps - 1)
    gate_rows = lambda r0, k: pl.BlockSpec(
        (cast_rows, d), lambda i, j: (r0 // cast_rows + 2 * cast(i, j) + k, 0))
    return pl.pallas_call(
        functools.partial(_in_proj_kernel, seq=seq, cast_steps=cast_steps),
        grid=(t // tm, n_ft),
        in_specs=[
            pl.BlockSpec((tm, d), lambda i, j: (i, 0), pipeline_mode=pl.Buffered(1)),
            pl.BlockSpec((halo, d), lambda i, j: (jnp.maximum(i * (tm // halo) - 1, 0), 0)),
            pl.BlockSpec((halo, d), lambda i, j: (jnp.minimum((i + 1) * (tm // halo), nbh - 1), 0)),
            pl.BlockSpec((tn, d), lambda i, j: (row0 // tn + j, 0)),
            pl.BlockSpec((1, tn), lambda i, j: (0, j)),
            pl.BlockSpec((1, tn), lambda i, j: (0, j)),
            gate_rows(gate_row0, 0), gate_rows(gate_row0, 1),
            gate_rows(gate_row0 + half, 0), gate_rows(gate_row0 + half, 1),
        ],
        out_specs=[
            pl.BlockSpec((tm, tn), lambda i, j: (i, j)),
            pl.BlockSpec((2, 2 * cast_rows, d), lambda i, j: (0, cast(i, j), 0)),
        ],
        out_shape=[
            jax.ShapeDtypeStruct((t, n_ft * tn), BF16),
            jax.ShapeDtypeStruct((2, half, d), BF16),
        ],
        scratch_shapes=[pltpu.VMEM((tm + 2 * halo, d), BF16)],
        compiler_params=_cparams("arbitrary", "arbitrary"),
        name="in_proj",
    )(h, h, h, w_t, mup, mun, w_t, w_t, w_t, w_t)


def _gelu(x):
    return 0.5 * x * (1.0 + lax.erf(x * (2.0 ** -0.5)))


def _sgu_kernel(x_ref, g_ref, w_ref, lng_ref, lnb_ref, ws_ref, bs_ref, h_ref, o_ref, wb_ref,
                *, width, groups):
    @pl.when(pl.program_id(0) == 0)
    def _():
        wb_ref[...] = w_ref[...].astype(wb_ref.dtype)

    tm = x_ref.shape[0]
    nt = (((1,), (1,)), ((), ()))
    x = x_ref[...]
    ms = jnp.mean(x * x, axis=-1, keepdims=True)
    h = (x * lax.rsqrt(ms + NORM_EPS) * g_ref[...]).astype(h_ref.dtype)
    h_ref[...] = h
    v = lax.dot_general(h, wb_ref[width:, :], nt, preferred_element_type=F32)
    gv = _gelu(v)
    mu = jnp.mean(gv, axis=-1, keepdims=True)
    vc = gv - mu
    var = jnp.mean(vc * vc, axis=-1, keepdims=True)
    vn = (vc * lax.rsqrt(var + LN_EPS) * lng_ref[...] + lnb_ref[...]).astype(BF16)
    uv = lax.dot_general(h, wb_ref[0:width, :], nt, preferred_element_type=F32)
    gd = width // groups
    for c in range(tm // SGU_CHUNK):
        rows = slice(c * SGU_CHUNK, (c + 1) * SGU_CHUNK)
        for g in range(groups):
            cols = slice(g * gd, (g + 1) * gd)
            mixed = jnp.dot(ws_ref[g], vn[rows, cols], preferred_element_type=F32)
            gu = _gelu(uv[rows, cols])
            o_ref[rows, cols] = (gu * (mixed + bs_ref[:, cols])).astype(o_ref.dtype)


def _sgu(x, g, w_t, ln_g, ln_b, ws, bs_full, tm=256):
    t, d = x.shape
    width = ln_g.shape[1]
    groups = ws.shape[0]
    return pl.pallas_call(
        functools.partial(_sgu_kernel, width=width, groups=groups),
        grid=(t // tm,),
        in_specs=[
            pl.BlockSpec((tm, d), lambda i: (i, 0)),
            pl.BlockSpec((1, d), lambda i: (0, 0)),
            pl.BlockSpec((2 * width, d), lambda i: (0, 0), pipeline_mode=pl.Buffered(1)),
            pl.BlockSpec((1, width), lambda i: (0, 0)),
            pl.BlockSpec((1, width), lambda i: (0, 0)),
            pl.BlockSpec(ws.shape, lambda i: (0, 0, 0)),
            pl.BlockSpec(bs_full.shape, lambda i: (0, 0)),
        ],
        out_specs=[pl.BlockSpec((tm, d), lambda i: (i, 0)),
                   pl.BlockSpec((tm, width), lambda i: (i, 0))],
        out_shape=[jax.ShapeDtypeStruct((t, d), BF16), jax.ShapeDtypeStruct((t, width), BF16)],
        scratch_shapes=[pltpu.VMEM((2 * width, d), BF16)],
        compiler_params=_cparams("arbitrary"),
        name="sgu_mixer",
    )(x, g, w_t, ln_g, ln_b, ws, bs_full)


def _head_block_ones(scale):
    r = lax.broadcasted_iota(jnp.int32, (LANES, LANES), 0) // HEAD_DIM
    c = lax.broadcasted_iota(jnp.int32, (LANES, LANES), 1) // HEAD_DIM
    return jnp.where(r == c, scale, 0.0).astype(BF16)


def _head_sum(x, bd):
    return jnp.dot(x.astype(BF16), bd, preferred_element_type=F32)


def _prep_kernel(xs_ref, w0_ref, w2_ref, a0_ref, a2_ref, g2_ref, kk_ref, ka_ref, rk_ref,
                 h_ref, wg_ref, sh_ref, lw_ref, bk_ref, bonus_ref, gg_ref, gt_ref, *, width):
    gt_ref[...] = jax.nn.sigmoid(
        lax.dot_general(h_ref[...], wg_ref[...], (((1,), (1,)), ((), ())),
                        preferred_element_type=F32)).astype(gt_ref.dtype)

    xs = xs_ref[...].astype(F32)
    r = xs[:, 0:width]
    k = xs[:, width:2 * width]
    v = xs[:, 2 * width:3 * width]
    o = 3 * width
    lo_w = xs[:, o:o + LANES]
    lo_a = xs[:, o + LANES:o + 2 * LANES]
    lo_g = xs[:, o + 2 * LANES:]

    wpre = w0_ref[...] + jnp.dot(jnp.tanh(lo_w).astype(BF16), w2_ref[...],
                                 preferred_element_type=F32)
    lw = -(2.718281828459045 ** -0.5) * jax.nn.sigmoid(wpre)
    a = jax.nn.sigmoid(a0_ref[...] + jnp.dot(lo_a.astype(BF16), a2_ref[...],
                                             preferred_element_type=F32))
    gg_ref[...] = jnp.dot(jax.nn.sigmoid(lo_g).astype(BF16), g2_ref[...],
                          preferred_element_type=F32).astype(gg_ref.dtype)
    lw_ref[0] = lw[:, :width]
    lw_ref[1] = lw[:, width:]

    bd = _head_block_ones(1.0)
    kk = k * kk_ref[...]
    sh_ref[0] = xs_ref[:, 0:width]
    sh_ref[1] = xs_ref[:, 2 * width:3 * width]
    for j in range(width // LANES):
        cols = slice(j * LANES, (j + 1) * LANES)
        kkj = kk[:, cols]
        ss = _head_sum(kkj * kkj, bd)
        kkn = kkj * lax.rsqrt(jnp.maximum(ss, 1e-24))
        sh_ref[2, :, cols] = kkn.astype(sh_ref.dtype)
        ksum = jnp.zeros_like(kkj)
        for d in range(2):
            dcols = slice(d * width + j * LANES, d * width + (j + 1) * LANES)
            a_d = a[:, dcols]
            k_d = k[:, cols] * (1.0 + (a_d - 1.0) * ka_ref[:, cols])
            bk_ref[d, 0, :, cols] = (kkn * a_d).astype(bk_ref.dtype)
            bk_ref[d, 1, :, cols] = k_d.astype(bk_ref.dtype)
            ksum = ksum + k_d
        bsum = _head_sum(r[:, cols] * ksum * rk_ref[:, cols], bd)
        bonus_ref[:, cols] = (bsum * v[:, cols]).astype(bonus_ref.dtype)


def _rwkv_prep(xs, h, wg, width, w0c, w2c, a0c, a2c, g2p, k_k, k_a, r_k, tm=128):
    t, fw = xs.shape
    _, ngh, d = wg.shape
    r = lambda j, i: 2 * i + j
    row = lambda j, i: (0, 0)
    resident = lambda a: pl.BlockSpec(a.shape, row, pipeline_mode=pl.Buffered(1))
    rows = lambda w: pl.BlockSpec((tm, w), lambda j, i: (r(j, i), 0))
    return pl.pallas_call(
        functools.partial(_prep_kernel, width=width),
        grid=(2, t // (2 * tm)),
        in_specs=[
            rows(fw),
            pl.BlockSpec(w0c.shape, row),
            resident(w2c),
            pl.BlockSpec(a0c.shape, row),
            resident(a2c),
            resident(g2p),
            pl.BlockSpec((1, width), row),
            pl.BlockSpec((1, width), row),
            pl.BlockSpec((1, width), row),
            pl.BlockSpec((2 * tm, d), lambda j, i: (i, 0)),
            pl.BlockSpec((None, ngh, d), lambda j, i: (j, 0, 0), pipeline_mode=pl.Buffered(1)),
        ],
        out_specs=[
            pl.BlockSpec((3, tm, width), lambda j, i: (0, r(j, i), 0)),
            pl.BlockSpec((2, tm, width), lambda j, i: (0, r(j, i), 0)),
            pl.BlockSpec((2, 2, tm, width), lambda j, i: (0, 0, r(j, i), 0)),
            rows(width),
            rows(width),
            pl.BlockSpec((2 * tm, ngh), lambda j, i: (i, j)),
        ],
        out_shape=[
            jax.ShapeDtypeStruct((3, t, width), BF16),
            jax.ShapeDtypeStruct((2, t, width), F32),
            jax.ShapeDtypeStruct((2, 2, t, width), BF16),
            jax.ShapeDtypeStruct((t, width), BF16),
            jax.ShapeDtypeStruct((t, width), BF16),
            jax.ShapeDtypeStruct((t, 2 * ngh), BF16),
        ],
        compiler_params=_cparams("arbitrary", "arbitrary"),
        name="rwkv_prep",
    )(xs, w0c, w2c, a0c, a2c, g2p, k_k, k_a, r_k, h, wg)


def _hat(x, first_half):
    return jnp.concatenate([jnp.where(first_half, x, 0.0), jnp.where(first_half, 0.0, x)],
                           axis=0)


def _scan_kernel(shf_ref, shb_ref, lwf_ref, lwb_ref, bkf_ref, bkb_ref, *rest, group, n_cast,
                 cast_steps):
    cast_in, (of_ref, ob_ref), cast_out, s_ref = (rest[:n_cast], rest[n_cast:n_cast + 2],
                                                  rest[n_cast + 2:2 * n_cast + 2], rest[-1])
    c = SCAN_CHUNK
    batch, width = shf_ref.shape[1], shf_ref.shape[3]
    npair = width // LANES
    assert 2 * c == LANES and c == HEAD_DIM

    @pl.when(pl.program_id(0) == 0)
    def _():
        s_ref[...] = jnp.zeros_like(s_ref)

    @pl.when(pl.program_id(0) < cast_steps)
    def _():
        for src, dst in zip(cast_in, cast_out):
            dst[...] = src[...].astype(dst.dtype)

    bf = lambda t: t.astype(BF16)
    f32 = lambda t: t.astype(F32)
    dg = lambda p, q, dims: lax.dot_general(bf(p), bf(q), (dims, ((), ())),
                                            preferred_element_type=F32)
    dot = lambda p, q: dg(p, q, ((1,), (0,)))
    dot_nt = lambda p, q: dg(p, q, ((1,), (1,)))
    dot_tn = lambda p, q: dg(p, q, ((0,), (0,)))

    wt = lax.broadcasted_iota(jnp.int32, (c, LANES), 0)
    ws = lax.broadcasted_iota(jnp.int32, (c, LANES), 1) % c
    masks = ((wt > ws, wt >= ws), (wt < ws, wt <= ws))
    eye = jnp.where(wt == ws, 1.0, 0.0).astype(F32)
    half = lax.broadcasted_iota(jnp.int32, (c, LANES), 1) < c
    same_head = (lax.broadcasted_iota(jnp.int32, (LANES, LANES), 0) // HEAD_DIM
                 == lax.broadcasted_iota(jnp.int32, (LANES, LANES), 1) // HEAD_DIM)
    hat = lambda t: _hat(t, half)
    refs = ((shf_ref, lwf_ref, bkf_ref, of_ref), (shb_ref, lwb_ref, bkb_ref, ob_ref))

    st = {}

    def prepare(unit):
        d, b, p = unit
        sh_ref, lw_ref, bk_ref, _ = refs[d]
        strict, incl = masks[d]
        cols = slice(p * LANES, (p + 1) * LANES)
        r = f32(sh_ref[0, b, :, cols])
        v = f32(sh_ref[1, b, :, cols])
        kk = f32(sh_ref[2, b, :, cols])
        lw = lw_ref[0, b, :, cols]
        bb = f32(bk_ref[0, 0, b, :, cols])
        kd = f32(bk_ref[0, 1, b, :, cols])
        pre = lw
        s = 1
        while s < c:
            pre = pre + jnp.where(wt >= s, pltpu.roll(pre, s, 0), 0.0)
            s *= 2
        gtot = pre[c - 1:c, :]
        gi = pre if d == 0 else gtot - pre + lw
        en = jnp.exp(-gi)
        ec = jnp.exp(gtot - gi)
        lhs = bf(jnp.concatenate([-kk * jnp.exp(gi - lw), r * jnp.exp(gi)], axis=0))
        rhs = jnp.concatenate([hat(bb * en), hat(kd * en)], axis=0)
        aa = dot_nt(lhs, rhs)
        a_ab = jnp.where(strict, aa[:c, :LANES], 0.0)
        a_ak = jnp.where(strict, aa[:c, LANES:], 0.0)
        ark = bf(jnp.concatenate([jnp.where(incl, aa[c:, :LANES], 0.0),
                                  jnp.where(incl, aa[c:, LANES:], 0.0)], axis=1))
        vh = bf(hat(v))
        sidx = (d * batch + b) * npair + p
        s0 = s_ref[sidx]
        ls = dot_nt(lhs, s0)
        st[unit] = dict(
            sidx=sidx, cols=cols, v=v, s0=s0, ark=ark, vh=vh,
            bk=bf(jnp.concatenate([bb * ec, kd * ec], axis=0)),
            eg=jnp.exp(gtot), pw=a_ab, inv=eye + a_ab,
            rhs_u=ls[:c] + dot(a_ak, vh), rs=ls[c:])

    def square(grp):
        for u in grp:
            st[u]["pw"] = dot(st[u]["pw"], hat(st[u]["pw"]))

    def inverse_round(grp):
        for u in grp:
            e = st[u]
            both = dot(jnp.concatenate([e["pw"], e["inv"]], axis=0), hat(e["pw"]))
            e["pw"] = both[:c]
            e["inv"] = e["inv"] + both[c:]

    def finish_stages(grp):
        def last_product():
            for u in grp:
                st[u]["inv"] = st[u]["inv"] + dot(st[u]["inv"], hat(st[u]["pw"]))

        def solve():
            for u in grp:
                st[u]["u"] = dot(st[u]["inv"], hat(st[u]["rhs_u"]))

        def outputs():
            for (d, b, p) in grp:
                e = st[(d, b, p)]
                o_ref = refs[d][3]
                uvh = jnp.concatenate([bf(hat(e["u"])), e["vh"]], axis=0)
                o_ref[b, :, e["cols"]] = (e["rs"] + dot(e["ark"], uvh)).astype(o_ref.dtype)

        def states():
            for u in grp:
                e = st[u]
                uv = jnp.concatenate([e["u"], e["v"]], axis=0)
                s_ref[e["sidx"]] = (e["s0"] * e["eg"]
                                    + jnp.where(same_head, dot_tn(uv, e["bk"]), 0.0))

        return [last_product, solve, outputs, states]

    units = [(d, b, p) for d in range(2) for b in range(batch) for p in range(npair)]
    groups = [units[g0:g0 + group] for g0 in range(0, len(units), group)]
    rounds = 0
    n = 4
    while n < c:
        rounds += 1
        n *= 2
    for u in groups[0]:
        prepare(u)
    pending = []
    for gi_, grp in enumerate(groups):
        nxt = groups[gi_ + 1] if gi_ + 1 < len(groups) else []
        share = -(-len(nxt) // (rounds + 1))
        square(grp)
        for u in nxt[:share]:
            prepare(u)
        for rd in range(rounds):
            if pending:
                pending.pop(0)()
            inverse_round(grp)
            for u in nxt[(rd + 1) * share:(rd + 2) * share]:
                prepare(u)
        for stage in pending:
            stage()
        pending = finish_stages(grp)
    for stage in pending:
        stage()


def _rwkv_scan(sh, lw, bk, batch, seq, to_bf16=(), group=16, cast_steps=32):
    _, t, width = sh.shape
    c = SCAN_CHUNK
    nc = seq // c
    sh = sh.reshape(3, batch, seq, width)
    lw = lw.reshape(2, batch, seq, width)
    bk = bk.reshape(2, 2, batch, seq, width)
    sh_spec = lambda f: pl.BlockSpec((3, batch, c, width), lambda i: (0, 0, f(i), 0))
    lw_spec = lambda d, f: pl.BlockSpec((1, batch, c, width), lambda i: (d, 0, f(i), 0))
    bk_spec = lambda d, f: pl.BlockSpec((1, 2, batch, c, width), lambda i: (d, 0, 0, f(i), 0))
    o_spec = lambda f: pl.BlockSpec((batch, c, width), lambda i: (0, f(i), 0))
    fwd = lambda i: i
    bwd = lambda i: nc - 1 - i
    cast_specs = [pl.BlockSpec((a.shape[0] // cast_steps, a.shape[1]),
                               lambda i: (jnp.minimum(i, cast_steps - 1), 0)) for a in to_bf16]
    outs = pl.pallas_call(
        functools.partial(_scan_kernel, group=group, n_cast=len(to_bf16), cast_steps=cast_steps),
        grid=(nc,),
        in_specs=[sh_spec(fwd), sh_spec(bwd), lw_spec(0, fwd), lw_spec(1, bwd),
                  bk_spec(0, fwd), bk_spec(1, bwd)] + cast_specs,
        out_specs=[o_spec(fwd), o_spec(bwd)] + cast_specs,
        out_shape=([jax.ShapeDtypeStruct((batch, seq, width), BF16)] * 2
                   + [jax.ShapeDtypeStruct(a.shape, BF16) for a in to_bf16]),
        scratch_shapes=[pltpu.VMEM((2 * batch * (width // LANES), LANES, LANES), F32)],
        compiler_params=_cparams("arbitrary"),
        name="rwkv_scan",
    )(sh, sh, lw, lw, bk, bk, *to_bf16)
    return (outs[0].reshape(t, width), outs[1].reshape(t, width)) + tuple(outs[2:])


def _rwkv_out_rows(of_ref, ob_ref, bonus_ref, gg_ref, gng_ref, gnb_ref):
    width = of_ref.shape[1]
    avg = _head_block_ones(1.0 / HEAD_DIM)
    tiles = []
    for j in range(width // LANES):
        cols = slice(j * LANES, (j + 1) * LANES)
        o = of_ref[:, cols].astype(F32) + ob_ref[:, cols].astype(F32)
        oc = o - _head_sum(o, avg)
        var = _head_sum(oc * oc, avg)
        y = oc * lax.rsqrt(var + GN_EPS) * gng_ref[:, cols] + gnb_ref[:, cols]
        tiles.append(((y + bonus_ref[:, cols].astype(F32))
                      * gg_ref[:, cols].astype(F32)).astype(BF16))
    return jnp.concatenate(tiles, axis=1)


def _mix_out_kernel(ya_ref, of_ref, ob_ref, bonus_ref, gg_ref, gng_ref, gnb_ref, wa_ref, wb_ref,
                    ga_ref, gb_ref, wo_ref, x_ref, g_ref, x1_ref, h2_ref):
    yb = _rwkv_out_rows(of_ref, ob_ref, bonus_ref, gg_ref, gng_ref, gnb_ref)
    pa = jnp.dot(ya_ref[...], wa_ref[...], preferred_element_type=F32)
    pb = jnp.dot(yb, wb_ref[...], preferred_element_type=F32)
    merged = (ga_ref[...].astype(F32) * pa + gb_ref[...].astype(F32) * pb).astype(BF16)
    x1 = x_ref[...] + jnp.dot(merged, wo_ref[...], preferred_element_type=F32)
    x1_ref[...] = x1
    ms = jnp.mean(x1 * x1, axis=-1, keepdims=True)
    h2_ref[...] = (x1 * lax.rsqrt(ms + NORM_EPS) * g_ref[...]).astype(h2_ref.dtype)


def _mix_out(ya, o_f, o_b, bonus, gg, gn_g, gn_b, wa, wb, gates, wo, x, g, tm=256):
    t, k = ya.shape
    d = x.shape[1]
    resident = lambda shape: pl.BlockSpec(shape, lambda i: (0, 0), pipeline_mode=pl.Buffered(1))
    rows = lambda w: pl.BlockSpec((tm, w), lambda i: (i, 0))
    return pl.pallas_call(
        _mix_out_kernel,
        grid=(t // tm,),
        in_specs=[
            rows(k), rows(k), rows(k), rows(k), rows(k),
            pl.BlockSpec((1, k), lambda i: (0, 0)),
            pl.BlockSpec((1, k), lambda i: (0, 0)),
            resident((k, d)),
            resident((k, d)),
            pl.BlockSpec((tm, d), lambda i: (i, 0)),
            pl.BlockSpec((tm, d), lambda i: (i, 1)),
            resident((d, d)),
            rows(d),
            pl.BlockSpec((1, d), lambda i: (0, 0)),
        ],
        out_specs=[rows(d), rows(d)],
        out_shape=[jax.ShapeDtypeStruct((t, d), F32), jax.ShapeDtypeStruct((t, d), BF16)],
        compiler_params=_cparams("parallel"),
        name="mix_out",
    )(ya, o_f, o_b, bonus, gg, gn_g, gn_b, wa, wb, gates, gates, wo, x, g)


def _ffn1_kernel(h_ref, hb_ref, ha_ref, wg_ref, wu_ref, cw_ref, cb_ref, a_ref, *, seq):
    tm = h_ref.shape[0]
    halo = hb_ref.shape[0]
    i = pl.program_id(0)
    tiles_per_seq = seq // tm
    first = (i % tiles_per_seq) == 0
    last = (i % tiles_per_seq) == tiles_per_seq - 1

    h = h_ref[...]
    zero = jnp.zeros_like(hb_ref[...])
    h_ext = jnp.concatenate([jnp.where(first, zero, hb_ref[...]), h,
                             jnp.where(last, zero, ha_ref[...])], axis=0)
    g_ext = jnp.dot(h_ext, wg_ref[...].astype(BF16), preferred_element_type=F32)
    up = jnp.dot(h, wu_ref[...].astype(BF16), preferred_element_type=F32)
    rows = tm + 2 * halo
    g = g_ext[halo:halo + tm]
    prev = pltpu.roll(g_ext, 1, 0)[halo:halo + tm]
    nxt = pltpu.roll(g_ext, rows - 1, 0)[halo:halo + tm]
    gt = prev * cw_ref[0:1, :] + g * cw_ref[1:2, :] + nxt * cw_ref[2:3, :] + cb_ref[...]
    a_ref[...] = (jax.nn.silu(gt) * up).astype(a_ref.dtype)


def _ffn1(h, wg, wu, cw, cb, seq, tm=1024, tn=512, halo=16):
    t, d = h.shape
    f = wg.shape[1]
    nbh = t // halo
    return pl.pallas_call(
        functools.partial(_ffn1_kernel, seq=seq),
        grid=(t // tm, f // tn),
        in_specs=[
            pl.BlockSpec((tm, d), lambda i, j: (i, 0)),
            pl.BlockSpec((halo, d), lambda i, j: (jnp.maximum(i * (tm // halo) - 1, 0), 0)),
            pl.BlockSpec((halo, d), lambda i, j: (jnp.minimum((i + 1) * (tm // halo), nbh - 1), 0)),
            pl.BlockSpec((d, tn), lambda i, j: (0, j)),
            pl.BlockSpec((d, tn), lambda i, j: (0, j)),
            pl.BlockSpec((cw.shape[0], tn), lambda i, j: (0, j)),
            pl.BlockSpec((1, tn), lambda i, j: (0, j)),
        ],
        out_specs=pl.BlockSpec((tm, tn), lambda i, j: (i, j)),
        out_shape=jax.ShapeDtypeStruct((t, f), BF16),
        compiler_params=_cparams("parallel", "arbitrary"),
        name="ffn_gate_up",
    )(h, h, h, wg, wu, cw, cb)


def _ffn2_kernel(a_ref, wd_ref, x1_ref, nf_ref, o_ref):
    xf = x1_ref[...] + jnp.dot(a_ref[...], wd_ref[...], preferred_element_type=F32)
    ms = jnp.mean(xf * xf, axis=-1, keepdims=True)
    o_ref[...] = xf * lax.rsqrt(ms + NORM_EPS) * nf_ref[...]


def _ffn2(act, wd, x1, nf, tm=256):
    t, f = act.shape
    d = x1.shape[1]
    return pl.pallas_call(
        _ffn2_kernel,
        grid=(t // tm,),
        in_specs=[
            pl.BlockSpec((tm, f), lambda i: (i, 0)),
            pl.BlockSpec((f, d), lambda i: (0, 0), pipeline_mode=pl.Buffered(1)),
            pl.BlockSpec((tm, d), lambda i: (i, 0)),
            pl.BlockSpec((1, d), lambda i: (0, 0)),
        ],
        out_specs=pl.BlockSpec((tm, d), lambda i: (i, 0)),
        out_shape=jax.ShapeDtypeStruct((t, d), F32),
        compiler_params=_cparams("parallel"),
        name="ffn_down",
    )(act, wd, x1, nf)


def _pad_cols(a, n):
    return jnp.pad(a, ((0, 0), (0, n - a.shape[1])))


def _pad_rows(a, n):
    return jnp.pad(a, ((0, n - a.shape[0]), (0, 0)))


def _layer(x, batch, seq, norm1_g, w_in, sgu_ln_g, sgu_ln_b, sgu_w, sgu_b,
           mu_prev, mu_next, w0_f, w2_f, a0_f, a2_f, w0_b, w2_b, a0_b, a2_b, k_k, k_a, r_k,
           g2, gn_g, gn_b, w_proj_a, w_proj_b, w_out, norm2_g, ffn_w_gate, ffn_w_up,
           ffn_conv_w, ffn_conv_b, ffn_w_down, norm_out_g):
    d = x.shape[1]
    sgu_width = sgu_ln_g.shape[0]
    width = k_k.shape[0]
    dl = w2_f.shape[0]
    al = a2_f.shape[0]
    feat_w = mu_prev.shape[0]
    feat_pad = -(-feat_w // (4 * LANES)) * (4 * LANES)
    assert dl == HEAD_DIM and al == HEAD_DIM and 2 * dl == LANES
    row = lambda a: a.reshape(1, -1)

    o1 = 2 * sgu_width
    o2 = o1 + feat_w
    w_t = jnp.swapaxes(w_in, 0, 1)
    zw = jnp.zeros_like(w2_f)
    w2c = jnp.concatenate([jnp.concatenate([w2_f, zw], 1),
                           jnp.concatenate([zw, w2_b], 1)], 0).astype(BF16)
    za = jnp.zeros_like(a2_f)
    a2c = jnp.concatenate([jnp.concatenate([a2_f, za], 1),
                           jnp.concatenate([za, a2_b], 1)], 0).astype(BF16)
    g2p = _pad_rows(g2, feat_pad - 3 * width - 2 * LANES).astype(BF16)
    w0c = row(jnp.concatenate([w0_f, w0_b]))
    a0c = row(jnp.concatenate([a0_f, a0_b]))
    mup = _pad_cols(row(mu_prev), feat_pad)
    mun = _pad_cols(row(mu_next), feat_pad)
    bs_full = jnp.repeat(sgu_b.T, sgu_width // sgu_b.shape[0], axis=1)

    h1, y_a = _sgu(x, row(norm1_g), w_t, row(sgu_ln_g), row(sgu_ln_b), sgu_w.astype(BF16),
                   bs_full)
    xs, w_gates = _in_proj(h1, w_t, mup, mun, o1, feat_pad, o2, seq)

    sh, lw, bk, bonus, gg, gates = _rwkv_prep(xs, h1, w_gates, width, w0c, w2c, a0c, a2c, g2p,
                                              row(k_k), row(k_a), row(r_k))
    o_f, o_b, wa_bf, wb_bf, wo_bf, wd_bf = _rwkv_scan(
        sh, lw, bk, batch, seq, to_bf16=(w_proj_a, w_proj_b, w_out, ffn_w_down))

    x1, h2 = _mix_out(y_a, o_f, o_b, bonus, gg, row(gn_g), row(gn_b), wa_bf, wb_bf, gates, wo_bf,
                      x, row(norm2_g))

    act = _ffn1(h2, ffn_w_gate, ffn_w_up, ffn_conv_w, row(ffn_conv_b), seq)
    return _ffn2(act, wd_bf, x1, norm_out_g)


def kernel(x, norm1_g, w_in, sgu_ln_g, sgu_ln_b, sgu_w, sgu_b, rwkv_mu_prev, rwkv_mu_next, rwkv_w0_f, rwkv_w2_f, rwkv_a0_f, rwkv_a2_f, rwkv_w0_b, rwkv_w2_b, rwkv_a0_b, rwkv_a2_b, rwkv_k_k, rwkv_k_a, rwkv_r_k, rwkv_g2, rwkv_gn_g, rwkv_gn_b, w_proj_a, w_proj_b, w_out, norm2_g, ffn_w_gate, ffn_w_up, ffn_conv_w, ffn_conv_b, ffn_w_down, norm_f_g):
    batch, seq, d = x.shape
    depth = norm1_g.shape[0]
    assert depth == 1, "the fused final RMSNorm assumes a single layer"
    xf = x.reshape(batch * seq, d)
    per_layer = (norm1_g, w_in, sgu_ln_g, sgu_ln_b, sgu_w, sgu_b, rwkv_mu_prev, rwkv_mu_next,
                 rwkv_w0_f, rwkv_w2_f, rwkv_a0_f, rwkv_a2_f, rwkv_w0_b, rwkv_w2_b, rwkv_a0_b,
                 rwkv_a2_b, rwkv_k_k, rwkv_k_a, rwkv_r_k, rwkv_g2, rwkv_gn_g, rwkv_gn_b,
                 w_proj_a, w_proj_b, w_out, norm2_g, ffn_w_gate, ffn_w_up, ffn_conv_w,
                 ffn_conv_b, ffn_w_down)
    out = _layer(xf, batch, seq, *(p[0] for p in per_layer), norm_f_g.reshape(1, d))
    return out.reshape(batch, seq, d)
```

```python
import functools

import jax
import jax.numpy as jnp
from jax import lax
from jax.experimental import pallas as pl
from jax.experimental.pallas import tpu as pltpu

F32 = jnp.float32
BF16 = jnp.bfloat16

NORM_EPS = 1e-6
LN_EPS = 1e-5
GN_EPS = 64e-5

HEAD_DIM = 64
LANES = 128
SGU_CHUNK = 128
SCAN_CHUNK = 64
VMEM_LIMIT = 48 * 1024 * 1024


def _cparams(*sem):
    return pltpu.CompilerParams(dimension_semantics=sem, vmem_limit_bytes=VMEM_LIMIT)


def _in_proj_kernel(h_ref, hb_ref, ha_ref, w_ref, mup_ref, mun_ref, xs_ref, hx_ref, *, seq):
    tm = h_ref.shape[0]
    halo = hb_ref.shape[0]
    i = pl.program_id(0)
    j = pl.program_id(1)
    tiles_per_seq = seq // tm
    first = (i % tiles_per_seq) == 0
    last = (i % tiles_per_seq) == tiles_per_seq - 1

    @pl.when(j == 0)
    def _():
        zero = jnp.zeros_like(hb_ref[...])
        hx_ref[0:halo, :] = jnp.where(first, zero, hb_ref[...])
        hx_ref[halo:halo + tm, :] = h_ref[...]
        hx_ref[halo + tm:, :] = jnp.where(last, zero, ha_ref[...])

    f_ext = lax.dot_general(hx_ref[...], w_ref[...].astype(BF16), (((1,), (1,)), ((), ())),
                            preferred_element_type=F32)
    rows = tm + 2 * halo
    f = f_ext[halo:halo + tm]
    prev = pltpu.roll(f_ext, 1, 0)[halo:halo + tm]
    nxt = pltpu.roll(f_ext, rows - 1, 0)[halo:halo + tm]
    mup = mup_ref[...]
    mun = mun_ref[...]
    xs_ref[...] = (f * (1.0 - mup - mun) + mup * prev + mun * nxt).astype(xs_ref.dtype)


def _in_proj(h, w_t, mup, mun, row0, w_ft, seq, tm=2048, tn=512, halo=16):
    t, d = h.shape
    n_ft = w_ft // tn
    nbh = t // halo
    assert row0 % tn == 0
    return pl.pallas_call(
        functools.partial(_in_proj_kernel, seq=seq),
        grid=(t // tm, n_ft),
        in_specs=[
            pl.BlockSpec((tm, d), lambda i, j: (i, 0), pipeline_mode=pl.Buffered(1)),
            pl.BlockSpec((halo, d), lambda i, j: (jnp.maximum(i * (tm // halo) - 1, 0), 0)),
            pl.BlockSpec((halo, d), lambda i, j: (jnp.minimum((i + 1) * (tm // halo), nbh - 1), 0)),
            pl.BlockSpec((tn, d), lambda i, j: (row0 // tn + j, 0)),
            pl.BlockSpec((1, tn), lambda i, j: (0, j)),
            pl.BlockSpec((1, tn), lambda i, j: (0, j)),
        ],
        out_specs=pl.BlockSpec((tm, tn), lambda i, j: (i, j)),
        out_shape=jax.ShapeDtypeStruct((t, n_ft * tn), BF16),
        scratch_shapes=[pltpu.VMEM((tm + 2 * halo, d), BF16)],
        compiler_params=_cparams("parallel", "arbitrary"),
        name="in_proj",
    )(h, h, h, w_t, mup, mun)


def _gelu(x):
    return 0.5 * x * (1.0 + lax.erf(x * (2.0 ** -0.5)))


def _sgu_kernel(x_ref, g_ref, w_ref, lng_ref, lnb_ref, ws_ref, bs_ref, ga0_ref, ga1_ref, gb0_ref,
                gb1_ref, h_ref, o_ref, wg_ref, wb_ref, *, width, groups):
    @pl.when(pl.program_id(0) == 0)
    def _():
        wb_ref[...] = w_ref[...].astype(wb_ref.dtype)

    cr = ga0_ref.shape[0]
    wg_ref[0, 0:cr, :] = ga0_ref[...].astype(wg_ref.dtype)
    wg_ref[0, cr:, :] = ga1_ref[...].astype(wg_ref.dtype)
    wg_ref[1, 0:cr, :] = gb0_ref[...].astype(wg_ref.dtype)
    wg_ref[1, cr:, :] = gb1_ref[...].astype(wg_ref.dtype)

    tm = x_ref.shape[0]
    nt = (((1,), (1,)), ((), ()))
    x = x_ref[...]
    ms = jnp.mean(x * x, axis=-1, keepdims=True)
    h = (x * lax.rsqrt(ms + NORM_EPS) * g_ref[...]).astype(h_ref.dtype)
    h_ref[...] = h
    v = lax.dot_general(h, wb_ref[width:, :], nt, preferred_element_type=F32)
    gv = _gelu(v)
    mu = jnp.mean(gv, axis=-1, keepdims=True)
    vc = gv - mu
    var = jnp.mean(vc * vc, axis=-1, keepdims=True)
    vn = (vc * lax.rsqrt(var + LN_EPS) * lng_ref[...] + lnb_ref[...]).astype(BF16)
    uv = lax.dot_general(h, wb_ref[0:width, :], nt, preferred_element_type=F32)
    gd = width // groups
    for c in range(tm // SGU_CHUNK):
        rows = slice(c * SGU_CHUNK, (c + 1) * SGU_CHUNK)
        for g in range(groups):
            cols = slice(g * gd, (g + 1) * gd)
            mixed = jnp.dot(ws_ref[g], vn[rows, cols], preferred_element_type=F32)
            gu = _gelu(uv[rows, cols])
            o_ref[rows, cols] = (gu * (mixed + bs_ref[:, cols])).astype(o_ref.dtype)


def _sgu(x, g, w_t, ln_g, ln_b, ws, bs_full, gate_row0, tm=256, cast_rows=32):
    t, d = x.shape
    width = ln_g.shape[1]
    groups = ws.shape[0]
    steps = t // tm
    half = (w_t.shape[0] - gate_row0) // 2
    assert gate_row0 % cast_rows == 0 and half == 2 * cast_rows * steps
    gate_rows = lambda r0, k: pl.BlockSpec((cast_rows, d),
                                           lambda i: (r0 // cast_rows + 2 * i + k, 0))
    return pl.pallas_call(
        functools.partial(_sgu_kernel, width=width, groups=groups),
        grid=(steps,),
        in_specs=[
            pl.BlockSpec((tm, d), lambda i: (i, 0)),
            pl.BlockSpec((1, d), lambda i: (0, 0)),
            pl.BlockSpec((2 * width, d), lambda i: (0, 0), pipeline_mode=pl.Buffered(1)),
            pl.BlockSpec((1, width), lambda i: (0, 0)),
            pl.BlockSpec((1, width), lambda i: (0, 0)),
            pl.BlockSpec(ws.shape, lambda i: (0, 0, 0)),
            pl.BlockSpec(bs_full.shape, lambda i: (0, 0)),
            gate_rows(gate_row0, 0), gate_rows(gate_row0, 1),
            gate_rows(gate_row0 + half, 0), gate_rows(gate_row0 + half, 1),
        ],
        out_specs=[pl.BlockSpec((tm, d), lambda i: (i, 0)),
                   pl.BlockSpec((tm, width), lambda i: (i, 0)),
                   pl.BlockSpec((2, 2 * cast_rows, d), lambda i: (0, i, 0))],
        out_shape=[jax.ShapeDtypeStruct((t, d), BF16), jax.ShapeDtypeStruct((t, width), BF16),
                   jax.ShapeDtypeStruct((2, half, d), BF16)],
        scratch_shapes=[pltpu.VMEM((2 * width, d), BF16)],
        compiler_params=_cparams("arbitrary"),
        name="sgu_mixer",
    )(x, g, w_t, ln_g, ln_b, ws, bs_full, w_t, w_t, w_t, w_t)


def _head_block_ones(scale):
    r = lax.broadcasted_iota(jnp.int32, (LANES, LANES), 0) // HEAD_DIM
    c = lax.broadcasted_iota(jnp.int32, (LANES, LANES), 1) // HEAD_DIM
    return jnp.where(r == c, scale, 0.0).astype(BF16)


def _head_sum(x, bd):
    return jnp.dot(x.astype(BF16), bd, preferred_element_type=F32)


def _prep_kernel(xs_ref, w0_ref, w2_ref, a0_ref, a2_ref, g2_ref, kk_ref, ka_ref, rk_ref,
                 h_ref, wg_ref, sh_ref, lw_ref, bk_ref, bonus_ref, gg_ref, gt_ref, *, width):
    gt_ref[...] = jax.nn.sigmoid(
        lax.dot_general(h_ref[...], wg_ref[...], (((1,), (1,)), ((), ())),
                        preferred_element_type=F32)).astype(gt_ref.dtype)

    xs = xs_ref[...].astype(F32)
    r = xs[:, 0:width]
    k = xs[:, width:2 * width]
    v = xs[:, 2 * width:3 * width]
    o = 3 * width
    lo_w = xs[:, o:o + LANES]
    lo_a = xs[:, o + LANES:o + 2 * LANES]
    lo_g = xs[:, o + 2 * LANES:]

    wpre = w0_ref[...] + jnp.dot(jnp.tanh(lo_w).astype(BF16), w2_ref[...],
                                 preferred_element_type=F32)
    lw = -(2.718281828459045 ** -0.5) * jax.nn.sigmoid(wpre)
    a = jax.nn.sigmoid(a0_ref[...] + jnp.dot(lo_a.astype(BF16), a2_ref[...],
                                             preferred_element_type=F32))
    gg_ref[...] = jnp.dot(jax.nn.sigmoid(lo_g).astype(BF16), g2_ref[...],
                          preferred_element_type=F32).astype(gg_ref.dtype)
    lw_ref[0] = lw[:, :width]
    lw_ref[1] = lw[:, width:]

    bd = _head_block_ones(1.0)
    kk = k * kk_ref[...]
    sh_ref[0] = xs_ref[:, 0:width]
    sh_ref[1] = xs_ref[:, 2 * width:3 * width]
    for j in range(width // LANES):
        cols = slice(j * LANES, (j + 1) * LANES)
        kkj = kk[:, cols]
        ss = _head_sum(kkj * kkj, bd)
        kkn = kkj * lax.rsqrt(jnp.maximum(ss, 1e-24))
        sh_ref[2, :, cols] = kkn.astype(sh_ref.dtype)
        ksum = jnp.zeros_like(kkj)
        for d in range(2):
            dcols = slice(d * width + j * LANES, d * width + (j + 1) * LANES)
            a_d = a[:, dcols]
            k_d = k[:, cols] * (1.0 + (a_d - 1.0) * ka_ref[:, cols])
            bk_ref[d, 0, :, cols] = (kkn * a_d).astype(bk_ref.dtype)
            bk_ref[d, 1, :, cols] = k_d.astype(bk_ref.dtype)
            ksum = ksum + k_d
        bsum = _head_sum(r[:, cols] * ksum * rk_ref[:, cols], bd)
        bonus_ref[:, cols] = (bsum * v[:, cols]).astype(bonus_ref.dtype)


def _rwkv_prep(xs, h, wg, width, w0c, w2c, a0c, a2c, g2p, k_k, k_a, r_k, tm=128):
    t, fw = xs.shape
    _, ngh, d = wg.shape
    r = lambda j, i: 2 * i + j
    row = lambda j, i: (0, 0)
    resident = lambda a: pl.BlockSpec(a.shape, row, pipeline_mode=pl.Buffered(1))
    rows = lambda w: pl.BlockSpec((tm, w), lambda j, i: (r(j, i), 0))
    return pl.pallas_call(
        functools.partial(_prep_kernel, width=width),
        grid=(2, t // (2 * tm)),
        in_specs=[
            rows(fw),
            pl.BlockSpec(w0c.shape, row),
            resident(w2c),
            pl.BlockSpec(a0c.shape, row),
            resident(a2c),
            resident(g2p),
            pl.BlockSpec((1, width), row),
            pl.BlockSpec((1, width), row),
            pl.BlockSpec((1, width), row),
            pl.BlockSpec((2 * tm, d), lambda j, i: (i, 0)),
            pl.BlockSpec((None, ngh, d), lambda j, i: (j, 0, 0), pipeline_mode=pl.Buffered(1)),
        ],
        out_specs=[
            pl.BlockSpec((3, tm, width), lambda j, i: (0, r(j, i), 0)),
            pl.BlockSpec((2, tm, width), lambda j, i: (0, r(j, i), 0)),
            pl.BlockSpec((2, 2, tm, width), lambda j, i: (0, 0, r(j, i), 0)),
            rows(width),
            rows(width),
            pl.BlockSpec((2 * tm, ngh), lambda j, i: (i, j)),
        ],
        out_shape=[
            jax.ShapeDtypeStruct((3, t, width), BF16),
            jax.ShapeDtypeStruct((2, t, width), F32),
            jax.ShapeDtypeStruct((2, 2, t, width), BF16),
            jax.ShapeDtypeStruct((t, width), BF16),
            jax.ShapeDtypeStruct((t, width), BF16),
            jax.ShapeDtypeStruct((t, 2 * ngh), BF16),
        ],
        compiler_params=_cparams("arbitrary", "arbitrary"),
        name="rwkv_prep",
    )(xs, w0c, w2c, a0c, a2c, g2p, k_k, k_a, r_k, h, wg)


def _hat(x, first_half):
    return jnp.concatenate([jnp.where(first_half, x, 0.0), jnp.where(first_half, 0.0, x)],
                           axis=0)


def _scan_kernel(shf_ref, shb_ref, lwf_ref, lwb_ref, bkf_ref, bkb_ref, *rest, group, n_cast,
                 cast_steps):
    cast_in, (of_ref, ob_ref), cast_out, s_ref = (rest[:n_cast], rest[n_cast:n_cast + 2],
                                                  rest[n_cast + 2:2 * n_cast + 2], rest[-1])
    c = SCAN_CHUNK
    batch, width = shf_ref.shape[1], shf_ref.shape[3]
    npair = width // LANES
    assert 2 * c == LANES and c == HEAD_DIM

    @pl.when(pl.program_id(0) == 0)
    def _():
        s_ref[...] = jnp.zeros_like(s_ref)

    @pl.when(pl.program_id(0) < cast_steps)
    def _():
        for src, dst in zip(cast_in, cast_out):
            dst[...] = src[...].astype(dst.dtype)

    bf = lambda t: t.astype(BF16)
    f32 = lambda t: t.astype(F32)
    dg = lambda p, q, dims: lax.dot_general(bf(p), bf(q), (dims, ((), ())),
                                            preferred_element_type=F32)
    dot = lambda p, q: dg(p, q, ((1,), (0,)))
    dot_nt = lambda p, q: dg(p, q, ((1,), (1,)))
    dot_tn = lambda p, q: dg(p, q, ((0,), (0,)))

    wt = lax.broadcasted_iota(jnp.int32, (c, LANES), 0)
    ws = lax.broadcasted_iota(jnp.int32, (c, LANES), 1) % c
    masks = ((wt > ws, wt >= ws), (wt < ws, wt <= ws))
    eye = jnp.where(wt == ws, 1.0, 0.0).astype(F32)
    half = lax.broadcasted_iota(jnp.int32, (c, LANES), 1) < c
    same_head = (lax.broadcasted_iota(jnp.int32, (LANES, LANES), 0) // HEAD_DIM
                 == lax.broadcasted_iota(jnp.int32, (LANES, LANES), 1) // HEAD_DIM)
    hat = lambda t: _hat(t, half)
    refs = ((shf_ref, lwf_ref, bkf_ref, of_ref), (shb_ref, lwb_ref, bkb_ref, ob_ref))

    st = {}
    slots = shf_ref.shape[2] // c

    def prepare(unit):
        slot, d, b, p = unit
        sh_ref, lw_ref, bk_ref, _ = refs[d]
        strict, incl = masks[d]
        cols = slice(p * LANES, (p + 1) * LANES)
        r0 = (slot if d == 0 else slots - 1 - slot) * c
        rows = slice(r0, r0 + c)
        r = f32(sh_ref[0, b, rows, cols])
        v = f32(sh_ref[1, b, rows, cols])
        kk = f32(sh_ref[2, b, rows, cols])
        lw = lw_ref[0, b, rows, cols]
        bb = f32(bk_ref[0, 0, b, rows, cols])
        kd = f32(bk_ref[0, 1, b, rows, cols])
        pre = lw
        s = 1
        while s < c:
            pre = pre + jnp.where(wt >= s, pltpu.roll(pre, s, 0), 0.0)
            s *= 2
        gtot = pre[c - 1:c, :]
        gi = pre if d == 0 else gtot - pre + lw
        en = jnp.exp(-gi)
        ec = jnp.exp(gtot - gi)
        lhs = bf(jnp.concatenate([-kk * jnp.exp(gi - lw), r * jnp.exp(gi)], axis=0))
        rhs = jnp.concatenate([hat(bb * en), hat(kd * en)], axis=0)
        aa = dot_nt(lhs, rhs)
        a_ab = jnp.where(strict, aa[:c, :LANES], 0.0)
        a_ak = jnp.where(strict, aa[:c, LANES:], 0.0)
        ark = bf(jnp.concatenate([jnp.where(incl, aa[c:, :LANES], 0.0),
                                  jnp.where(incl, aa[c:, LANES:], 0.0)], axis=1))
        vh = bf(hat(v))
        sidx = (d * batch + b) * npair + p
        s0 = s_ref[sidx]
        ls = dot_nt(lhs, s0)
        st[unit] = dict(
            sidx=sidx, rows=rows, cols=cols, v=v, s0=s0, ark=ark, vh=vh,
            bk=bf(jnp.concatenate([bb * ec, kd * ec], axis=0)),
            eg=jnp.exp(gtot), pw=a_ab, inv=eye + a_ab,
            rhs_u=ls[:c] + dot(a_ak, vh), rs=ls[c:])

    def square(grp):
        for u in grp:
            st[u]["pw"] = dot(st[u]["pw"], hat(st[u]["pw"]))

    def inverse_round(grp):
        for u in grp:
            e = st[u]
            both = dot(jnp.concatenate([e["pw"], e["inv"]], axis=0), hat(e["pw"]))
            e["pw"] = both[:c]
            e["inv"] = e["inv"] + both[c:]

    def finish_stages(grp):
        def last_product():
            for u in grp:
                st[u]["inv"] = st[u]["inv"] + dot(st[u]["inv"], hat(st[u]["pw"]))

        def solve():
            for u in grp:
                st[u]["u"] = dot(st[u]["inv"], hat(st[u]["rhs_u"]))

        def outputs():
            for u in grp:
                e = st[u]
                o_ref = refs[u[1]][3]
                uvh = jnp.concatenate([bf(hat(e["u"])), e["vh"]], axis=0)
                o_ref[u[2], e["rows"], e["cols"]] = (e["rs"] + dot(e["ark"], uvh)).astype(o_ref.dtype)

        def states():
            for u in grp:
                e = st[u]
                uv = jnp.concatenate([e["u"], e["v"]], axis=0)
                s_ref[e["sidx"]] = (e["s0"] * e["eg"]
                                    + jnp.where(same_head, dot_tn(uv, e["bk"]), 0.0))

        return [last_product, solve, outputs, states]

    units = [(slot, d, b, p) for slot in range(slots) for d in range(2)
             for b in range(batch) for p in range(npair)]
    assert group == batch * npair, "one group per (chunk slot, direction)"
    groups = [units[g0:g0 + group] for g0 in range(0, len(units), group)]
    rounds = 0
    n = 4
    while n < c:
        rounds += 1
        n *= 2
    for u in groups[0]:
        prepare(u)
    for gi_, grp in enumerate(groups):
        nxt = groups[gi_ + 1] if gi_ + 1 < len(groups) else []
        share = -(-len(nxt) // (rounds + 1))
        square(grp)
        for u in nxt[:share]:
            prepare(u)
        for rd in range(rounds):
            inverse_round(grp)
            for u in nxt[(rd + 1) * share:(rd + 2) * share]:
                prepare(u)
        for stage in finish_stages(grp):
            stage()


def _rwkv_scan(sh, lw, bk, batch, seq, to_bf16=(), slots=2):
    _, t, width = sh.shape
    rows = slots * SCAN_CHUNK
    steps = seq // rows
    sh = sh.reshape(3, batch, seq, width)
    lw = lw.reshape(2, batch, seq, width)
    bk = bk.reshape(2, 2, batch, seq, width)
    sh_spec = lambda f: pl.BlockSpec((3, batch, rows, width), lambda i: (0, 0, f(i), 0))
    lw_spec = lambda d, f: pl.BlockSpec((1, batch, rows, width), lambda i: (d, 0, f(i), 0))
    bk_spec = lambda d, f: pl.BlockSpec((1, 2, batch, rows, width), lambda i: (d, 0, 0, f(i), 0))
    o_spec = lambda f: pl.BlockSpec((batch, rows, width), lambda i: (0, f(i), 0))
    fwd = lambda i: i
    bwd = lambda i: steps - 1 - i
    cast_specs = [pl.BlockSpec((a.shape[0] // steps, a.shape[1]), lambda i: (i, 0))
                  for a in to_bf16]
    outs = pl.pallas_call(
        functools.partial(_scan_kernel, group=batch * (width // LANES), n_cast=len(to_bf16),
                          cast_steps=steps),
        grid=(steps,),
        in_specs=[sh_spec(fwd), sh_spec(bwd), lw_spec(0, fwd), lw_spec(1, bwd),
                  bk_spec(0, fwd), bk_spec(1, bwd)] + cast_specs,
        out_specs=[o_spec(fwd), o_spec(bwd)] + cast_specs,
        out_shape=([jax.ShapeDtypeStruct((batch, seq, width), BF16)] * 2
                   + [jax.ShapeDtypeStruct(a.shape, BF16) for a in to_bf16]),
        scratch_shapes=[pltpu.VMEM((2 * batch * (width // LANES), LANES, LANES), F32)],
        compiler_params=_cparams("arbitrary"),
        name="rwkv_scan",
    )(sh, sh, lw, lw, bk, bk, *to_bf16)
    return (outs[0].reshape(t, width), outs[1].reshape(t, width)) + tuple(outs[2:])


def _rwkv_out_rows(of_ref, ob_ref, bonus_ref, gg_ref, gng_ref, gnb_ref):
    width = of_ref.shape[1]
    avg = _head_block_ones(1.0 / HEAD_DIM)
    tiles = []
    for j in range(width // LANES):
        cols = slice(j * LANES, (j + 1) * LANES)
        o = of_ref[:, cols].astype(F32) + ob_ref[:, cols].astype(F32)
        oc = o - _head_sum(o, avg)
        var = _head_sum(oc * oc, avg)
        y = oc * lax.rsqrt(var + GN_EPS) * gng_ref[:, cols] + gnb_ref[:, cols]
        tiles.append(((y + bonus_ref[:, cols].astype(F32))
                      * gg_ref[:, cols].astype(F32)).astype(BF16))
    return jnp.concatenate(tiles, axis=1)


def _mix_out_kernel(ya_ref, of_ref, ob_ref, bonus_ref, gg_ref, gng_ref, gnb_ref, wa_ref, wb_ref,
                    ga_ref, gb_ref, wo_ref, x_ref, g_ref, x1_ref, h2_ref):
    yb = _rwkv_out_rows(of_ref, ob_ref, bonus_ref, gg_ref, gng_ref, gnb_ref)
    pa = jnp.dot(ya_ref[...], wa_ref[...], preferred_element_type=F32)
    pb = jnp.dot(yb, wb_ref[...], preferred_element_type=F32)
    merged = (ga_ref[...].astype(F32) * pa + gb_ref[...].astype(F32) * pb).astype(BF16)
    x1 = x_ref[...] + jnp.dot(merged, wo_ref[...], preferred_element_type=F32)
    x1_ref[...] = x1
    ms = jnp.mean(x1 * x1, axis=-1, keepdims=True)
    h2_ref[...] = (x1 * lax.rsqrt(ms + NORM_EPS) * g_ref[...]).astype(h2_ref.dtype)


def _mix_out(ya, o_f, o_b, bonus, gg, gn_g, gn_b, wa, wb, gates, wo, x, g, tm=256):
    t, k = ya.shape
    d = x.shape[1]
    resident = lambda shape: pl.BlockSpec(shape, lambda i: (0, 0), pipeline_mode=pl.Buffered(1))
    rows = lambda w: pl.BlockSpec((tm, w), lambda i: (i, 0))
    return pl.pallas_call(
        _mix_out_kernel,
        grid=(t // tm,),
        in_specs=[
            rows(k), rows(k), rows(k), rows(k), rows(k),
            pl.BlockSpec((1, k), lambda i: (0, 0)),
            pl.BlockSpec((1, k), lambda i: (0, 0)),
            resident((k, d)),
            resident((k, d)),
            pl.BlockSpec((tm, d), lambda i: (i, 0)),
            pl.BlockSpec((tm, d), lambda i: (i, 1)),
            resident((d, d)),
            rows(d),
            pl.BlockSpec((1, d), lambda i: (0, 0)),
        ],
        out_specs=[rows(d), rows(d)],
        out_shape=[jax.ShapeDtypeStruct((t, d), F32), jax.ShapeDtypeStruct((t, d), BF16)],
        compiler_params=_cparams("parallel"),
        name="mix_out",
    )(ya, o_f, o_b, bonus, gg, gn_g, gn_b, wa, wb, gates, gates, wo, x, g)


def _ffn1_kernel(h_ref, hb_ref, ha_ref, wg_ref, wu_ref, cw_ref, cb_ref, a_ref, *, seq):
    tm = h_ref.shape[0]
    halo = hb_ref.shape[0]
    i = pl.program_id(0)
    tiles_per_seq = seq // tm
    first = (i % tiles_per_seq) == 0
    last = (i % tiles_per_seq) == tiles_per_seq - 1

    h = h_ref[...]
    zero = jnp.zeros_like(hb_ref[...])
    h_ext = jnp.concatenate([jnp.where(first, zero, hb_ref[...]), h,
                             jnp.where(last, zero, ha_ref[...])], axis=0)
    g_ext = jnp.dot(h_ext, wg_ref[...].astype(BF16), preferred_element_type=F32)
    up = jnp.dot(h, wu_ref[...].astype(BF16), preferred_element_type=F32)
    rows = tm + 2 * halo
    g = g_ext[halo:halo + tm]
    prev = pltpu.roll(g_ext, 1, 0)[halo:halo + tm]
    nxt = pltpu.roll(g_ext, rows - 1, 0)[halo:halo + tm]
    gt = prev * cw_ref[0:1, :] + g * cw_ref[1:2, :] + nxt * cw_ref[2:3, :] + cb_ref[...]
    a_ref[...] = (jax.nn.silu(gt) * up).astype(a_ref.dtype)


def _ffn1(h, wg, wu, cw, cb, seq, tm=1024, tn=512, halo=16):
    t, d = h.shape
    f = wg.shape[1]
    nbh = t // halo
    return pl.pallas_call(
        functools.partial(_ffn1_kernel, seq=seq),
        grid=(t // tm, f // tn),
        in_specs=[
            pl.BlockSpec((tm, d), lambda i, j: (i, 0)),
            pl.BlockSpec((halo, d), lambda i, j: (jnp.maximum(i * (tm // halo) - 1, 0), 0)),
            pl.BlockSpec((halo, d), lambda i, j: (jnp.minimum((i + 1) * (tm // halo), nbh - 1), 0)),
            pl.BlockSpec((d, tn), lambda i, j: (0, j)),
            pl.BlockSpec((d, tn), lambda i, j: (0, j)),
            pl.BlockSpec((cw.shape[0], tn), lambda i, j: (0, j)),
            pl.BlockSpec((1, tn), lambda i, j: (0, j)),
        ],
        out_specs=pl.BlockSpec((tm, tn), lambda i, j: (i, j)),
        out_shape=jax.ShapeDtypeStruct((t, f), BF16),
        compiler_params=_cparams("parallel", "arbitrary"),
        name="ffn_gate_up",
    )(h, h, h, wg, wu, cw, cb)


def _ffn2_kernel(a_ref, wd_ref, x1_ref, nf_ref, o_ref):
    xf = x1_ref[...] + jnp.dot(a_ref[...], wd_ref[...], preferred_element_type=F32)
    ms = jnp.mean(xf * xf, axis=-1, keepdims=True)
    o_ref[...] = xf * lax.rsqrt(ms + NORM_EPS) * nf_ref[...]


def _ffn2(act, wd, x1, nf, tm=256):
    t, f = act.shape
    d = x1.shape[1]
    return pl.pallas_call(
        _ffn2_kernel,
        grid=(t // tm,),
        in_specs=[
            pl.BlockSpec((tm, f), lambda i: (i, 0)),
            pl.BlockSpec((f, d), lambda i: (0, 0), pipeline_mode=pl.Buffered(1)),
            pl.BlockSpec((tm, d), lambda i: (i, 0)),
            pl.BlockSpec((1, d), lambda i: (0, 0)),
        ],
        out_specs=pl.BlockSpec((tm, d), lambda i: (i, 0)),
        out_shape=jax.ShapeDtypeStruct((t, d), F32),
        compiler_params=_cparams("parallel"),
        name="ffn_down",
    )(act, wd, x1, nf)


def _pad_cols(a, n):
    return jnp.pad(a, ((0, 0), (0, n - a.shape[1])))


def _pad_rows(a, n):
    return jnp.pad(a, ((0, n - a.shape[0]), (0, 0)))


def _layer(x, batch, seq, norm1_g, w_in, sgu_ln_g, sgu_ln_b, sgu_w, sgu_b,
           mu_prev, mu_next, w0_f, w2_f, a0_f, a2_f, w0_b, w2_b, a0_b, a2_b, k_k, k_a, r_k,
           g2, gn_g, gn_b, w_proj_a, w_proj_b, w_out, norm2_g, ffn_w_gate, ffn_w_up,
           ffn_conv_w, ffn_conv_b, ffn_w_down, norm_out_g):
    d = x.shape[1]
    sgu_width = sgu_ln_g.shape[0]
    width = k_k.shape[0]
    dl = w2_f.shape[0]
    al = a2_f.shape[0]
    feat_w = mu_prev.shape[0]
    feat_pad = -(-feat_w // (4 * LANES)) * (4 * LANES)
    assert dl == HEAD_DIM and al == HEAD_DIM and 2 * dl == LANES
    row = lambda a: a.reshape(1, -1)

    o1 = 2 * sgu_width
    o2 = o1 + feat_w
    w_t = jnp.swapaxes(w_in, 0, 1)
    zw = jnp.zeros_like(w2_f)
    w2c = jnp.concatenate([jnp.concatenate([w2_f, zw], 1),
                           jnp.concatenate([zw, w2_b], 1)], 0).astype(BF16)
    za = jnp.zeros_like(a2_f)
    a2c = jnp.concatenate([jnp.concatenate([a2_f, za], 1),
                           jnp.concatenate([za, a2_b], 1)], 0).astype(BF16)
    g2p = _pad_rows(g2, feat_pad - 3 * width - 2 * LANES).astype(BF16)
    w0c = row(jnp.concatenate([w0_f, w0_b]))
    a0c = row(jnp.concatenate([a0_f, a0_b]))
    mup = _pad_cols(row(mu_prev), feat_pad)
    mun = _pad_cols(row(mu_next), feat_pad)
    bs_full = jnp.repeat(sgu_b.T, sgu_width // sgu_b.shape[0], axis=1)

    h1, y_a, w_gates = _sgu(x, row(norm1_g), w_t, row(sgu_ln_g), row(sgu_ln_b),
                            sgu_w.astype(BF16), bs_full, o2)
    xs = _in_proj(h1, w_t, mup, mun, o1, feat_pad, seq)

    sh, lw, bk, bonus, gg, gates = _rwkv_prep(xs, h1, w_gates, width, w0c, w2c, a0c, a2c, g2p,
                                              row(k_k), row(k_a), row(r_k))
    o_f, o_b, wa_bf, wb_bf, wo_bf, wd_bf = _rwkv_scan(
        sh, lw, bk, batch, seq, to_bf16=(w_proj_a, w_proj_b, w_out, ffn_w_down))

    x1, h2 = _mix_out(y_a, o_f, o_b, bonus, gg, row(gn_g), row(gn_b), wa_bf, wb_bf, gates, wo_bf,
                      x, row(norm2_g))

    act = _ffn1(h2, ffn_w_gate, ffn_w_up, ffn_conv_w, row(ffn_conv_b), seq)
    return _ffn2(act, wd_bf, x1, norm_out_g)


def kernel(x, norm1_g, w_in, sgu_ln_g, sgu_ln_b, sgu_w, sgu_b, rwkv_mu_prev, rwkv_mu_next, rwkv_w0_f, rwkv_w2_f, rwkv_a0_f, rwkv_a2_f, rwkv_w0_b, rwkv_w2_b, rwkv_a0_b, rwkv_a2_b, rwkv_k_k, rwkv_k_a, rwkv_r_k, rwkv_g2, rwkv_gn_g, rwkv_gn_b, w_proj_a, w_proj_b, w_out, norm2_g, ffn_w_gate, ffn_w_up, ffn_conv_w, ffn_conv_b, ffn_w_down, norm_f_g):
    batch, seq, d = x.shape
    depth = norm1_g.shape[0]
    assert depth == 1, "the fused final RMSNorm assumes a single layer"
    xf = x.reshape(batch * seq, d)
    per_layer = (norm1_g, w_in, sgu_ln_g, sgu_ln_b, sgu_w, sgu_b, rwkv_mu_prev, rwkv_mu_next,
                 rwkv_w0_f, rwkv_w2_f, rwkv_a0_f, rwkv_a2_f, rwkv_w0_b, rwkv_w2_b, rwkv_a0_b,
                 rwkv_a2_b, rwkv_k_k, rwkv_k_a, rwkv_r_k, rwkv_g2, rwkv_gn_g, rwkv_gn_b,
                 w_proj_a, w_proj_b, w_out, norm2_g, ffn_w_gate, ffn_w_up, ffn_conv_w,
                 ffn_conv_b, ffn_w_down)
    out = _layer(xf, batch, seq, *(p[0] for p in per_layer), norm_f_g.reshape(1, d))
    return out.reshape(batch, seq, d)
```

```python
import functools

import jax
import jax.numpy as jnp
from jax import lax
from jax.experimental import pallas as pl
from jax.experimental.pallas import tpu as pltpu

F32 = jnp.float32
BF16 = jnp.bfloat16

NORM_EPS = 1e-6
LN_EPS = 1e-5
GN_EPS = 64e-5

HEAD_DIM = 64
LANES = 128
SGU_CHUNK = 128
SCAN_CHUNK = 64
VMEM_LIMIT = 48 * 1024 * 1024
VMEM_LIMIT_RESIDENT = 56 * 1024 * 1024


def _cparams(*sem, vmem_limit=VMEM_LIMIT):
    return pltpu.CompilerParams(dimension_semantics=sem, vmem_limit_bytes=vmem_limit)


def _in_proj_kernel(h_ref, hb_ref, ha_ref, w_ref, mup_ref, mun_ref, xs_ref, hx_ref, *, seq):
    tm = h_ref.shape[0]
    halo = hb_ref.shape[0]
    i = pl.program_id(0)
    j = pl.program_id(1)
    tiles_per_seq = seq // tm
    first = (i % tiles_per_seq) == 0
    last = (i % tiles_per_seq) == tiles_per_seq - 1

    @pl.when(j == 0)
    def _():
        zero = jnp.zeros_like(hb_ref[...])
        hx_ref[0:halo, :] = jnp.where(first, zero, hb_ref[...])
        hx_ref[halo:halo + tm, :] = h_ref[...]
        hx_ref[halo + tm:, :] = jnp.where(last, zero, ha_ref[...])

    f_ext = lax.dot_general(hx_ref[...], w_ref[...].astype(BF16), (((1,), (1,)), ((), ())),
                            preferred_element_type=F32)
    rows = tm + 2 * halo
    f = f_ext[halo:halo + tm]
    prev = pltpu.roll(f_ext, 1, 0)[halo:halo + tm]
    nxt = pltpu.roll(f_ext, rows - 1, 0)[halo:halo + tm]
    mup = mup_ref[...]
    mun = mun_ref[...]
    xs_ref[...] = (f * (1.0 - mup - mun) + mup * prev + mun * nxt).astype(xs_ref.dtype)


def _in_proj(h, w_t, mup, mun, row0, w_ft, seq, tm=2048, tn=512, halo=16):
    t, d = h.shape
    n_ft = w_ft // tn
    nbh = t // halo
    assert row0 % tn == 0
    return pl.pallas_call(
        functools.partial(_in_proj_kernel, seq=seq),
        grid=(t // tm, n_ft),
        in_specs=[
            pl.BlockSpec((tm, d), lambda i, j: (i, 0), pipeline_mode=pl.Buffered(1)),
            pl.BlockSpec((halo, d), lambda i, j: (jnp.maximum(i * (tm // halo) - 1, 0), 0)),
            pl.BlockSpec((halo, d), lambda i, j: (jnp.minimum((i + 1) * (tm // halo), nbh - 1), 0)),
            pl.BlockSpec((tn, d), lambda i, j: (row0 // tn + j, 0)),
            pl.BlockSpec((1, tn), lambda i, j: (0, j)),
            pl.BlockSpec((1, tn), lambda i, j: (0, j)),
        ],
        out_specs=pl.BlockSpec((tm, tn), lambda i, j: (i, j)),
        out_shape=jax.ShapeDtypeStruct((t, n_ft * tn), BF16),
        scratch_shapes=[pltpu.VMEM((tm + 2 * halo, d), BF16)],
        compiler_params=_cparams("parallel", "arbitrary"),
        name="in_proj",
    )(h, h, h, w_t, mup, mun)


def _gelu(x):
    return 0.5 * x * (1.0 + lax.erf(x * (2.0 ** -0.5)))


def _sgu_kernel(x_ref, g_ref, w_ref, lng_ref, lnb_ref, ws_ref, bs_ref, ga0_ref, ga1_ref, gb0_ref,
                gb1_ref, h_ref, o_ref, wg_ref, wb_ref, *, width, groups):
    @pl.when(pl.program_id(0) == 0)
    def _():
        wb_ref[...] = w_ref[...].astype(wb_ref.dtype)

    cr = ga0_ref.shape[0]
    wg_ref[0, 0:cr, :] = ga0_ref[...].astype(wg_ref.dtype)
    wg_ref[0, cr:, :] = ga1_ref[...].astype(wg_ref.dtype)
    wg_ref[1, 0:cr, :] = gb0_ref[...].astype(wg_ref.dtype)
    wg_ref[1, cr:, :] = gb1_ref[...].astype(wg_ref.dtype)

    tm = x_ref.shape[0]
    nt = (((1,), (1,)), ((), ()))
    x = x_ref[...]
    ms = jnp.mean(x * x, axis=-1, keepdims=True)
    h = (x * lax.rsqrt(ms + NORM_EPS) * g_ref[...]).astype(h_ref.dtype)
    h_ref[...] = h
    v = lax.dot_general(h, wb_ref[width:, :], nt, preferred_element_type=F32)
    gv = _gelu(v)
    mu = jnp.mean(gv, axis=-1, keepdims=True)
    vc = gv - mu
    var = jnp.mean(vc * vc, axis=-1, keepdims=True)
    vn = (vc * lax.rsqrt(var + LN_EPS) * lng_ref[...] + lnb_ref[...]).astype(BF16)
    uv = lax.dot_general(h, wb_ref[0:width, :], nt, preferred_element_type=F32)
    gd = width // groups
    for c in range(tm // SGU_CHUNK):
        rows = slice(c * SGU_CHUNK, (c + 1) * SGU_CHUNK)
        for g in range(groups):
            cols = slice(g * gd, (g + 1) * gd)
            mixed = jnp.dot(ws_ref[g], vn[rows, cols], preferred_element_type=F32)
            gu = _gelu(uv[rows, cols])
            o_ref[rows, cols] = (gu * (mixed + bs_ref[:, cols])).astype(o_ref.dtype)


def _sgu(x, g, w_t, ln_g, ln_b, ws, bs_full, gate_row0, tm=256, cast_rows=32):
    t, d = x.shape
    width = ln_g.shape[1]
    groups = ws.shape[0]
    steps = t // tm
    half = (w_t.shape[0] - gate_row0) // 2
    assert gate_row0 % cast_rows == 0 and half == 2 * cast_rows * steps
    gate_rows = lambda r0, k: pl.BlockSpec((cast_rows, d),
                                           lambda i: (r0 // cast_rows + 2 * i + k, 0))
    return pl.pallas_call(
        functools.partial(_sgu_kernel, width=width, groups=groups),
        grid=(steps,),
        in_specs=[
            pl.BlockSpec((tm, d), lambda i: (i, 0)),
            pl.BlockSpec((1, d), lambda i: (0, 0)),
            pl.BlockSpec((2 * width, d), lambda i: (0, 0), pipeline_mode=pl.Buffered(1)),
            pl.BlockSpec((1, width), lambda i: (0, 0)),
            pl.BlockSpec((1, width), lambda i: (0, 0)),
            pl.BlockSpec(ws.shape, lambda i: (0, 0, 0)),
            pl.BlockSpec(bs_full.shape, lambda i: (0, 0)),
            gate_rows(gate_row0, 0), gate_rows(gate_row0, 1),
            gate_rows(gate_row0 + half, 0), gate_rows(gate_row0 + half, 1),
        ],
        out_specs=[pl.BlockSpec((tm, d), lambda i: (i, 0)),
                   pl.BlockSpec((tm, width), lambda i: (i, 0)),
                   pl.BlockSpec((2, 2 * cast_rows, d), lambda i: (0, i, 0))],
        out_shape=[jax.ShapeDtypeStruct((t, d), BF16), jax.ShapeDtypeStruct((t, width), BF16),
                   jax.ShapeDtypeStruct((2, half, d), BF16)],
        scratch_shapes=[pltpu.VMEM((2 * width, d), BF16)],
        compiler_params=_cparams("arbitrary"),
        name="sgu_mixer",
    )(x, g, w_t, ln_g, ln_b, ws, bs_full, w_t, w_t, w_t, w_t)


def _head_block_ones(scale):
    r = lax.broadcasted_iota(jnp.int32, (LANES, LANES), 0) // HEAD_DIM
    c = lax.broadcasted_iota(jnp.int32, (LANES, LANES), 1) // HEAD_DIM
    return jnp.where(r == c, scale, 0.0).astype(BF16)


def _head_sum(x, bd):
    return jnp.dot(x.astype(BF16), bd, preferred_element_type=F32)


def _prep_kernel(xs_ref, w0_ref, w2_ref, a0_ref, a2_ref, g2_ref, kk_ref, ka_ref, rk_ref,
                 h_ref, wg_ref, sh_ref, lw_ref, bk_ref, bonus_ref, gg_ref, gt_ref, *, width):
    gt_ref[...] = jax.nn.sigmoid(
        lax.dot_general(h_ref[...], wg_ref[...], (((1,), (1,)), ((), ())),
                        preferred_element_type=F32)).astype(gt_ref.dtype)

    xs = xs_ref[...].astype(F32)
    r = xs[:, 0:width]
    k = xs[:, width:2 * width]
    v = xs[:, 2 * width:3 * width]
    o = 3 * width
    lo_w = xs[:, o:o + LANES]
    lo_a = xs[:, o + LANES:o + 2 * LANES]
    lo_g = xs[:, o + 2 * LANES:]

    wpre = w0_ref[...] + jnp.dot(jnp.tanh(lo_w).astype(BF16), w2_ref[...],
                                 preferred_element_type=F32)
    lw = -(2.718281828459045 ** -0.5) * jax.nn.sigmoid(wpre)
    a = jax.nn.sigmoid(a0_ref[...] + jnp.dot(lo_a.astype(BF16), a2_ref[...],
                                             preferred_element_type=F32))
    gg_ref[...] = jnp.dot(jax.nn.sigmoid(lo_g).astype(BF16), g2_ref[...],
                          preferred_element_type=F32).astype(gg_ref.dtype)
    lw_ref[0] = lw[:, :width]
    lw_ref[1] = lw[:, width:]

    bd = _head_block_ones(1.0)
    kk = k * kk_ref[...]
    sh_ref[0] = xs_ref[:, 0:width]
    sh_ref[1] = xs_ref[:, 2 * width:3 * width]
    for j in range(width // LANES):
        cols = slice(j * LANES, (j + 1) * LANES)
        kkj = kk[:, cols]
        ss = _head_sum(kkj * kkj, bd)
        kkn = kkj * lax.rsqrt(jnp.maximum(ss, 1e-24))
        sh_ref[2, :, cols] = kkn.astype(sh_ref.dtype)
        ksum = jnp.zeros_like(kkj)
        for d in range(2):
            dcols = slice(d * width + j * LANES, d * width + (j + 1) * LANES)
            a_d = a[:, dcols]
            k_d = k[:, cols] * (1.0 + (a_d - 1.0) * ka_ref[:, cols])
            bk_ref[d, 0, :, cols] = (kkn * a_d).astype(bk_ref.dtype)
            bk_ref[d, 1, :, cols] = k_d.astype(bk_ref.dtype)
            ksum = ksum + k_d
        bsum = _head_sum(r[:, cols] * ksum * rk_ref[:, cols], bd)
        bonus_ref[:, cols] = (bsum * v[:, cols]).astype(bonus_ref.dtype)


def _rwkv_prep(xs, h, wg, width, w0c, w2c, a0c, a2c, g2p, k_k, k_a, r_k, tm=256):
    t, fw = xs.shape
    _, ngh, d = wg.shape
    r = lambda j, i: 2 * i + j
    row = lambda j, i: (0, 0)
    resident = lambda a: pl.BlockSpec(a.shape, row, pipeline_mode=pl.Buffered(1))
    rows = lambda w: pl.BlockSpec((tm, w), lambda j, i: (r(j, i), 0))
    return pl.pallas_call(
        functools.partial(_prep_kernel, width=width),
        grid=(2, t // (2 * tm)),
        in_specs=[
            rows(fw),
            pl.BlockSpec(w0c.shape, row),
            resident(w2c),
            pl.BlockSpec(a0c.shape, row),
            resident(a2c),
            resident(g2p),
            pl.BlockSpec((1, width), row),
            pl.BlockSpec((1, width), row),
            pl.BlockSpec((1, width), row),
            pl.BlockSpec((2 * tm, d), lambda j, i: (i, 0)),
            pl.BlockSpec((None, ngh, d), lambda j, i: (j, 0, 0), pipeline_mode=pl.Buffered(1)),
        ],
        out_specs=[
            pl.BlockSpec((3, tm, width), lambda j, i: (0, r(j, i), 0)),
            pl.BlockSpec((2, tm, width), lambda j, i: (0, r(j, i), 0)),
            pl.BlockSpec((2, 2, tm, width), lambda j, i: (0, 0, r(j, i), 0)),
            rows(width),
            rows(width),
            pl.BlockSpec((2 * tm, ngh), lambda j, i: (i, j)),
        ],
        out_shape=[
            jax.ShapeDtypeStruct((3, t, width), BF16),
            jax.ShapeDtypeStruct((2, t, width), F32),
            jax.ShapeDtypeStruct((2, 2, t, width), BF16),
            jax.ShapeDtypeStruct((t, width), BF16),
            jax.ShapeDtypeStruct((t, width), BF16),
            jax.ShapeDtypeStruct((t, 2 * ngh), BF16),
        ],
        compiler_params=_cparams("arbitrary", "arbitrary"),
        name="rwkv_prep",
    )(xs, w0c, w2c, a0c, a2c, g2p, k_k, k_a, r_k, h, wg)


def _hat(x, first_half):
    return jnp.concatenate([jnp.where(first_half, x, 0.0), jnp.where(first_half, 0.0, x)],
                           axis=0)


def _scan_kernel(shf_ref, shb_ref, lwf_ref, lwb_ref, bkf_ref, bkb_ref, *rest, group, n_cast,
                 cast_steps):
    cast_in, (of_ref, ob_ref), cast_out, s_ref = (rest[:n_cast], rest[n_cast:n_cast + 2],
                                                  rest[n_cast + 2:2 * n_cast + 2], rest[-1])
    c = SCAN_CHUNK
    batch, width = shf_ref.shape[1], shf_ref.shape[3]
    npair = width // LANES
    assert 2 * c == LANES and c == HEAD_DIM

    @pl.when(pl.program_id(0) == 0)
    def _():
        s_ref[...] = jnp.zeros_like(s_ref)

    @pl.when(pl.program_id(0) < cast_steps)
    def _():
        for src, dst in zip(cast_in, cast_out):
            dst[...] = src[...].astype(dst.dtype)

    bf = lambda t: t.astype(BF16)
    f32 = lambda t: t.astype(F32)
    dg = lambda p, q, dims: lax.dot_general(bf(p), bf(q), (dims, ((), ())),
                                            preferred_element_type=F32)
    dot = lambda p, q: dg(p, q, ((1,), (0,)))
    dot_nt = lambda p, q: dg(p, q, ((1,), (1,)))
    dot_tn = lambda p, q: dg(p, q, ((0,), (0,)))

    wt = lax.broadcasted_iota(jnp.int32, (c, LANES), 0)
    ws = lax.broadcasted_iota(jnp.int32, (c, LANES), 1) % c
    masks = ((wt > ws, wt >= ws), (wt < ws, wt <= ws))
    eye = jnp.where(wt == ws, 1.0, 0.0).astype(F32)
    half = lax.broadcasted_iota(jnp.int32, (c, LANES), 1) < c
    same_head = (lax.broadcasted_iota(jnp.int32, (LANES, LANES), 0) // HEAD_DIM
                 == lax.broadcasted_iota(jnp.int32, (LANES, LANES), 1) // HEAD_DIM)
    hat = lambda t: _hat(t, half)
    refs = ((shf_ref, lwf_ref, bkf_ref, of_ref), (shb_ref, lwb_ref, bkb_ref, ob_ref))

    st = {}
    slots = shf_ref.shape[2] // c

    def prepare(unit):
        slot, d, b, p = unit
        sh_ref, lw_ref, bk_ref, _ = refs[d]
        strict, incl = masks[d]
        cols = slice(p * LANES, (p + 1) * LANES)
        r0 = (slot if d == 0 else slots - 1 - slot) * c
        rows = slice(r0, r0 + c)
        r = f32(sh_ref[0, b, rows, cols])
        v = f32(sh_ref[1, b, rows, cols])
        kk = f32(sh_ref[2, b, rows, cols])
        lw = lw_ref[0, b, rows, cols]
        bb = f32(bk_ref[0, 0, b, rows, cols])
        kd = f32(bk_ref[0, 1, b, rows, cols])
        pre = lw
        s = 1
        while s < c:
            pre = pre + jnp.where(wt >= s, pltpu.roll(pre, s, 0), 0.0)
            s *= 2
        gtot = pre[c - 1:c, :]
        gi = pre if d == 0 else gtot - pre + lw
        en = jnp.exp(-gi)
        ec = jnp.exp(gtot - gi)
        lhs = bf(jnp.concatenate([-kk * jnp.exp(gi - lw), r * jnp.exp(gi)], axis=0))
        rhs = jnp.concatenate([hat(bb * en), hat(kd * en)], axis=0)
        aa = dot_nt(lhs, rhs)
        a_ab = jnp.where(strict, aa[:c, :LANES], 0.0)
        a_ak = jnp.where(strict, aa[:c, LANES:], 0.0)
        ark = bf(jnp.concatenate([jnp.where(incl, aa[c:, :LANES], 0.0),
                                  jnp.where(incl, aa[c:, LANES:], 0.0)], axis=1))
        vh = bf(hat(v))
        sidx = (d * batch + b) * npair + p
        s0 = s_ref[sidx]
        ls = dot_nt(lhs, s0)
        st[unit] = dict(
            sidx=sidx, rows=rows, cols=cols, v=v, s0=s0, ark=ark, vh=vh,
            bk=bf(jnp.concatenate([bb * ec, kd * ec], axis=0)),
            eg=jnp.exp(gtot), pw=a_ab, inv=eye + a_ab,
            rhs_u=ls[:c] + dot(a_ak, vh), rs=ls[c:])

    def square(grp):
        for u in grp:
            st[u]["pw"] = dot(st[u]["pw"], hat(st[u]["pw"]))

    def inverse_round(grp):
        for u in grp:
            e = st[u]
            both = dot(jnp.concatenate([e["pw"], e["inv"]], axis=0), hat(e["pw"]))
            e["pw"] = both[:c]
            e["inv"] = e["inv"] + both[c:]

    def finish_stages(grp):
        def last_product():
            for u in grp:
                st[u]["inv"] = st[u]["inv"] + dot(st[u]["inv"], hat(st[u]["pw"]))

        def solve():
            for u in grp:
                st[u]["u"] = dot(st[u]["inv"], hat(st[u]["rhs_u"]))

        def outputs():
            for u in grp:
                e = st[u]
                o_ref = refs[u[1]][3]
                uvh = jnp.concatenate([bf(hat(e["u"])), e["vh"]], axis=0)
                o_ref[u[2], e["rows"], e["cols"]] = (e["rs"] + dot(e["ark"], uvh)).astype(o_ref.dtype)

        def states():
            for u in grp:
                e = st[u]
                uv = jnp.concatenate([e["u"], e["v"]], axis=0)
                s_ref[e["sidx"]] = (e["s0"] * e["eg"]
                                    + jnp.where(same_head, dot_tn(uv, e["bk"]), 0.0))

        return [last_product, solve, outputs, states]

    units = [(slot, d, b, p) for slot in range(slots) for d in range(2)
             for b in range(batch) for p in range(npair)]
    assert group == batch * npair, "one group per (chunk slot, direction)"
    groups = [units[g0:g0 + group] for g0 in range(0, len(units), group)]
    rounds = 0
    n = 4
    while n < c:
        rounds += 1
        n *= 2
    for u in groups[0]:
        prepare(u)
    for gi_, grp in enumerate(groups):
        nxt = groups[gi_ + 1] if gi_ + 1 < len(groups) else []
        share = -(-len(nxt) // (rounds + 1))
        square(grp)
        for u in nxt[:share]:
            prepare(u)
        for rd in range(rounds):
            inverse_round(grp)
            for u in nxt[(rd + 1) * share:(rd + 2) * share]:
                prepare(u)
        for stage in finish_stages(grp):
            stage()


def _rwkv_scan(sh, lw, bk, batch, seq, to_bf16=(), slots=2):
    _, t, width = sh.shape
    rows = slots * SCAN_CHUNK
    steps = seq // rows
    sh = sh.reshape(3, batch, seq, width)
    lw = lw.reshape(2, batch, seq, width)
    bk = bk.reshape(2, 2, batch, seq, width)
    sh_spec = lambda f: pl.BlockSpec((3, batch, rows, width), lambda i: (0, 0, f(i), 0))
    lw_spec = lambda d, f: pl.BlockSpec((1, batch, rows, width), lambda i: (d, 0, f(i), 0))
    bk_spec = lambda d, f: pl.BlockSpec((1, 2, batch, rows, width), lambda i: (d, 0, 0, f(i), 0))
    o_spec = lambda f: pl.BlockSpec((batch, rows, width), lambda i: (0, f(i), 0))
    fwd = lambda i: i
    bwd = lambda i: steps - 1 - i
    cast_specs = [pl.BlockSpec((a.shape[0] // steps, a.shape[1]), lambda i: (i, 0))
                  for a in to_bf16]
    outs = pl.pallas_call(
        functools.partial(_scan_kernel, group=batch * (width // LANES), n_cast=len(to_bf16),
                          cast_steps=steps),
        grid=(steps,),
        in_specs=[sh_spec(fwd), sh_spec(bwd), lw_spec(0, fwd), lw_spec(1, bwd),
                  bk_spec(0, fwd), bk_spec(1, bwd)] + cast_specs,
        out_specs=[o_spec(fwd), o_spec(bwd)] + cast_specs,
        out_shape=([jax.ShapeDtypeStruct((batch, seq, width), BF16)] * 2
                   + [jax.ShapeDtypeStruct(a.shape, BF16) for a in to_bf16]),
        scratch_shapes=[pltpu.VMEM((2 * batch * (width // LANES), LANES, LANES), F32)],
        compiler_params=_cparams("arbitrary"),
        name="rwkv_scan",
    )(sh, sh, lw, lw, bk, bk, *to_bf16)
    return (outs[0].reshape(t, width), outs[1].reshape(t, width)) + tuple(outs[2:])


def _rwkv_out_rows(of_ref, ob_ref, bonus_ref, gg_ref, gng_ref, gnb_ref):
    width = of_ref.shape[1]
    avg = _head_block_ones(1.0 / HEAD_DIM)
    tiles = []
    for j in range(width // LANES):
        cols = slice(j * LANES, (j + 1) * LANES)
        o = of_ref[:, cols].astype(F32) + ob_ref[:, cols].astype(F32)
        oc = o - _head_sum(o, avg)
        var = _head_sum(oc * oc, avg)
        y = oc * lax.rsqrt(var + GN_EPS) * gng_ref[:, cols] + gnb_ref[:, cols]
        tiles.append(((y + bonus_ref[:, cols].astype(F32))
                      * gg_ref[:, cols].astype(F32)).astype(BF16))
    return jnp.concatenate(tiles, axis=1)


def _mix_out_kernel(ya_ref, of_ref, ob_ref, bonus_ref, gg_ref, gng_ref, gnb_ref, wa_ref, wb_ref,
                    ga_ref, gb_ref, wo_ref, x_ref, g_ref, x1_ref, h2_ref):
    yb = _rwkv_out_rows(of_ref, ob_ref, bonus_ref, gg_ref, gng_ref, gnb_ref)
    pa = jnp.dot(ya_ref[...], wa_ref[...], preferred_element_type=F32)
    pb = jnp.dot(yb, wb_ref[...], preferred_element_type=F32)
    merged = (ga_ref[...].astype(F32) * pa + gb_ref[...].astype(F32) * pb).astype(BF16)
    x1 = x_ref[...] + jnp.dot(merged, wo_ref[...], preferred_element_type=F32)
    x1_ref[...] = x1
    ms = jnp.mean(x1 * x1, axis=-1, keepdims=True)
    h2_ref[...] = (x1 * lax.rsqrt(ms + NORM_EPS) * g_ref[...]).astype(h2_ref.dtype)


def _mix_out(ya, o_f, o_b, bonus, gg, gn_g, gn_b, wa, wb, gates, wo, x, g, tm=256):
    t, k = ya.shape
    d = x.shape[1]
    resident = lambda shape: pl.BlockSpec(shape, lambda i: (0, 0), pipeline_mode=pl.Buffered(1))
    rows = lambda w: pl.BlockSpec((tm, w), lambda i: (i, 0))
    return pl.pallas_call(
        _mix_out_kernel,
        grid=(t // tm,),
        in_specs=[
            rows(k), rows(k), rows(k), rows(k), rows(k),
            pl.BlockSpec((1, k), lambda i: (0, 0)),
            pl.BlockSpec((1, k), lambda i: (0, 0)),
            resident((k, d)),
            resident((k, d)),
            pl.BlockSpec((tm, d), lambda i: (i, 0)),
            pl.BlockSpec((tm, d), lambda i: (i, 1)),
            resident((d, d)),
            rows(d),
            pl.BlockSpec((1, d), lambda i: (0, 0)),
        ],
        out_specs=[rows(d), rows(d)],
        out_shape=[jax.ShapeDtypeStruct((t, d), F32), jax.ShapeDtypeStruct((t, d), BF16)],
        compiler_params=_cparams("parallel"),
        name="mix_out",
    )(ya, o_f, o_b, bonus, gg, gn_g, gn_b, wa, wb, gates, gates, wo, x, g)


def _ffn1_kernel(h_ref, hb_ref, ha_ref, wg_ref, wu_ref, cw_ref, cb_ref, a_ref, *, seq):
    tm = h_ref.shape[0]
    halo = hb_ref.shape[0]
    i = pl.program_id(0)
    tiles_per_seq = seq // tm
    first = (i % tiles_per_seq) == 0
    last = (i % tiles_per_seq) == tiles_per_seq - 1

    h = h_ref[...]
    zero = jnp.zeros_like(hb_ref[...])
    h_ext = jnp.concatenate([jnp.where(first, zero, hb_ref[...]), h,
                             jnp.where(last, zero, ha_ref[...])], axis=0)
    g_ext = jnp.dot(h_ext, wg_ref[...].astype(BF16), preferred_element_type=F32)
    up = jnp.dot(h, wu_ref[...].astype(BF16), preferred_element_type=F32)
    rows = tm + 2 * halo
    g = g_ext[halo:halo + tm]
    prev = pltpu.roll(g_ext, 1, 0)[halo:halo + tm]
    nxt = pltpu.roll(g_ext, rows - 1, 0)[halo:halo + tm]
    gt = prev * cw_ref[0:1, :] + g * cw_ref[1:2, :] + nxt * cw_ref[2:3, :] + cb_ref[...]
    a_ref[...] = (jax.nn.silu(gt) * up).astype(a_ref.dtype)


def _ffn1(h, wg, wu, cw, cb, seq, tm=1024, tn=512, halo=16):
    t, d = h.shape
    f = wg.shape[1]
    nbh = t // halo
    return pl.pallas_call(
        functools.partial(_ffn1_kernel, seq=seq),
        grid=(t // tm, f // tn),
        in_specs=[
            pl.BlockSpec((tm, d), lambda i, j: (i, 0)),
            pl.BlockSpec((halo, d), lambda i, j: (jnp.maximum(i * (tm // halo) - 1, 0), 0)),
            pl.BlockSpec((halo, d), lambda i, j: (jnp.minimum((i + 1) * (tm // halo), nbh - 1), 0)),
            pl.BlockSpec((d, tn), lambda i, j: (0, j)),
            pl.BlockSpec((d, tn), lambda i, j: (0, j)),
            pl.BlockSpec((cw.shape[0], tn), lambda i, j: (0, j)),
            pl.BlockSpec((1, tn), lambda i, j: (0, j)),
        ],
        out_specs=pl.BlockSpec((tm, tn), lambda i, j: (i, j)),
        out_shape=jax.ShapeDtypeStruct((t, f), BF16),
        compiler_params=_cparams("parallel", "arbitrary"),
        name="ffn_gate_up",
    )(h, h, h, wg, wu, cw, cb)


def _ffn2_kernel(a_ref, wd_ref, x1_ref, nf_ref, o_ref):
    xf = x1_ref[...] + jnp.dot(a_ref[...], wd_ref[...], preferred_element_type=F32)
    ms = jnp.mean(xf * xf, axis=-1, keepdims=True)
    o_ref[...] = xf * lax.rsqrt(ms + NORM_EPS) * nf_ref[...]


def _ffn2(act, wd, x1, nf, tm=512):
    t, f = act.shape
    d = x1.shape[1]
    return pl.pallas_call(
        _ffn2_kernel,
        grid=(t // tm,),
        in_specs=[
            pl.BlockSpec((tm, f), lambda i: (i, 0)),
            pl.BlockSpec((f, d), lambda i: (0, 0), pipeline_mode=pl.Buffered(1)),
            pl.BlockSpec((tm, d), lambda i: (i, 0)),
            pl.BlockSpec((1, d), lambda i: (0, 0)),
        ],
        out_specs=pl.BlockSpec((tm, d), lambda i: (i, 0)),
        out_shape=jax.ShapeDtypeStruct((t, d), F32),
        compiler_params=_cparams("parallel", vmem_limit=VMEM_LIMIT_RESIDENT),
        name="ffn_down",
    )(act, wd, x1, nf)


def _pad_cols(a, n):
    return jnp.pad(a, ((0, 0), (0, n - a.shape[1])))


def _pad_rows(a, n):
    return jnp.pad(a, ((0, n - a.shape[0]), (0, 0)))


def _layer(x, batch, seq, norm1_g, w_in, sgu_ln_g, sgu_ln_b, sgu_w, sgu_b,
           mu_prev, mu_next, w0_f, w2_f, a0_f, a2_f, w0_b, w2_b, a0_b, a2_b, k_k, k_a, r_k,
           g2, gn_g, gn_b, w_proj_a, w_proj_b, w_out, norm2_g, ffn_w_gate, ffn_w_up,
           ffn_conv_w, ffn_conv_b, ffn_w_down, norm_out_g):
    d = x.shape[1]
    sgu_width = sgu_ln_g.shape[0]
    width = k_k.shape[0]
    dl = w2_f.shape[0]
    al = a2_f.shape[0]
    feat_w = mu_prev.shape[0]
    feat_pad = -(-feat_w // (4 * LANES)) * (4 * LANES)
    assert dl == HEAD_DIM and al == HEAD_DIM and 2 * dl == LANES
    row = lambda a: a.reshape(1, -1)

    o1 = 2 * sgu_width
    o2 = o1 + feat_w
    w_t = jnp.swapaxes(w_in, 0, 1)
    zw = jnp.zeros_like(w2_f)
    w2c = jnp.concatenate([jnp.concatenate([w2_f, zw], 1),
                           jnp.concatenate([zw, w2_b], 1)], 0).astype(BF16)
    za = jnp.zeros_like(a2_f)
    a2c = jnp.concatenate([jnp.concatenate([a2_f, za], 1),
                           jnp.concatenate([za, a2_b], 1)], 0).astype(BF16)
    g2p = _pad_rows(g2, feat_pad - 3 * width - 2 * LANES).astype(BF16)
    w0c = row(jnp.concatenate([w0_f, w0_b]))
    a0c = row(jnp.concatenate([a0_f, a0_b]))
    mup = _pad_cols(row(mu_prev), feat_pad)
    mun = _pad_cols(row(mu_next), feat_pad)
    bs_full = jnp.repeat(sgu_b.T, sgu_width // sgu_b.shape[0], axis=1)

    h1, y_a, w_gates = _sgu(x, row(norm1_g), w_t, row(sgu_ln_g), row(sgu_ln_b),
                            sgu_w.astype(BF16), bs_full, o2)
    xs = _in_proj(h1, w_t, mup, mun, o1, feat_pad, seq)

    sh, lw, bk, bonus, gg, gates = _rwkv_prep(xs, h1, w_gates, width, w0c, w2c, a0c, a2c, g2p,
                                              row(k_k), row(k_a), row(r_k))
    o_f, o_b, wa_bf, wb_bf, wo_bf, wd_bf = _rwkv_scan(
        sh, lw, bk, batch, seq, to_bf16=(w_proj_a, w_proj_b, w_out, ffn_w_down))

    x1, h2 = _mix_out(y_a, o_f, o_b, bonus, gg, row(gn_g), row(gn_b), wa_bf, wb_bf, gates, wo_bf,
                      x, row(norm2_g))

    act = _ffn1(h2, ffn_w_gate, ffn_w_up, ffn_conv_w, row(ffn_conv_b), seq)
    return _ffn2(act, wd_bf, x1, norm_out_g)


def kernel(x, norm1_g, w_in, sgu_ln_g, sgu_ln_b, sgu_w, sgu_b, rwkv_mu_prev, rwkv_mu_next, rwkv_w0_f, rwkv_w2_f, rwkv_a0_f, rwkv_a2_f, rwkv_w0_b, rwkv_w2_b, rwkv_a0_b, rwkv_a2_b, rwkv_k_k, rwkv_k_a, rwkv_r_k, rwkv_g2, rwkv_gn_g, rwkv_gn_b, w_proj_a, w_proj_b, w_out, norm2_g, ffn_w_gate, ffn_w_up, ffn_conv_w, ffn_conv_b, ffn_w_down, norm_f_g):
    batch, seq, d = x.shape
    depth = norm1_g.shape[0]
    assert depth == 1, "the fused final RMSNorm assumes a single layer"
    xf = x.reshape(batch * seq, d)
    per_layer = (norm1_g, w_in, sgu_ln_g, sgu_ln_b, sgu_w, sgu_b, rwkv_mu_prev, rwkv_mu_next,
                 rwkv_w0_f, rwkv_w2_f, rwkv_a0_f, rwkv_a2_f, rwkv_w0_b, rwkv_w2_b, rwkv_a0_b,
                 rwkv_a2_b, rwkv_k_k, rwkv_k_a, rwkv_r_k, rwkv_g2, rwkv_gn_g, rwkv_gn_b,
                 w_proj_a, w_proj_b, w_out, norm2_g, ffn_w_gate, ffn_w_up, ffn_conv_w,
                 ffn_conv_b, ffn_w_down)
    out = _layer(xf, batch, seq, *(p[0] for p in per_layer), norm_f_g.reshape(1, d))
    return out.reshape(batch, seq, d)
```

```python
import functools

import jax
import jax.numpy as jnp
from jax import lax
from jax.experimental import pallas as pl
from jax.experimental.pallas import tpu as pltpu

F32 = jnp.float32
BF16 = jnp.bfloat16

NORM_EPS = 1e-6
LN_EPS = 1e-5
GN_EPS = 64e-5

HEAD_DIM = 64
LANES = 128
SGU_CHUNK = 128
SCAN_CHUNK = 64
VMEM_LIMIT = 48 * 1024 * 1024


def _cparams(*sem):
    return pltpu.CompilerParams(dimension_semantics=sem, vmem_limit_bytes=VMEM_LIMIT)


def _in_proj_kernel(h_ref, hb_ref, ha_ref, w_ref, mup_ref, mun_ref, xs_ref, hx_ref, *, seq):
    tm = h_ref.shape[0]
    halo = hb_ref.shape[0]
    i = pl.program_id(0)
    j = pl.program_id(1)
    tiles_per_seq = seq // tm
    first = (i % tiles_per_seq) == 0
    last = (i % tiles_per_seq) == tiles_per_seq - 1

    @pl.when(j == 0)
    def _():
        zero = jnp.zeros_like(hb_ref[...])
        hx_ref[0:halo, :] = jnp.where(first, zero, hb_ref[...])
        hx_ref[halo:halo + tm, :] = h_ref[...]
        hx_ref[halo + tm:, :] = jnp.where(last, zero, ha_ref[...])

    f_ext = lax.dot_general(hx_ref[...], w_ref[...].astype(BF16), (((1,), (1,)), ((), ())),
                            preferred_element_type=F32)
    rows = tm + 2 * halo
    f = f_ext[halo:halo + tm]
    prev = pltpu.roll(f_ext, 1, 0)[halo:halo + tm]
    nxt = pltpu.roll(f_ext, rows - 1, 0)[halo:halo + tm]
    mup = mup_ref[...]
    mun = mun_ref[...]
    xs_ref[...] = (f * (1.0 - mup - mun) + mup * prev + mun * nxt).astype(xs_ref.dtype)


def _in_proj(h, w_t, mup, mun, row0, w_ft, seq, tm=2048, tn=512, halo=16):
    t, d = h.shape
    n_ft = w_ft // tn
    nbh = t // halo
    assert row0 % tn == 0
    return pl.pallas_call(
        functools.partial(_in_proj_kernel, seq=seq),
        grid=(t // tm, n_ft),
        in_specs=[
            pl.BlockSpec((tm, d), lambda i, j: (i, 0), pipeline_mode=pl.Buffered(1)),
            pl.BlockSpec((halo, d), lambda i, j: (jnp.maximum(i * (tm // halo) - 1, 0), 0)),
            pl.BlockSpec((halo, d), lambda i, j: (jnp.minimum((i + 1) * (tm // halo), nbh - 1), 0)),
            pl.BlockSpec((tn, d), lambda i, j: (row0 // tn + j, 0)),
            pl.BlockSpec((1, tn), lambda i, j: (0, j)),
            pl.BlockSpec((1, tn), lambda i, j: (0, j)),
        ],
        out_specs=pl.BlockSpec((tm, tn), lambda i, j: (i, j)),
        out_shape=jax.ShapeDtypeStruct((t, n_ft * tn), BF16),
        scratch_shapes=[pltpu.VMEM((tm + 2 * halo, d), BF16)],
        compiler_params=_cparams("parallel", "arbitrary"),
        name="in_proj",
    )(h, h, h, w_t, mup, mun)


def _gelu(x):
    return 0.5 * x * (1.0 + lax.erf(x * (2.0 ** -0.5)))


def _sgu_kernel(x_ref, g_ref, w_ref, lng_ref, lnb_ref, ws_ref, bs_ref, ga0_ref, ga1_ref, gb0_ref,
                gb1_ref, h_ref, o_ref, wg_ref, wb_ref, *, width, groups):
    @pl.when(pl.program_id(0) == 0)
    def _():
        wb_ref[...] = w_ref[...].astype(wb_ref.dtype)

    cr = ga0_ref.shape[0]
    wg_ref[0, 0:cr, :] = ga0_ref[...].astype(wg_ref.dtype)
    wg_ref[0, cr:, :] = ga1_ref[...].astype(wg_ref.dtype)
    wg_ref[1, 0:cr, :] = gb0_ref[...].astype(wg_ref.dtype)
    wg_ref[1, cr:, :] = gb1_ref[...].astype(wg_ref.dtype)

    tm = x_ref.shape[0]
    nt = (((1,), (1,)), ((), ()))
    x = x_ref[...]
    ms = jnp.mean(x * x, axis=-1, keepdims=True)
    h = (x * lax.rsqrt(ms + NORM_EPS) * g_ref[...]).astype(h_ref.dtype)
    h_ref[...] = h
    v = lax.dot_general(h, wb_ref[width:, :], nt, preferred_element_type=F32)
    gv = _gelu(v)
    mu = jnp.mean(gv, axis=-1, keepdims=True)
    vc = gv - mu
    var = jnp.mean(vc * vc, axis=-1, keepdims=True)
    vn = (vc * lax.rsqrt(var + LN_EPS) * lng_ref[...] + lnb_ref[...]).astype(BF16)
    uv = lax.dot_general(h, wb_ref[0:width, :], nt, preferred_element_type=F32)
    gd = width // groups
    for c in range(tm // SGU_CHUNK):
        rows = slice(c * SGU_CHUNK, (c + 1) * SGU_CHUNK)
        for g in range(groups):
            cols = slice(g * gd, (g + 1) * gd)
            mixed = jnp.dot(ws_ref[g], vn[rows, cols], preferred_element_type=F32)
            gu = _gelu(uv[rows, cols])
            o_ref[rows, cols] = (gu * (mixed + bs_ref[:, cols])).astype(o_ref.dtype)


def _sgu(x, g, w_t, ln_g, ln_b, ws, bs_full, gate_row0, tm=256, cast_rows=32):
    t, d = x.shape
    width = ln_g.shape[1]
    groups = ws.shape[0]
    steps = t // tm
    half = (w_t.shape[0] - gate_row0) // 2
    assert gate_row0 % cast_rows == 0 and half == 2 * cast_rows * steps
    gate_rows = lambda r0, k: pl.BlockSpec((cast_rows, d),
                                           lambda i: (r0 // cast_rows + 2 * i + k, 0))
    return pl.pallas_call(
        functools.partial(_sgu_kernel, width=width, groups=groups),
        grid=(steps,),
        in_specs=[
            pl.BlockSpec((tm, d), lambda i: (i, 0)),
            pl.BlockSpec((1, d), lambda i: (0, 0)),
            pl.BlockSpec((2 * width, d), lambda i: (0, 0), pipeline_mode=pl.Buffered(1)),
            pl.BlockSpec((1, width), lambda i: (0, 0)),
            pl.BlockSpec((1, width), lambda i: (0, 0)),
            pl.BlockSpec(ws.shape, lambda i: (0, 0, 0)),
            pl.BlockSpec(bs_full.shape, lambda i: (0, 0)),
            gate_rows(gate_row0, 0), gate_rows(gate_row0, 1),
            gate_rows(gate_row0 + half, 0), gate_rows(gate_row0 + half, 1),
        ],
        out_specs=[pl.BlockSpec((tm, d), lambda i: (i, 0)),
                   pl.BlockSpec((tm, width), lambda i: (i, 0)),
                   pl.BlockSpec((2, 2 * cast_rows, d), lambda i: (0, i, 0))],
        out_shape=[jax.ShapeDtypeStruct((t, d), BF16), jax.ShapeDtypeStruct((t, width), BF16),
                   jax.ShapeDtypeStruct((2, half, d), BF16)],
        scratch_shapes=[pltpu.VMEM((2 * width, d), BF16)],
        compiler_params=_cparams("arbitrary"),
        name="sgu_mixer",
    )(x, g, w_t, ln_g, ln_b, ws, bs_full, w_t, w_t, w_t, w_t)


def _head_block_ones(scale):
    r = lax.broadcasted_iota(jnp.int32, (LANES, LANES), 0) // HEAD_DIM
    c = lax.broadcasted_iota(jnp.int32, (LANES, LANES), 1) // HEAD_DIM
    return jnp.where(r == c, scale, 0.0).astype(BF16)


def _head_sum(x, bd):
    return jnp.dot(x.astype(BF16), bd, preferred_element_type=F32)


def _prep_kernel(xs_ref, w0_ref, w2_ref, a0_ref, a2_ref, g2_ref, kk_ref, ka_ref, rk_ref,
                 h_ref, wg_ref, sh_ref, lw_ref, bk_ref, bonus_ref, gg_ref, gt_ref, *, width):
    pieces = width // LANES
    gn = gt_ref.shape[1] // pieces

    def gate_piece(c):
        cols = slice(c * gn, (c + 1) * gn)
        gt_ref[:, cols] = jax.nn.sigmoid(
            lax.dot_general(h_ref[...], wg_ref[cols, :], (((1,), (1,)), ((), ())),
                            preferred_element_type=F32)).astype(gt_ref.dtype)

    xs = xs_ref[...].astype(F32)
    r = xs[:, 0:width]
    k = xs[:, width:2 * width]
    v = xs[:, 2 * width:3 * width]
    o = 3 * width
    lo_w = xs[:, o:o + LANES]
    lo_a = xs[:, o + LANES:o + 2 * LANES]
    lo_g = xs[:, o + 2 * LANES:]

    wpre = w0_ref[...] + jnp.dot(jnp.tanh(lo_w).astype(BF16), w2_ref[...],
                                 preferred_element_type=F32)
    lw = -(2.718281828459045 ** -0.5) * jax.nn.sigmoid(wpre)
    a = jax.nn.sigmoid(a0_ref[...] + jnp.dot(lo_a.astype(BF16), a2_ref[...],
                                             preferred_element_type=F32))
    gg_ref[...] = jnp.dot(jax.nn.sigmoid(lo_g).astype(BF16), g2_ref[...],
                          preferred_element_type=F32).astype(gg_ref.dtype)
    lw_ref[0] = lw[:, :width]
    lw_ref[1] = lw[:, width:]

    bd = _head_block_ones(1.0)
    kk = k * kk_ref[...]
    sh_ref[0] = xs_ref[:, 0:width]
    sh_ref[1] = xs_ref[:, 2 * width:3 * width]
    for j in range(width // LANES):
        gate_piece(j)
        cols = slice(j * LANES, (j + 1) * LANES)
        kkj = kk[:, cols]
        ss = _head_sum(kkj * kkj, bd)
        kkn = kkj * lax.rsqrt(jnp.maximum(ss, 1e-24))
        sh_ref[2, :, cols] = kkn.astype(sh_ref.dtype)
        ksum = jnp.zeros_like(kkj)
        for d in range(2):
            dcols = slice(d * width + j * LANES, d * width + (j + 1) * LANES)
            a_d = a[:, dcols]
            k_d = k[:, cols] * (1.0 + (a_d - 1.0) * ka_ref[:, cols])
            bk_ref[d, 0, :, cols] = (kkn * a_d).astype(bk_ref.dtype)
            bk_ref[d, 1, :, cols] = k_d.astype(bk_ref.dtype)
            ksum = ksum + k_d
        bsum = _head_sum(r[:, cols] * ksum * rk_ref[:, cols], bd)
        bonus_ref[:, cols] = (bsum * v[:, cols]).astype(bonus_ref.dtype)


def _rwkv_prep(xs, h, wg, width, w0c, w2c, a0c, a2c, g2p, k_k, k_a, r_k, tm=128):
    t, fw = xs.shape
    _, ngh, d = wg.shape
    r = lambda j, i: 2 * i + j
    row = lambda j, i: (0, 0)
    resident = lambda a: pl.BlockSpec(a.shape, row, pipeline_mode=pl.Buffered(1))
    rows = lambda w: pl.BlockSpec((tm, w), lambda j, i: (r(j, i), 0))
    return pl.pallas_call(
        functools.partial(_prep_kernel, width=width),
        grid=(2, t // (2 * tm)),
        in_specs=[
            rows(fw),
            pl.BlockSpec(w0c.shape, row),
            resident(w2c),
            pl.BlockSpec(a0c.shape, row),
            resident(a2c),
            resident(g2p),
            pl.BlockSpec((1, width), row),
            pl.BlockSpec((1, width), row),
            pl.BlockSpec((1, width), row),
            pl.BlockSpec((2 * tm, d), lambda j, i: (i, 0)),
            pl.BlockSpec((None, ngh, d), lambda j, i: (j, 0, 0), pipeline_mode=pl.Buffered(1)),
        ],
        out_specs=[
            pl.BlockSpec((3, tm, width), lambda j, i: (0, r(j, i), 0)),
            pl.BlockSpec((2, tm, width), lambda j, i: (0, r(j, i), 0)),
            pl.BlockSpec((2, 2, tm, width), lambda j, i: (0, 0, r(j, i), 0)),
            rows(width),
            rows(width),
            pl.BlockSpec((2 * tm, ngh), lambda j, i: (i, j)),
        ],
        out_shape=[
            jax.ShapeDtypeStruct((3, t, width), BF16),
            jax.ShapeDtypeStruct((2, t, width), F32),
            jax.ShapeDtypeStruct((2, 2, t, width), BF16),
            jax.ShapeDtypeStruct((t, width), BF16),
            jax.ShapeDtypeStruct((t, width), BF16),
            jax.ShapeDtypeStruct((t, 2 * ngh), BF16),
        ],
        compiler_params=_cparams("arbitrary", "arbitrary"),
        name="rwkv_prep",
    )(xs, w0c, w2c, a0c, a2c, g2p, k_k, k_a, r_k, h, wg)


def _hat(x, first_half):
    return jnp.concatenate([jnp.where(first_half, x, 0.0), jnp.where(first_half, 0.0, x)],
                           axis=0)


def _scan_kernel(shf_ref, shb_ref, lwf_ref, lwb_ref, bkf_ref, bkb_ref, *rest, group, n_cast,
                 cast_steps):
    cast_in, (of_ref, ob_ref), cast_out, s_ref = (rest[:n_cast], rest[n_cast:n_cast + 2],
                                                  rest[n_cast + 2:2 * n_cast + 2], rest[-1])
    c = SCAN_CHUNK
    batch, width = shf_ref.shape[1], shf_ref.shape[3]
    npair = width // LANES
    assert 2 * c == LANES and c == HEAD_DIM

    @pl.when(pl.program_id(0) == 0)
    def _():
        s_ref[...] = jnp.zeros_like(s_ref)

    @pl.when(pl.program_id(0) < cast_steps)
    def _():
        for src, dst in zip(cast_in, cast_out):
            dst[...] = src[...].astype(dst.dtype)

    bf = lambda t: t.astype(BF16)
    f32 = lambda t: t.astype(F32)
    dg = lambda p, q, dims: lax.dot_general(bf(p), bf(q), (dims, ((), ())),
                                            preferred_element_type=F32)
    dot = lambda p, q: dg(p, q, ((1,), (0,)))
    dot_nt = lambda p, q: dg(p, q, ((1,), (1,)))
    dot_tn = lambda p, q: dg(p, q, ((0,), (0,)))

    wt = lax.broadcasted_iota(jnp.int32, (c, LANES), 0)
    ws = lax.broadcasted_iota(jnp.int32, (c, LANES), 1) % c
    masks = ((wt > ws, wt >= ws), (wt < ws, wt <= ws))
    eye = jnp.where(wt == ws, 1.0, 0.0).astype(F32)
    half = lax.broadcasted_iota(jnp.int32, (c, LANES), 1) < c
    same_head = (lax.broadcasted_iota(jnp.int32, (LANES, LANES), 0) // HEAD_DIM
                 == lax.broadcasted_iota(jnp.int32, (LANES, LANES), 1) // HEAD_DIM)
    hat = lambda t: _hat(t, half)
    refs = ((shf_ref, lwf_ref, bkf_ref, of_ref), (shb_ref, lwb_ref, bkb_ref, ob_ref))

    st = {}
    slots = shf_ref.shape[2] // c

    def prepare(unit):
        slot, d, b, p = unit
        sh_ref, lw_ref, bk_ref, _ = refs[d]
        strict, incl = masks[d]
        cols = slice(p * LANES, (p + 1) * LANES)
        r0 = (slot if d == 0 else slots - 1 - slot) * c
        rows = slice(r0, r0 + c)
        r = f32(sh_ref[0, b, rows, cols])
        v = f32(sh_ref[1, b, rows, cols])
        kk = f32(sh_ref[2, b, rows, cols])
        lw = lw_ref[0, b, rows, cols]
        bb = f32(bk_ref[0, 0, b, rows, cols])
        kd = f32(bk_ref[0, 1, b, rows, cols])
        pre = lw
        s = 1
        while s < c:
            pre = pre + jnp.where(wt >= s, pltpu.roll(pre, s, 0), 0.0)
            s *= 2
        gtot = pre[c - 1:c, :]
        gi = pre if d == 0 else gtot - pre + lw
        en = jnp.exp(-gi)
        ec = jnp.exp(gtot - gi)
        lhs = bf(jnp.concatenate([-kk * jnp.exp(gi - lw), r * jnp.exp(gi)], axis=0))
        rhs = jnp.concatenate([hat(bb * en), hat(kd * en)], axis=0)
        aa = dot_nt(lhs, rhs)
        a_ab = jnp.where(strict, aa[:c, :LANES], 0.0)
        a_ak = jnp.where(strict, aa[:c, LANES:], 0.0)
        ark = bf(jnp.concatenate([jnp.where(incl, aa[c:, :LANES], 0.0),
                                  jnp.where(incl, aa[c:, LANES:], 0.0)], axis=1))
        vh = bf(hat(v))
        sidx = (d * batch + b) * npair + p
        s0 = s_ref[sidx]
        ls = dot_nt(lhs, s0)
        st[unit] = dict(
            sidx=sidx, rows=rows, cols=cols, v=v, s0=s0, ark=ark, vh=vh,
            bk=bf(jnp.concatenate([bb * ec, kd * ec], axis=0)),
            eg=jnp.exp(gtot), pw=a_ab, inv=eye + a_ab,
            rhs_u=ls[:c] + dot(a_ak, vh), rs=ls[c:])

    def square(grp):
        for u in grp:
            st[u]["pw"] = dot(st[u]["pw"], hat(st[u]["pw"]))

    def inverse_round(grp):
        for u in grp:
            e = st[u]
            both = dot(jnp.concatenate([e["pw"], e["inv"]], axis=0), hat(e["pw"]))
            e["pw"] = both[:c]
            e["inv"] = e["inv"] + both[c:]

    def finish_stages(grp):
        def last_product():
            for u in grp:
                st[u]["inv"] = st[u]["inv"] + dot(st[u]["inv"], hat(st[u]["pw"]))

        def solve():
            for u in grp:
                st[u]["u"] = dot(st[u]["inv"], hat(st[u]["rhs_u"]))

        def outputs():
            for u in grp:
                e = st[u]
                o_ref = refs[u[1]][3]
                uvh = jnp.concatenate([bf(hat(e["u"])), e["vh"]], axis=0)
                o_ref[u[2], e["rows"], e["cols"]] = (e["rs"] + dot(e["ark"], uvh)).astype(o_ref.dtype)

        def states():
            for u in grp:
                e = st[u]
                uv = jnp.concatenate([e["u"], e["v"]], axis=0)
                s_ref[e["sidx"]] = (e["s0"] * e["eg"]
                                    + jnp.where(same_head, dot_tn(uv, e["bk"]), 0.0))

        return [last_product, solve, outputs, states]

    units = [(slot, d, b, p) for slot in range(slots) for d in range(2)
             for b in range(batch) for p in range(npair)]
    assert group == batch * npair, "one group per (chunk slot, direction)"
    groups = [units[g0:g0 + group] for g0 in range(0, len(units), group)]
    rounds = 0
    n = 4
    while n < c:
        rounds += 1
        n *= 2
    for u in groups[0]:
        prepare(u)
    for gi_, grp in enumerate(groups):
        nxt = groups[gi_ + 1] if gi_ + 1 < len(groups) else []
        share = -(-len(nxt) // (rounds + 1))
        square(grp)
        for u in nxt[:share]:
            prepare(u)
        for rd in range(rounds):
            inverse_round(grp)
            for u in nxt[(rd + 1) * share:(rd + 2) * share]:
                prepare(u)
        for stage in finish_stages(grp):
            stage()


def _rwkv_scan(sh, lw, bk, batch, seq, to_bf16=(), slots=2):
    _, t, width = sh.shape
    rows = slots * SCAN_CHUNK
    steps = seq // rows
    sh = sh.reshape(3, batch, seq, width)
    lw = lw.reshape(2, batch, seq, width)
    bk = bk.reshape(2, 2, batch, seq, width)
    sh_spec = lambda f: pl.BlockSpec((3, batch, rows, width), lambda i: (0, 0, f(i), 0))
    lw_spec = lambda d, f: pl.BlockSpec((1, batch, rows, width), lambda i: (d, 0, f(i), 0))
    bk_spec = lambda d, f: pl.BlockSpec((1, 2, batch, rows, width), lambda i: (d, 0, 0, f(i), 0))
    o_spec = lambda f: pl.BlockSpec((batch, rows, width), lambda i: (0, f(i), 0))
    fwd = lambda i: i
    bwd = lambda i: steps - 1 - i
    cast_specs = [pl.BlockSpec((a.shape[0] // steps, a.shape[1]), lambda i: (i, 0))
                  for a in to_bf16]
    outs = pl.pallas_call(
        functools.partial(_scan_kernel, group=batch * (width // LANES), n_cast=len(to_bf16),
                          cast_steps=steps),
        grid=(steps,),
        in_specs=[sh_spec(fwd), sh_spec(bwd), lw_spec(0, fwd), lw_spec(1, bwd),
                  bk_spec(0, fwd), bk_spec(1, bwd)] + cast_specs,
        out_specs=[o_spec(fwd), o_spec(bwd)] + cast_specs,
        out_shape=([jax.ShapeDtypeStruct((batch, seq, width), BF16)] * 2
                   + [jax.ShapeDtypeStruct(a.shape, BF16) for a in to_bf16]),
        scratch_shapes=[pltpu.VMEM((2 * batch * (width // LANES), LANES, LANES), F32)],
        compiler_params=_cparams("arbitrary"),
        name="rwkv_scan",
    )(sh, sh, lw, lw, bk, bk, *to_bf16)
    return (outs[0].reshape(t, width), outs[1].reshape(t, width)) + tuple(outs[2:])


def _rwkv_out_rows(of_ref, ob_ref, bonus_ref, gg_ref, gng_ref, gnb_ref):
    width = of_ref.shape[1]
    avg = _head_block_ones(1.0 / HEAD_DIM)
    tiles = []
    for j in range(width // LANES):
        cols = slice(j * LANES, (j + 1) * LANES)
        o = of_ref[:, cols].astype(F32) + ob_ref[:, cols].astype(F32)
        oc = o - _head_sum(o, avg)
        var = _head_sum(oc * oc, avg)
        y = oc * lax.rsqrt(var + GN_EPS) * gng_ref[:, cols] + gnb_ref[:, cols]
        tiles.append(((y + bonus_ref[:, cols].astype(F32))
                      * gg_ref[:, cols].astype(F32)).astype(BF16))
    return jnp.concatenate(tiles, axis=1)


def _mix_out_kernel(ya_ref, of_ref, ob_ref, bonus_ref, gg_ref, gng_ref, gnb_ref, wa_ref, wb_ref,
                    ga_ref, gb_ref, wo_ref, x_ref, g_ref, x1_ref, h2_ref):
    yb = _rwkv_out_rows(of_ref, ob_ref, bonus_ref, gg_ref, gng_ref, gnb_ref)
    pa = jnp.dot(ya_ref[...], wa_ref[...], preferred_element_type=F32)
    pb = jnp.dot(yb, wb_ref[...], preferred_element_type=F32)
    merged = (ga_ref[...].astype(F32) * pa + gb_ref[...].astype(F32) * pb).astype(BF16)
    x1 = x_ref[...] + jnp.dot(merged, wo_ref[...], preferred_element_type=F32)
    x1_ref[...] = x1
    ms = jnp.mean(x1 * x1, axis=-1, keepdims=True)
    h2_ref[...] = (x1 * lax.rsqrt(ms + NORM_EPS) * g_ref[...]).astype(h2_ref.dtype)


def _mix_out(ya, o_f, o_b, bonus, gg, gn_g, gn_b, wa, wb, gates, wo, x, g, tm=256):
    t, k = ya.shape
    d = x.shape[1]
    resident = lambda shape: pl.BlockSpec(shape, lambda i: (0, 0), pipeline_mode=pl.Buffered(1))
    rows = lambda w: pl.BlockSpec((tm, w), lambda i: (i, 0))
    return pl.pallas_call(
        _mix_out_kernel,
        grid=(t // tm,),
        in_specs=[
            rows(k), rows(k), rows(k), rows(k), rows(k),
            pl.BlockSpec((1, k), lambda i: (0, 0)),
            pl.BlockSpec((1, k), lambda i: (0, 0)),
            resident((k, d)),
            resident((k, d)),
            pl.BlockSpec((tm, d), lambda i: (i, 0)),
            pl.BlockSpec((tm, d), lambda i: (i, 1)),
            resident((d, d)),
            rows(d),
            pl.BlockSpec((1, d), lambda i: (0, 0)),
        ],
        out_specs=[rows(d), rows(d)],
        out_shape=[jax.ShapeDtypeStruct((t, d), F32), jax.ShapeDtypeStruct((t, d), BF16)],
        compiler_params=_cparams("parallel"),
        name="mix_out",
    )(ya, o_f, o_b, bonus, gg, gn_g, gn_b, wa, wb, gates, gates, wo, x, g)


def _ffn1_kernel(h_ref, hb_ref, ha_ref, wg_ref, wu_ref, cw_ref, cb_ref, a_ref, *, seq):
    tm = h_ref.shape[0]
    halo = hb_ref.shape[0]
    i = pl.program_id(0)
    tiles_per_seq = seq // tm
    first = (i % tiles_per_seq) == 0
    last = (i % tiles_per_seq) == tiles_per_seq - 1

    h = h_ref[...]
    zero = jnp.zeros_like(hb_ref[...])
    h_ext = jnp.concatenate([jnp.where(first, zero, hb_ref[...]), h,
                             jnp.where(last, zero, ha_ref[...])], axis=0)
    g_ext = jnp.dot(h_ext, wg_ref[...].astype(BF16), preferred_element_type=F32)
    up = jnp.dot(h, wu_ref[...].astype(BF16), preferred_element_type=F32)
    rows = tm + 2 * halo
    g = g_ext[halo:halo + tm]
    prev = pltpu.roll(g_ext, 1, 0)[halo:halo + tm]
    nxt = pltpu.roll(g_ext, rows - 1, 0)[halo:halo + tm]
    gt = prev * cw_ref[0:1, :] + g * cw_ref[1:2, :] + nxt * cw_ref[2:3, :] + cb_ref[...]
    a_ref[...] = (jax.nn.silu(gt) * up).astype(a_ref.dtype)


def _ffn1(h, wg, wu, cw, cb, seq, tm=1024, tn=512, halo=16):
    t, d = h.shape
    f = wg.shape[1]
    nbh = t // halo
    return pl.pallas_call(
        functools.partial(_ffn1_kernel, seq=seq),
        grid=(t // tm, f // tn),
        in_specs=[
            pl.BlockSpec((tm, d), lambda i, j: (i, 0)),
            pl.BlockSpec((halo, d), lambda i, j: (jnp.maximum(i * (tm // halo) - 1, 0), 0)),
            pl.BlockSpec((halo, d), lambda i, j: (jnp.minimum((i + 1) * (tm // halo), nbh - 1), 0)),
            pl.BlockSpec((d, tn), lambda i, j: (0, j)),
            pl.BlockSpec((d, tn), lambda i, j: (0, j)),
            pl.BlockSpec((cw.shape[0], tn), lambda i, j: (0, j)),
            pl.BlockSpec((1, tn), lambda i, j: (0, j)),
        ],
        out_specs=pl.BlockSpec((tm, tn), lambda i, j: (i, j)),
        out_shape=jax.ShapeDtypeStruct((t, f), BF16),
        compiler_params=_cparams("parallel", "arbitrary"),
        name="ffn_gate_up",
    )(h, h, h, wg, wu, cw, cb)


def _ffn2_kernel(a_ref, wd_ref, x1_ref, nf_ref, o_ref):
    xf = x1_ref[...] + jnp.dot(a_ref[...], wd_ref[...], preferred_element_type=F32)
    ms = jnp.mean(xf * xf, axis=-1, keepdims=True)
    o_ref[...] = xf * lax.rsqrt(ms + NORM_EPS) * nf_ref[...]


def _ffn2(act, wd, x1, nf, tm=256):
    t, f = act.shape
    d = x1.shape[1]
    return pl.pallas_call(
        _ffn2_kernel,
        grid=(t // tm,),
        in_specs=[
            pl.BlockSpec((tm, f), lambda i: (i, 0)),
            pl.BlockSpec((f, d), lambda i: (0, 0), pipeline_mode=pl.Buffered(1)),
            pl.BlockSpec((tm, d), lambda i: (i, 0)),
            pl.BlockSpec((1, d), lambda i: (0, 0)),
        ],
        out_specs=pl.BlockSpec((tm, d), lambda i: (i, 0)),
        out_shape=jax.ShapeDtypeStruct((t, d), F32),
        compiler_params=_cparams("parallel"),
        name="ffn_down",
    )(act, wd, x1, nf)


def _pad_cols(a, n):
    return jnp.pad(a, ((0, 0), (0, n - a.shape[1])))


def _pad_rows(a, n):
    return jnp.pad(a, ((0, n - a.shape[0]), (0, 0)))


def _layer(x, batch, seq, norm1_g, w_in, sgu_ln_g, sgu_ln_b, sgu_w, sgu_b,
           mu_prev, mu_next, w0_f, w2_f, a0_f, a2_f, w0_b, w2_b, a0_b, a2_b, k_k, k_a, r_k,
           g2, gn_g, gn_b, w_proj_a, w_proj_b, w_out, norm2_g, ffn_w_gate, ffn_w_up,
           ffn_conv_w, ffn_conv_b, ffn_w_down, norm_out_g):
    d = x.shape[1]
    sgu_width = sgu_ln_g.shape[0]
    width = k_k.shape[0]
    dl = w2_f.shape[0]
    al = a2_f.shape[0]
    feat_w = mu_prev.shape[0]
    feat_pad = -(-feat_w // (4 * LANES)) * (4 * LANES)
    assert dl == HEAD_DIM and al == HEAD_DIM and 2 * dl == LANES
    row = lambda a: a.reshape(1, -1)

    o1 = 2 * sgu_width
    o2 = o1 + feat_w
    w_t = jnp.swapaxes(w_in, 0, 1)
    zw = jnp.zeros_like(w2_f)
    w2c = jnp.concatenate([jnp.concatenate([w2_f, zw], 1),
                           jnp.concatenate([zw, w2_b], 1)], 0).astype(BF16)
    za = jnp.zeros_like(a2_f)
    a2c = jnp.concatenate([jnp.concatenate([a2_f, za], 1),
                           jnp.concatenate([za, a2_b], 1)], 0).astype(BF16)
    g2p = _pad_rows(g2, feat_pad - 3 * width - 2 * LANES).astype(BF16)
    w0c = row(jnp.concatenate([w0_f, w0_b]))
    a0c = row(jnp.concatenate([a0_f, a0_b]))
    mup = _pad_cols(row(mu_prev), feat_pad)
    mun = _pad_cols(row(mu_next), feat_pad)
    bs_full = jnp.repeat(sgu_b.T, sgu_width // sgu_b.shape[0], axis=1)

    h1, y_a, w_gates = _sgu(x, row(norm1_g), w_t, row(sgu_ln_g), row(sgu_ln_b),
                            sgu_w.astype(BF16), bs_full, o2)
    xs = _in_proj(h1, w_t, mup, mun, o1, feat_pad, seq)

    sh, lw, bk, bonus, gg, gates = _rwkv_prep(xs, h1, w_gates, width, w0c, w2c, a0c, a2c, g2p,
                                              row(k_k), row(k_a), row(r_k))
    o_f, o_b, wa_bf, wb_bf, wo_bf, wd_bf = _rwkv_scan(
        sh, lw, bk, batch, seq, to_bf16=(w_proj_a, w_proj_b, w_out, ffn_w_down))

    x1, h2 = _mix_out(y_a, o_f, o_b, bonus, gg, row(gn_g), row(gn_b), wa_bf, wb_bf, gates, wo_bf,
                      x, row(norm2_g))

    act = _ffn1(h2, ffn_w_gate, ffn_w_up, ffn_conv_w, row(ffn_conv_b), seq)
    return _ffn2(act, wd_bf, x1, norm_out_g)


def kernel(x, norm1_g, w_in, sgu_ln_g, sgu_ln_b, sgu_w, sgu_b, rwkv_mu_prev, rwkv_mu_next, rwkv_w0_f, rwkv_w2_f, rwkv_a0_f, rwkv_a2_f, rwkv_w0_b, rwkv_w2_b, rwkv_a0_b, rwkv_a2_b, rwkv_k_k, rwkv_k_a, rwkv_r_k, rwkv_g2, rwkv_gn_g, rwkv_gn_b, w_proj_a, w_proj_b, w_out, norm2_g, ffn_w_gate, ffn_w_up, ffn_conv_w, ffn_conv_b, ffn_w_down, norm_f_g):
    batch, seq, d = x.shape
    depth = norm1_g.shape[0]
    assert depth == 1, "the fused final RMSNorm assumes a single layer"
    xf = x.reshape(batch * seq, d)
    per_layer = (norm1_g, w_in, sgu_ln_g, sgu_ln_b, sgu_w, sgu_b, rwkv_mu_prev, rwkv_mu_next,
                 rwkv_w0_f, rwkv_w2_f, rwkv_a0_f, rwkv_a2_f, rwkv_w0_b, rwkv_w2_b, rwkv_a0_b,
                 rwkv_a2_b, rwkv_k_k, rwkv_k_a, rwkv_r_k, rwkv_g2, rwkv_gn_g, rwkv_gn_b,
                 w_proj_a, w_proj_b, w_out, norm2_g, ffn_w_gate, ffn_w_up, ffn_conv_w,
                 ffn_conv_b, ffn_w_down)
    out = _layer(xf, batch, seq, *(p[0] for p in per_layer), norm_f_g.reshape(1, d))
    return out.reshape(batch, seq, d)
```

```python
import functools

import jax
import jax.numpy as jnp
from jax import lax
from jax.experimental import pallas as pl
from jax.experimental.pallas import tpu as pltpu

F32 = jnp.float32
BF16 = jnp.bfloat16

NORM_EPS = 1e-6
LN_EPS = 1e-5
GN_EPS = 64e-5

HEAD_DIM = 64
LANES = 128
SGU_CHUNK = 128
SCAN_CHUNK = 64
VMEM_LIMIT = 48 * 1024 * 1024


def _cparams(*sem):
    return pltpu.CompilerParams(dimension_semantics=sem, vmem_limit_bytes=VMEM_LIMIT)


def _in_proj_kernel(h_ref, hb_ref, ha_ref, w_ref, mup_ref, mun_ref, xs_ref, hx_ref, *, seq):
    tm = h_ref.shape[0]
    halo = hb_ref.shape[0]
    i = pl.program_id(0)
    j = pl.program_id(1)
    tiles_per_seq = seq // tm
    first = (i % tiles_per_seq) == 0
    last = (i % tiles_per_seq) == tiles_per_seq - 1

    @pl.when(j == 0)
    def _():
        zero = jnp.zeros_like(hb_ref[...])
        hx_ref[0:halo, :] = jnp.where(first, zero, hb_ref[...])
        hx_ref[halo:halo + tm, :] = h_ref[...]
        hx_ref[halo + tm:, :] = jnp.where(last, zero, ha_ref[...])

    f_ext = lax.dot_general(hx_ref[...], w_ref[...].astype(BF16), (((1,), (1,)), ((), ())),
                            preferred_element_type=F32)
    rows = tm + 2 * halo
    f = f_ext[halo:halo + tm]
    prev = pltpu.roll(f_ext, 1, 0)[halo:halo + tm]
    nxt = pltpu.roll(f_ext, rows - 1, 0)[halo:halo + tm]
    mup = mup_ref[...]
    mun = mun_ref[...]
    xs_ref[...] = (f * (1.0 - mup - mun) + mup * prev + mun * nxt).astype(xs_ref.dtype)


def _in_proj(h, w_t, mup, mun, row0, w_ft, seq, tm=2048, tn=512, halo=16):
    t, d = h.shape
    n_ft = w_ft // tn
    nbh = t // halo
    assert row0 % tn == 0
    return pl.pallas_call(
        functools.partial(_in_proj_kernel, seq=seq),
        grid=(t // tm, n_ft),
        in_specs=[
            pl.BlockSpec((tm, d), lambda i, j: (i, 0), pipeline_mode=pl.Buffered(1)),
            pl.BlockSpec((halo, d), lambda i, j: (jnp.maximum(i * (tm // halo) - 1, 0), 0)),
            pl.BlockSpec((halo, d), lambda i, j: (jnp.minimum((i + 1) * (tm // halo), nbh - 1), 0)),
            pl.BlockSpec((tn, d), lambda i, j: (row0 // tn + j, 0)),
            pl.BlockSpec((1, tn), lambda i, j: (0, j)),
            pl.BlockSpec((1, tn), lambda i, j: (0, j)),
        ],
        out_specs=pl.BlockSpec((tm, tn), lambda i, j: (i, j)),
        out_shape=jax.ShapeDtypeStruct((t, n_ft * tn), BF16),
        scratch_shapes=[pltpu.VMEM((tm + 2 * halo, d), BF16)],
        compiler_params=_cparams("parallel", "arbitrary"),
        name="in_proj",
    )(h, h, h, w_t, mup, mun)


def _gelu(x):
    return 0.5 * x * (1.0 + lax.erf(x * (2.0 ** -0.5)))


def _sgu_kernel(x_ref, g_ref, w_ref, lng_ref, lnb_ref, ws_ref, bs_ref, ga0_ref, ga1_ref, gb0_ref,
                gb1_ref, h_ref, o_ref, wg_ref, wb_ref, *, width, groups):
    @pl.when(pl.program_id(0) == 0)
    def _():
        wb_ref[...] = w_ref[...].astype(wb_ref.dtype)

    cr = ga0_ref.shape[0]
    wg_ref[0, 0:cr, :] = ga0_ref[...].astype(wg_ref.dtype)
    wg_ref[0, cr:, :] = ga1_ref[...].astype(wg_ref.dtype)
    wg_ref[1, 0:cr, :] = gb0_ref[...].astype(wg_ref.dtype)
    wg_ref[1, cr:, :] = gb1_ref[...].astype(wg_ref.dtype)

    tm = x_ref.shape[0]
    nt = (((1,), (1,)), ((), ()))
    x = x_ref[...]
    ms = jnp.mean(x * x, axis=-1, keepdims=True)
    h = (x * lax.rsqrt(ms + NORM_EPS) * g_ref[...]).astype(h_ref.dtype)
    h_ref[...] = h
    v = lax.dot_general(h, wb_ref[width:, :], nt, preferred_element_type=F32)
    gv = _gelu(v)
    mu = jnp.mean(gv, axis=-1, keepdims=True)
    vc = gv - mu
    var = jnp.mean(vc * vc, axis=-1, keepdims=True)
    vn = (vc * lax.rsqrt(var + LN_EPS) * lng_ref[...] + lnb_ref[...]).astype(BF16)
    uv = lax.dot_general(h, wb_ref[0:width, :], nt, preferred_element_type=F32)
    gd = width // groups
    for c in range(tm // SGU_CHUNK):
        rows = slice(c * SGU_CHUNK, (c + 1) * SGU_CHUNK)
        for g in range(groups):
            cols = slice(g * gd, (g + 1) * gd)
            mixed = jnp.dot(ws_ref[g], vn[rows, cols], preferred_element_type=F32)
            gu = _gelu(uv[rows, cols])
            o_ref[rows, cols] = (gu * (mixed + bs_ref[:, cols])).astype(o_ref.dtype)


def _sgu(x, g, w_t, ln_g, ln_b, ws, bs_full, gate_row0, tm=256, cast_rows=32):
    t, d = x.shape
    width = ln_g.shape[1]
    groups = ws.shape[0]
    steps = t // tm
    half = (w_t.shape[0] - gate_row0) // 2
    assert gate_row0 % cast_rows == 0 and half == 2 * cast_rows * steps
    gate_rows = lambda r0, k: pl.BlockSpec((cast_rows, d),
                                           lambda i: (r0 // cast_rows + 2 * i + k, 0))
    return pl.pallas_call(
        functools.partial(_sgu_kernel, width=width, groups=groups),
        grid=(steps,),
        in_specs=[
            pl.BlockSpec((tm, d), lambda i: (i, 0)),
            pl.BlockSpec((1, d), lambda i: (0, 0)),
            pl.BlockSpec((2 * width, d), lambda i: (0, 0), pipeline_mode=pl.Buffered(1)),
            pl.BlockSpec((1, width), lambda i: (0, 0)),
            pl.BlockSpec((1, width), lambda i: (0, 0)),
            pl.BlockSpec(ws.shape, lambda i: (0, 0, 0)),
            pl.BlockSpec(bs_full.shape, lambda i: (0, 0)),
            gate_rows(gate_row0, 0), gate_rows(gate_row0, 1),
            gate_rows(gate_row0 + half, 0), gate_rows(gate_row0 + half, 1),
        ],
        out_specs=[pl.BlockSpec((tm, d), lambda i: (i, 0)),
                   pl.BlockSpec((tm, width), lambda i: (i, 0)),
                   pl.BlockSpec((2, 2 * cast_rows, d), lambda i: (0, i, 0))],
        out_shape=[jax.ShapeDtypeStruct((t, d), BF16), jax.ShapeDtypeStruct((t, width), BF16),
                   jax.ShapeDtypeStruct((2, half, d), BF16)],
        scratch_shapes=[pltpu.VMEM((2 * width, d), BF16)],
        compiler_params=_cparams("arbitrary"),
        name="sgu_mixer",
    )(x, g, w_t, ln_g, ln_b, ws, bs_full, w_t, w_t, w_t, w_t)


def _head_block_ones(scale):
    r = lax.broadcasted_iota(jnp.int32, (LANES, LANES), 0) // HEAD_DIM
    c = lax.broadcasted_iota(jnp.int32, (LANES, LANES), 1) // HEAD_DIM
    return jnp.where(r == c, scale, 0.0).astype(BF16)


def _head_sum(x, bd):
    return jnp.dot(x.astype(BF16), bd, preferred_element_type=F32)


def _prep_kernel(xs_ref, w0_ref, w2_ref, a0_ref, a2_ref, g2_ref, kk_ref, ka_ref, rk_ref,
                 h_ref, wg_ref, sh_ref, lw_ref, bk_ref, bonus_ref, gg_ref, gt_ref, *, width):
    pieces = width // LANES
    gn = gt_ref.shape[1] // pieces

    def gate_piece(c):
        cols = slice(c * gn, (c + 1) * gn)
        gt_ref[:, cols] = jax.nn.sigmoid(
            lax.dot_general(h_ref[...], wg_ref[cols, :], (((1,), (1,)), ((), ())),
                            preferred_element_type=F32)).astype(gt_ref.dtype)

    xs = xs_ref[...].astype(F32)
    r = xs[:, 0:width]
    k = xs[:, width:2 * width]
    v = xs[:, 2 * width:3 * width]
    o = 3 * width
    lo_w = xs[:, o:o + LANES]
    lo_a = xs[:, o + LANES:o + 2 * LANES]
    lo_g = xs[:, o + 2 * LANES:]

    wpre = w0_ref[...] + jnp.dot(jnp.tanh(lo_w).astype(BF16), w2_ref[...],
                                 preferred_element_type=F32)
    lw = -(2.718281828459045 ** -0.5) * jax.nn.sigmoid(wpre)
    a = jax.nn.sigmoid(a0_ref[...] + jnp.dot(lo_a.astype(BF16), a2_ref[...],
                                             preferred_element_type=F32))
    gg_ref[...] = jnp.dot(jax.nn.sigmoid(lo_g).astype(BF16), g2_ref[...],
                          preferred_element_type=F32).astype(gg_ref.dtype)
    lw_ref[0] = lw[:, :width]
    lw_ref[1] = lw[:, width:]

    bd = _head_block_ones(1.0)
    kk = k * kk_ref[...]
    sh_ref[0] = xs_ref[:, 0:width]
    sh_ref[1] = xs_ref[:, 2 * width:3 * width]
    for j in range(width // LANES):
        gate_piece(j)
        cols = slice(j * LANES, (j + 1) * LANES)
        kkj = kk[:, cols]
        ss = _head_sum(kkj * kkj, bd)
        kkn = kkj * lax.rsqrt(jnp.maximum(ss, 1e-24))
        sh_ref[2, :, cols] = kkn.astype(sh_ref.dtype)
        ksum = jnp.zeros_like(kkj)
        for d in range(2):
            dcols = slice(d * width + j * LANES, d * width + (j + 1) * LANES)
            a_d = a[:, dcols]
            k_d = k[:, cols] * (1.0 + (a_d - 1.0) * ka_ref[:, cols])
            bk_ref[d, 0, :, cols] = (kkn * a_d).astype(bk_ref.dtype)
            bk_ref[d, 1, :, cols] = k_d.astype(bk_ref.dtype)
            ksum = ksum + k_d
        bsum = _head_sum(r[:, cols] * ksum * rk_ref[:, cols], bd)
        bonus_ref[:, cols] = (bsum * v[:, cols]).astype(bonus_ref.dtype)


def _rwkv_prep(xs, h, wg, width, w0c, w2c, a0c, a2c, g2p, k_k, k_a, r_k, tm=128):
    t, fw = xs.shape
    _, ngh, d = wg.shape
    r = lambda j, i: 2 * i + j
    row = lambda j, i: (0, 0)
    resident = lambda a: pl.BlockSpec(a.shape, row, pipeline_mode=pl.Buffered(1))
    rows = lambda w: pl.BlockSpec((tm, w), lambda j, i: (r(j, i), 0))
    return pl.pallas_call(
        functools.partial(_prep_kernel, width=width),
        grid=(2, t // (2 * tm)),
        in_specs=[
            rows(fw),
            pl.BlockSpec(w0c.shape, row),
            resident(w2c),
            pl.BlockSpec(a0c.shape, row),
            resident(a2c),
            resident(g2p),
            pl.BlockSpec((1, width), row),
            pl.BlockSpec((1, width), row),
            pl.BlockSpec((1, width), row),
            pl.BlockSpec((2 * tm, d), lambda j, i: (i, 0)),
            pl.BlockSpec((None, ngh, d), lambda j, i: (j, 0, 0), pipeline_mode=pl.Buffered(1)),
        ],
        out_specs=[
            pl.BlockSpec((3, tm, width), lambda j, i: (0, r(j, i), 0)),
            pl.BlockSpec((2, tm, width), lambda j, i: (0, r(j, i), 0)),
            pl.BlockSpec((2, 2, tm, width), lambda j, i: (0, 0, r(j, i), 0)),
            rows(width),
            rows(width),
            pl.BlockSpec((2 * tm, ngh), lambda j, i: (i, j)),
        ],
        out_shape=[
            jax.ShapeDtypeStruct((3, t, width), BF16),
            jax.ShapeDtypeStruct((2, t, width), F32),
            jax.ShapeDtypeStruct((2, 2, t, width), BF16),
            jax.ShapeDtypeStruct((t, width), BF16),
            jax.ShapeDtypeStruct((t, width), BF16),
            jax.ShapeDtypeStruct((t, 2 * ngh), BF16),
        ],
        compiler_params=_cparams("arbitrary", "arbitrary"),
        name="rwkv_prep",
    )(xs, w0c, w2c, a0c, a2c, g2p, k_k, k_a, r_k, h, wg)


def _hat(x, first_half):
    return jnp.concatenate([jnp.where(first_half, x, 0.0), jnp.where(first_half, 0.0, x)],
                           axis=0)


def _scan_kernel(shf_ref, shb_ref, lwf_ref, lwb_ref, bkf_ref, bkb_ref, *rest, group, n_cast,
                 cast_steps):
    cast_in, (of_ref, ob_ref), cast_out, s_ref = (rest[:n_cast], rest[n_cast:n_cast + 2],
                                                  rest[n_cast + 2:2 * n_cast + 2], rest[-1])
    c = SCAN_CHUNK
    batch, width = shf_ref.shape[1], shf_ref.shape[3]
    npair = width // LANES
    assert 2 * c == LANES and c == HEAD_DIM

    @pl.when(pl.program_id(0) == 0)
    def _():
        s_ref[...] = jnp.zeros_like(s_ref)

    @pl.when(pl.program_id(0) < cast_steps)
    def _():
        for src, dst in zip(cast_in, cast_out):
            dst[...] = src[...].astype(dst.dtype)

    bf = lambda t: t.astype(BF16)
    f32 = lambda t: t.astype(F32)
    dg = lambda p, q, dims: lax.dot_general(bf(p), bf(q), (dims, ((), ())),
                                            preferred_element_type=F32)
    dot = lambda p, q: dg(p, q, ((1,), (0,)))
    dot_nt = lambda p, q: dg(p, q, ((1,), (1,)))
    dot_tn = lambda p, q: dg(p, q, ((0,), (0,)))

    wt = lax.broadcasted_iota(jnp.int32, (c, LANES), 0)
    ws = lax.broadcasted_iota(jnp.int32, (c, LANES), 1) % c
    masks = ((wt > ws, wt >= ws), (wt < ws, wt <= ws))
    eye = jnp.where(wt == ws, 1.0, 0.0).astype(F32)
    half = lax.broadcasted_iota(jnp.int32, (c, LANES), 1) < c
    same_head = (lax.broadcasted_iota(jnp.int32, (LANES, LANES), 0) // HEAD_DIM
                 == lax.broadcasted_iota(jnp.int32, (LANES, LANES), 1) // HEAD_DIM)
    hat = lambda t: _hat(t, half)
    refs = ((shf_ref, lwf_ref, bkf_ref, of_ref), (shb_ref, lwb_ref, bkb_ref, ob_ref))

    st = {}
    slots = shf_ref.shape[2] // c

    def prepare(unit):
        slot, d, b, p = unit
        sh_ref, lw_ref, bk_ref, _ = refs[d]
        strict, incl = masks[d]
        cols = slice(p * LANES, (p + 1) * LANES)
        r0 = (slot if d == 0 else slots - 1 - slot) * c
        rows = slice(r0, r0 + c)
        r = f32(sh_ref[0, b, rows, cols])
        v = f32(sh_ref[1, b, rows, cols])
        kk = f32(sh_ref[2, b, rows, cols])
        lw = lw_ref[0, b, rows, cols]
        bb = f32(bk_ref[0, 0, b, rows, cols])
        kd = f32(bk_ref[0, 1, b, rows, cols])
        pre = lw
        s = 1
        while s < c:
            pre = pre + jnp.where(wt >= s, pltpu.roll(pre, s, 0), 0.0)
            s *= 2
        gtot = pre[c - 1:c, :]
        gi = pre if d == 0 else gtot - pre + lw
        en = jnp.exp(-gi)
        ec = jnp.exp(gtot - gi)
        lhs = bf(jnp.concatenate([-kk * jnp.exp(gi - lw), r * jnp.exp(gi)], axis=0))
        rhs = jnp.concatenate([hat(bb * en), hat(kd * en)], axis=0)
        aa = dot_nt(lhs, rhs)
        a_ab = jnp.where(strict, aa[:c, :LANES], 0.0)
        a_ak = jnp.where(strict, aa[:c, LANES:], 0.0)
        ark = bf(jnp.concatenate([jnp.where(incl, aa[c:, :LANES], 0.0),
                                  jnp.where(incl, aa[c:, LANES:], 0.0)], axis=1))
        vh = bf(hat(v))
        sidx = (d * batch + b) * npair + p
        s0 = s_ref[sidx]
        ls = dot_nt(lhs, s0)
        st[unit] = dict(
            sidx=sidx, rows=rows, cols=cols, v=v, s0=s0, ark=ark, vh=vh,
            bk=bf(jnp.concatenate([bb * ec, kd * ec], axis=0)),
            eg=jnp.exp(gtot), pw=a_ab, inv=eye + a_ab,
            rhs_u=ls[:c] + dot(a_ak, vh), rs=ls[c:])

    def square(grp):
        for u in grp:
            st[u]["pw"] = dot(st[u]["pw"], hat(st[u]["pw"]))

    def inverse_round(grp):
        for u in grp:
            e = st[u]
            both = dot(jnp.concatenate([e["pw"], e["inv"]], axis=0), hat(e["pw"]))
            e["pw"] = both[:c]
            e["inv"] = e["inv"] + both[c:]

    def finish_stages(grp):
        def last_product():
            for u in grp:
                st[u]["inv"] = st[u]["inv"] + dot(st[u]["inv"], hat(st[u]["pw"]))

        def solve():
            for u in grp:
                st[u]["u"] = dot(st[u]["inv"], hat(st[u]["rhs_u"]))

        def outputs():
            for u in grp:
                e = st[u]
                o_ref = refs[u[1]][3]
                uvh = jnp.concatenate([bf(hat(e["u"])), e["vh"]], axis=0)
                o_ref[u[2], e["rows"], e["cols"]] = (e["rs"] + dot(e["ark"], uvh)).astype(o_ref.dtype)

        def states():
            for u in grp:
                e = st[u]
                uv = jnp.concatenate([e["u"], e["v"]], axis=0)
                s_ref[e["sidx"]] = (e["s0"] * e["eg"]
                                    + jnp.where(same_head, dot_tn(uv, e["bk"]), 0.0))

        return [last_product, solve, outputs, states]

    units = [(slot, d, b, p) for slot in range(slots) for d in range(2)
             for b in range(batch) for p in range(npair)]
    assert group == batch * npair, "one group per (chunk slot, direction)"
    groups = [units[g0:g0 + group] for g0 in range(0, len(units), group)]
    rounds = 0
    n = 4
    while n < c:
        rounds += 1
        n *= 2
    for u in groups[0]:
        prepare(u)
    for gi_, grp in enumerate(groups):
        nxt = groups[gi_ + 1] if gi_ + 1 < len(groups) else []
        share = -(-len(nxt) // (rounds + 1))
        square(grp)
        for u in nxt[:share]:
            prepare(u)
        for rd in range(rounds):
            inverse_round(grp)
            for u in nxt[(rd + 1) * share:(rd + 2) * share]:
                prepare(u)
        for stage in finish_stages(grp):
            stage()


def _rwkv_scan(sh, lw, bk, batch, seq, to_bf16=(), slots=2):
    _, t, width = sh.shape
    rows = slots * SCAN_CHUNK
    steps = seq // rows
    sh = sh.reshape(3, batch, seq, width)
    lw = lw.reshape(2, batch, seq, width)
    bk = bk.reshape(2, 2, batch, seq, width)
    sh_spec = lambda f: pl.BlockSpec((3, batch, rows, width), lambda i: (0, 0, f(i), 0))
    lw_spec = lambda d, f: pl.BlockSpec((1, batch, rows, width), lambda i: (d, 0, f(i), 0))
    bk_spec = lambda d, f: pl.BlockSpec((1, 2, batch, rows, width), lambda i: (d, 0, 0, f(i), 0))
    o_spec = lambda f: pl.BlockSpec((batch, rows, width), lambda i: (0, f(i), 0))
    fwd = lambda i: i
    bwd = lambda i: steps - 1 - i
    cast_specs = [pl.BlockSpec((a.shape[0] // steps, a.shape[1]), lambda i: (i, 0))
                  for a in to_bf16]
    outs = pl.pallas_call(
        functools.partial(_scan_kernel, group=batch * (width // LANES), n_cast=len(to_bf16),
                          cast_steps=steps),
        grid=(steps,),
        in_specs=[sh_spec(fwd), sh_spec(bwd), lw_spec(0, fwd), lw_spec(1, bwd),
                  bk_spec(0, fwd), bk_spec(1, bwd)] + cast_specs,
        out_specs=[o_spec(fwd), o_spec(bwd)] + cast_specs,
        out_shape=([jax.ShapeDtypeStruct((batch, seq, width), BF16)] * 2
                   + [jax.ShapeDtypeStruct(a.shape, BF16) for a in to_bf16]),
        scratch_shapes=[pltpu.VMEM((2 * batch * (width // LANES), LANES, LANES), F32)],
        compiler_params=_cparams("arbitrary"),
        name="rwkv_scan",
    )(sh, sh, lw, lw, bk, bk, *to_bf16)
    return (outs[0].reshape(t, width), outs[1].reshape(t, width)) + tuple(outs[2:])


def _rwkv_out_rows(of_ref, ob_ref, bonus_ref, gg_ref, gng_ref, gnb_ref):
    width = of_ref.shape[1]
    avg = _head_block_ones(1.0 / HEAD_DIM)
    tiles = []
    for j in range(width // LANES):
        cols = slice(j * LANES, (j + 1) * LANES)
        o = of_ref[:, cols].astype(F32) + ob_ref[:, cols].astype(F32)
        oc = o - _head_sum(o, avg)
        var = _head_sum(oc * oc, avg)
        y = oc * lax.rsqrt(var + GN_EPS) * gng_ref[:, cols] + gnb_ref[:, cols]
        tiles.append(((y + bonus_ref[:, cols].astype(F32))
                      * gg_ref[:, cols].astype(F32)).astype(BF16))
    return jnp.concatenate(tiles, axis=1)


def _mix_out_kernel(ya_ref, of_ref, ob_ref, bonus_ref, gg_ref, gng_ref, gnb_ref, wa_ref, wb_ref,
                    ga_ref, gb_ref, wo_ref, x_ref, g_ref, x1_ref, h2_ref):
    yb = _rwkv_out_rows(of_ref, ob_ref, bonus_ref, gg_ref, gng_ref, gnb_ref)
    pa = jnp.dot(ya_ref[...], wa_ref[...], preferred_element_type=F32)
    pb = jnp.dot(yb, wb_ref[...], preferred_element_type=F32)
    merged = (ga_ref[...].astype(F32) * pa + gb_ref[...].astype(F32) * pb).astype(BF16)
    x1 = x_ref[...] + jnp.dot(merged, wo_ref[...], preferred_element_type=F32)
    x1_ref[...] = x1
    ms = jnp.mean(x1 * x1, axis=-1, keepdims=True)
    h2_ref[...] = (x1 * lax.rsqrt(ms + NORM_EPS) * g_ref[...]).astype(h2_ref.dtype)


def _mix_out(ya, o_f, o_b, bonus, gg, gn_g, gn_b, wa, wb, gates, wo, x, g, tm=256):
    t, k = ya.shape
    d = x.shape[1]
    resident = lambda shape: pl.BlockSpec(shape, lambda i: (0, 0), pipeline_mode=pl.Buffered(1))
    rows = lambda w: pl.BlockSpec((tm, w), lambda i: (i, 0))
    return pl.pallas_call(
        _mix_out_kernel,
        grid=(t // tm,),
        in_specs=[
            rows(k), rows(k), rows(k), rows(k), rows(k),
            pl.BlockSpec((1, k), lambda i: (0, 0)),
            pl.BlockSpec((1, k), lambda i: (0, 0)),
            resident((k, d)),
            resident((k, d)),
            pl.BlockSpec((tm, d), lambda i: (i, 0)),
            pl.BlockSpec((tm, d), lambda i: (i, 1)),
            resident((d, d)),
            rows(d),
            pl.BlockSpec((1, d), lambda i: (0, 0)),
        ],
        out_specs=[rows(d), rows(d)],
        out_shape=[jax.ShapeDtypeStruct((t, d), F32), jax.ShapeDtypeStruct((t, d), BF16)],
        compiler_params=_cparams("parallel"),
        name="mix_out",
    )(ya, o_f, o_b, bonus, gg, gn_g, gn_b, wa, wb, gates, gates, wo, x, g)


def _ffn1_kernel(h_ref, hb_ref, ha_ref, wg_ref, wu_ref, cw_ref, cb_ref, wd_ref, a_ref, wdo_ref,
                 *, seq):
    wdo_ref[...] = wd_ref[...].astype(wdo_ref.dtype)
    tm = h_ref.shape[0]
    halo = hb_ref.shape[0]
    i = pl.program_id(0)
    tiles_per_seq = seq // tm
    first = (i % tiles_per_seq) == 0
    last = (i % tiles_per_seq) == tiles_per_seq - 1

    h = h_ref[...]
    zero = jnp.zeros_like(hb_ref[...])
    h_ext = jnp.concatenate([jnp.where(first, zero, hb_ref[...]), h,
                             jnp.where(last, zero, ha_ref[...])], axis=0)
    g_ext = jnp.dot(h_ext, wg_ref[...].astype(BF16), preferred_element_type=F32)
    up = jnp.dot(h, wu_ref[...].astype(BF16), preferred_element_type=F32)
    rows = tm + 2 * halo
    g = g_ext[halo:halo + tm]
    prev = pltpu.roll(g_ext, 1, 0)[halo:halo + tm]
    nxt = pltpu.roll(g_ext, rows - 1, 0)[halo:halo + tm]
    gt = prev * cw_ref[0:1, :] + g * cw_ref[1:2, :] + nxt * cw_ref[2:3, :] + cb_ref[...]
    a_ref[...] = (jax.nn.silu(gt) * up).astype(a_ref.dtype)


def _ffn1(h, wg, wu, cw, cb, wd, seq, tm=1024, tn=512, halo=16):
    t, d = h.shape
    f = wg.shape[1]
    nbh = t // halo
    nj = f // tn
    wd_rows = wd.shape[0] // ((t // tm) * nj)
    assert wd_rows % 16 == 0 and wd_rows * (t // tm) * nj == wd.shape[0]
    wd_spec = pl.BlockSpec((wd_rows, wd.shape[1]), lambda i, j: (i * nj + j, 0))
    return pl.pallas_call(
        functools.partial(_ffn1_kernel, seq=seq),
        grid=(t // tm, f // tn),
        in_specs=[
            pl.BlockSpec((tm, d), lambda i, j: (i, 0)),
            pl.BlockSpec((halo, d), lambda i, j: (jnp.maximum(i * (tm // halo) - 1, 0), 0)),
            pl.BlockSpec((halo, d), lambda i, j: (jnp.minimum((i + 1) * (tm // halo), nbh - 1), 0)),
            pl.BlockSpec((d, tn), lambda i, j: (0, j)),
            pl.BlockSpec((d, tn), lambda i, j: (0, j)),
            pl.BlockSpec((cw.shape[0], tn), lambda i, j: (0, j)),
            pl.BlockSpec((1, tn), lambda i, j: (0, j)),
            wd_spec,
        ],
        out_specs=[pl.BlockSpec((tm, tn), lambda i, j: (i, j)), wd_spec],
        out_shape=[jax.ShapeDtypeStruct((t, f), BF16), jax.ShapeDtypeStruct(wd.shape, BF16)],
        compiler_params=_cparams("arbitrary", "arbitrary"),
        name="ffn_gate_up",
    )(h, h, h, wg, wu, cw, cb, wd)


def _ffn2_kernel(a_ref, wd_ref, x1_ref, nf_ref, o_ref):
    xf = x1_ref[...] + jnp.dot(a_ref[...], wd_ref[...], preferred_element_type=F32)
    ms = jnp.mean(xf * xf, axis=-1, keepdims=True)
    o_ref[...] = xf * lax.rsqrt(ms + NORM_EPS) * nf_ref[...]


def _ffn2(act, wd, x1, nf, tm=256):
    t, f = act.shape
    d = x1.shape[1]
    return pl.pallas_call(
        _ffn2_kernel,
        grid=(t // tm,),
        in_specs=[
            pl.BlockSpec((tm, f), lambda i: (i, 0)),
            pl.BlockSpec((f, d), lambda i: (0, 0), pipeline_mode=pl.Buffered(1)),
            pl.BlockSpec((tm, d), lambda i: (i, 0)),
            pl.BlockSpec((1, d), lambda i: (0, 0)),
        ],
        out_specs=pl.BlockSpec((tm, d), lambda i: (i, 0)),
        out_shape=jax.ShapeDtypeStruct((t, d), F32),
        compiler_params=_cparams("parallel"),
        name="ffn_down",
    )(act, wd, x1, nf)


def _pad_cols(a, n):
    return jnp.pad(a, ((0, 0), (0, n - a.shape[1])))


def _pad_rows(a, n):
    return jnp.pad(a, ((0, n - a.shape[0]), (0, 0)))


def _layer(x, batch, seq, norm1_g, w_in, sgu_ln_g, sgu_ln_b, sgu_w, sgu_b,
           mu_prev, mu_next, w0_f, w2_f, a0_f, a2_f, w0_b, w2_b, a0_b, a2_b, k_k, k_a, r_k,
           g2, gn_g, gn_b, w_proj_a, w_proj_b, w_out, norm2_g, ffn_w_gate, ffn_w_up,
           ffn_conv_w, ffn_conv_b, ffn_w_down, norm_out_g):
    d = x.shape[1]
    sgu_width = sgu_ln_g.shape[0]
    width = k_k.shape[0]
    dl = w2_f.shape[0]
    al = a2_f.shape[0]
    feat_w = mu_prev.shape[0]
    feat_pad = -(-feat_w // (4 * LANES)) * (4 * LANES)
    assert dl == HEAD_DIM and al == HEAD_DIM and 2 * dl == LANES
    row = lambda a: a.reshape(1, -1)

    o1 = 2 * sgu_width
    o2 = o1 + feat_w
    w_t = jnp.swapaxes(w_in, 0, 1)
    zw = jnp.zeros_like(w2_f)
    w2c = jnp.concatenate([jnp.concatenate([w2_f, zw], 1),
                           jnp.concatenate([zw, w2_b], 1)], 0).astype(BF16)
    za = jnp.zeros_like(a2_f)
    a2c = jnp.concatenate([jnp.concatenate([a2_f, za], 1),
                           jnp.concatenate([za, a2_b], 1)], 0).astype(BF16)
    g2p = _pad_rows(g2, feat_pad - 3 * width - 2 * LANES).astype(BF16)
    w0c = row(jnp.concatenate([w0_f, w0_b]))
    a0c = row(jnp.concatenate([a0_f, a0_b]))
    mup = _pad_cols(row(mu_prev), feat_pad)
    mun = _pad_cols(row(mu_next), feat_pad)
    bs_full = jnp.repeat(sgu_b.T, sgu_width // sgu_b.shape[0], axis=1)

    h1, y_a, w_gates = _sgu(x, row(norm1_g), w_t, row(sgu_ln_g), row(sgu_ln_b),
                            sgu_w.astype(BF16), bs_full, o2)
    xs = _in_proj(h1, w_t, mup, mun, o1, feat_pad, seq)

    sh, lw, bk, bonus, gg, gates = _rwkv_prep(xs, h1, w_gates, width, w0c, w2c, a0c, a2c, g2p,
                                              row(k_k), row(k_a), row(r_k))
    o_f, o_b, wa_bf, wb_bf, wo_bf = _rwkv_scan(
        sh, lw, bk, batch, seq, to_bf16=(w_proj_a, w_proj_b, w_out))

    x1, h2 = _mix_out(y_a, o_f, o_b, bonus, gg, row(gn_g), row(gn_b), wa_bf, wb_bf, gates, wo_bf,
                      x, row(norm2_g))

    act, wd_bf = _ffn1(h2, ffn_w_gate, ffn_w_up, ffn_conv_w, row(ffn_conv_b), ffn_w_down, seq)
    return _ffn2(act, wd_bf, x1, norm_out_g)


def kernel(x, norm1_g, w_in, sgu_ln_g, sgu_ln_b, sgu_w, sgu_b, rwkv_mu_prev, rwkv_mu_next, rwkv_w0_f, rwkv_w2_f, rwkv_a0_f, rwkv_a2_f, rwkv_w0_b, rwkv_w2_b, rwkv_a0_b, rwkv_a2_b, rwkv_k_k, rwkv_k_a, rwkv_r_k, rwkv_g2, rwkv_gn_g, rwkv_gn_b, w_proj_a, w_proj_b, w_out, norm2_g, ffn_w_gate, ffn_w_up, ffn_conv_w, ffn_conv_b, ffn_w_down, norm_f_g):
    batch, seq, d = x.shape
    depth = norm1_g.shape[0]
    assert depth == 1, "the fused final RMSNorm assumes a single layer"
    xf = x.reshape(batch * seq, d)
    per_layer = (norm1_g, w_in, sgu_ln_g, sgu_ln_b, sgu_w, sgu_b, rwkv_mu_prev, rwkv_mu_next,
                 rwkv_w0_f, rwkv_w2_f, rwkv_a0_f, rwkv_a2_f, rwkv_w0_b, rwkv_w2_b, rwkv_a0_b,
                 rwkv_a2_b, rwkv_k_k, rwkv_k_a, rwkv_r_k, rwkv_g2, rwkv_gn_g, rwkv_gn_b,
                 w_proj_a, w_proj_b, w_out, norm2_g, ffn_w_gate, ffn_w_up, ffn_conv_w,
                 ffn_conv_b, ffn_w_down)
    out = _layer(xf, batch, seq, *(p[0] for p in per_layer), norm_f_g.reshape(1, d))
    return out.reshape(batch, seq, d)
```
